```python
import jax, jax.numpy as jnp
from jax import lax
import numpy as np

D_MODEL = 1024
BATCH = 4
SEQ = 4096
DEPTH = 2

MLA_HEADS = 8
MLA_NOPE = 64
MLA_ROPE = 32
MLA_V = 64
Q_LORA = 384
KV_LORA = 256
DSA_HEADS = 8
DSA_HEAD_DIM = 64
IDX_HEADS = 8
IDX_DIM = 32
TOPK_MAX = 256
D_FF = 2816
N_EXPERTS = 8
TOP_K = 2
D_FF_EXPERT = 3584
ROPE_THETA = 10000.0
Q_BLOCK = 128
NORM_EPS = 1e-5
MAX_POS_OFFSET = 1024
ALPHA = (2 * DEPTH) ** 0.25
BETA = (8 * DEPTH) ** -0.25
N_DENSE = (DEPTH + 1) // 2
N_MOE = DEPTH // 2

IN_SIZES = (
    Q_LORA,
    KV_LORA,
    MLA_ROPE,
    DSA_HEADS * DSA_HEAD_DIM,
    DSA_HEADS * DSA_HEAD_DIM,
    DSA_HEADS * DSA_HEAD_DIM,
    IDX_HEADS * IDX_DIM,
    IDX_DIM,
    IDX_HEADS,
    2 * D_MODEL,
)
D_IN = sum(IN_SIZES)

kernel_name = "hybrid_mla_dsa_deepnorm_moe"


def layer_norm(x, g, b):
    xf = x.astype(jnp.float32)
    mu = jnp.mean(xf, axis=-1, keepdims=True)
    var = jnp.mean(jnp.square(xf - mu), axis=-1, keepdims=True)
    return ((xf - mu) * lax.rsqrt(var + NORM_EPS) * g.astype(jnp.float32) + b.astype(jnp.float32)).astype(x.dtype)


def rms_norm(x, g):
    xf = x.astype(jnp.float32)
    ms = jnp.mean(jnp.square(xf), axis=-1, keepdims=True)
    return (xf * lax.rsqrt(ms + NORM_EPS) * g.astype(jnp.float32)).astype(x.dtype)


def rope(x, pos):
    d = x.shape[-1]
    inv = jnp.power(ROPE_THETA, -jnp.arange(0, d, 2, dtype=jnp.float32) / d)
    ang = pos.astype(jnp.float32)[:, :, None, None] * inv
    c, s = jnp.cos(ang), jnp.sin(ang)
    xf = x.astype(jnp.float32)
    x1, x2 = xf[..., : d // 2], xf[..., d // 2:]
    return jnp.concatenate([x1 * c - x2 * s, x1 * s + x2 * c], axis=-1).astype(x.dtype)


def to_blocks(a):
    B, S = a.shape[:2]
    return jnp.moveaxis(a.reshape((B, S // Q_BLOCK, Q_BLOCK) + a.shape[2:]), 1, 0)


def from_blocks(a):
    nb, B, Q = a.shape[:3]
    return jnp.moveaxis(a, 0, 1).reshape((B, nb * Q) + a.shape[3:])


def split_cols(a):
    idx = np.cumsum(IN_SIZES)[:-1].tolist()
    return jnp.split(a, idx, axis=-1)


def dense_causal_attention(q, k, v):
    S = q.shape[1]
    scale = q.shape[-1] ** -0.5
    key_pos = jnp.arange(S)

    def block(args):
        qb, start = args
        q_pos = start + jnp.arange(Q_BLOCK)
        logits = jnp.einsum('bqhd,bkhd->bhqk', qb, k).astype(jnp.float32) * scale
        logits = jnp.where(key_pos[None, :] <= q_pos[:, None], logits, -jnp.inf)
        p = jax.nn.softmax(logits, axis=-1).astype(v.dtype)
        return jnp.einsum('bhqk,bkhd->bqhd', p, v)

    starts = jnp.arange(0, S, Q_BLOCK)
    return from_blocks(lax.map(block, (to_blocks(q), starts)))


def mla_branch(c_q, c_kv, k_r, pos, g_q, w_uq, g_kv, w_ukv):
    B, S, _ = c_q.shape
    q = (rms_norm(c_q, g_q) @ w_uq).reshape(B, S, MLA_HEADS, MLA_NOPE + MLA_ROPE)
    q = jnp.concatenate([q[..., :MLA_NOPE], rope(q[..., MLA_NOPE:], pos)], axis=-1)
    kv = (rms_norm(c_kv, g_kv) @ w_ukv).reshape(B, S, MLA_HEADS, MLA_NOPE + MLA_V)
    k_nope, v = kv[..., :MLA_NOPE], kv[..., MLA_NOPE:]
    k_rope = rope(k_r[:, :, None, :], pos)
    k = jnp.concatenate([k_nope, jnp.broadcast_to(k_rope, (B, S, MLA_HEADS, MLA_ROPE))], axis=-1)
    o = dense_causal_attention(q, k, v)
    return o.reshape(B, S, MLA_HEADS * MLA_V)


def dsa_branch(q, k, v, q_i, k_i, w_i, pos):
    B, S, _ = q.shape
    n_sel = min(TOPK_MAX, S // 4)
    scale = DSA_HEAD_DIM ** -0.5
    q = rope(q.reshape(B, S, DSA_HEADS, DSA_HEAD_DIM), pos)
    k = rope(k.reshape(B, S, DSA_HEADS, DSA_HEAD_DIM), pos)
    v = v.reshape(B, S, DSA_HEADS, DSA_HEAD_DIM)
    q_i = rope(q_i.reshape(B, S, IDX_HEADS, IDX_DIM), pos)
    k_i = rope(k_i[:, :, None, :], pos)[:, :, 0, :]
    w_i = w_i * ((IDX_HEADS * IDX_DIM) ** -0.5)
    key_pos = jnp.arange(S)
    gather = jax.vmap(lambda a, ix: a[ix])

    def block(args):
        qb, qib, wib, start = args
        q_pos = start + jnp.arange(Q_BLOCK)
        dots = jnp.einsum('bqhd,bsd->bqhs', qib, k_i).astype(jnp.float32)
        score = jnp.einsum('bqh,bqhs->bqs', wib.astype(jnp.float32), jax.nn.relu(dots))
        score = jnp.where(key_pos[None, :] <= q_pos[:, None], score, -jnp.inf)
        _, idx = lax.top_k(score, n_sel)
        valid = idx <= q_pos[None, :, None]
        k_sel = gather(k, idx)
        v_sel = gather(v, idx)
        logits = jnp.einsum('bqhd,bqnhd->bqhn', qb, k_sel).astype(jnp.float32) * scale
        logits = jnp.where(valid[:, :, None, :], logits, -jnp.inf)
        p = jax.nn.softmax(logits, axis=-1).astype(v.dtype)
        return jnp.einsum('bqhn,bqnhd->bqhd', p, v_sel)

    starts = jnp.arange(0, S, Q_BLOCK)
    o = from_blocks(lax.map(block, (to_blocks(q), to_blocks(q_i), to_blocks(w_i), starts)))
    return o.reshape(B, S, DSA_HEADS * DSA_HEAD_DIM)


def mixer_sublayer(x, pos, w_in, g_q, w_uq, g_kv, w_ukv, w_o_mla, w_o_dsa, w_out):
    c_q, c_kv, k_r, dq, dk, dv, iq, ik, iw, gates = split_cols(x @ w_in)
    y_a = mla_branch(c_q, c_kv, k_r, pos, g_q, w_uq, g_kv, w_ukv) @ w_o_mla
    y_b = dsa_branch(dq, dk, dv, iq, ik, iw, pos) @ w_o_dsa
    g = jax.nn.sigmoid(gates.astype(jnp.float32)).astype(x.dtype)
    merged = g[..., :D_MODEL] * y_a + g[..., D_MODEL:] * y_b
    return merged @ w_out


def swiglu(x, w1, w3, w2):
    return (jax.nn.silu(x @ w1) * (x @ w3)) @ w2


def moe_ffn(x, w_router, w1, w3, w2):
    B, S, D = x.shape
    t = x.reshape(B * S, D)
    logits = (t @ w_router).astype(jnp.float32)
    top_val, top_idx = lax.top_k(logits, TOP_K)
    top_w = jax.nn.softmax(top_val, axis=-1)
    comb = jnp.sum(jax.nn.one_hot(top_idx, N_EXPERTS, dtype=jnp.float32) * top_w[..., None], axis=1)
    comb = comb.astype(t.dtype)
    y = jnp.zeros_like(t)
    for e in range(N_EXPERTS):
        y = y + comb[:, e:e + 1] * swiglu(t, w1[e], w3[e], w2[e])
    return y.reshape(B, S, D)


def setup_inputs(seed: int = 0) -> dict:
    key = jax.random.key(seed)
    ks = jax.random.split(key, 24)

    def nrm(k, shape, fan_in, s=1.0):
        return jax.random.normal(k, shape, jnp.float32) * (s * fan_in ** -0.5)

    def gain(k, shape):
        return 1.0 + 0.01 * jax.random.normal(k, shape, jnp.float32)

    x = jax.random.normal(ks[0], (BATCH, SEQ, D_MODEL), jnp.float32)
    offset = jax.random.randint(ks[1], (BATCH, 1), 0, MAX_POS_OFFSET, dtype=jnp.int32)
    positions = offset + jnp.arange(SEQ, dtype=jnp.int32)[None, :]
    return {
        "x": x,
        "positions": positions,
        "w_in": nrm(ks[2], (DEPTH, D_MODEL, D_IN), D_MODEL),
        "mla_q_norm": gain(ks[3], (DEPTH, Q_LORA)),
        "w_uq": nrm(ks[4], (DEPTH, Q_LORA, MLA_HEADS * (MLA_NOPE + MLA_ROPE)), Q_LORA),
        "mla_kv_norm": gain(ks[5], (DEPTH, KV_LORA)),
        "w_ukv": nrm(ks[6], (DEPTH, KV_LORA, MLA_HEADS * (MLA_NOPE + MLA_V)), KV_LORA),
        "w_o_mla": nrm(ks[7], (DEPTH, MLA_HEADS * MLA_V, D_MODEL), MLA_HEADS * MLA_V, BETA),
        "w_o_dsa": nrm(ks[8], (DEPTH, DSA_HEADS * DSA_HEAD_DIM, D_MODEL), DSA_HEADS * DSA_HEAD_DIM, BETA),
        "w_out": nrm(ks[9], (DEPTH, D_MODEL, D_MODEL), D_MODEL, BETA),
        "ln1_g": gain(ks[10], (DEPTH, D_MODEL)),
        "ln1_b": 0.01 * jax.random.normal(ks[11], (DEPTH, D_MODEL), jnp.float32),
        "ln2_g": gain(ks[12], (DEPTH, D_MODEL)),
        "ln2_b": 0.01 * jax.random.normal(ks[13], (DEPTH, D_MODEL), jnp.float32),
        "dense_w1": nrm(ks[14], (N_DENSE, D_MODEL, D_FF), D_MODEL),
        "dense_w3": nrm(ks[15], (N_DENSE, D_MODEL, D_FF), D_MODEL),
        "dense_w2": nrm(ks[16], (N_DENSE, D_FF, D_MODEL), D_FF, BETA),
        "moe_router": nrm(ks[17], (N_MOE, D_MODEL, N_EXPERTS), D_MODEL),
        "moe_w1": nrm(ks[18], (N_MOE, N_EXPERTS, D_MODEL, D_FF_EXPERT), D_MODEL),
        "moe_w3": nrm(ks[19], (N_MOE, N_EXPERTS, D_MODEL, D_FF_EXPERT), D_MODEL),
        "moe_w2": nrm(ks[20], (N_MOE, N_EXPERTS, D_FF_EXPERT, D_MODEL), D_FF_EXPERT, BETA),
    }


def reference(x, positions, w_in, mla_q_norm, w_uq, mla_kv_norm, w_ukv, w_o_mla, w_o_dsa, w_out,
              ln1_g, ln1_b, ln2_g, ln2_b, dense_w1, dense_w3, dense_w2,
              moe_router, moe_w1, moe_w3, moe_w2):
    for l in range(DEPTH):
        h = mixer_sublayer(x, positions, w_in[l], mla_q_norm[l], w_uq[l], mla_kv_norm[l], w_ukv[l],
                           w_o_mla[l], w_o_dsa[l], w_out[l])
        x = layer_norm(ALPHA * x + h, ln1_g[l], ln1_b[l])
        if l % 2 == 0:
            f = swiglu(x, dense_w1[l // 2], dense_w3[l // 2], dense_w2[l // 2])
        else:
            f = moe_ffn(x, moe_router[l // 2], moe_w1[l // 2], moe_w3[l // 2], moe_w2[l // 2])
        x = layer_norm(ALPHA * x + f, ln2_g[l], ln2_b[l])
    return x
```

```python
import functools

import numpy as np
import jax
import jax.numpy as jnp
from jax import lax
from jax.experimental import pallas as pl
from jax.experimental.pallas import tpu as pltpu

F32 = jnp.float32
BF16 = jnp.bfloat16
LANES = 128
VMEM_LIMIT = 56 * 1024 * 1024

MLA_HEADS = 8
MLA_NOPE = 64
MLA_ROPE = 32
MLA_V = 64
Q_LORA = 384
KV_LORA = 256
DSA_HEADS = 8
DSA_HEAD_DIM = 64
IDX_HEADS = 8
IDX_DIM = 32
TOPK_MAX = 256
N_EXPERTS = 8
ROPE_THETA = 10000.0
NORM_EPS = 1e-5
NEG = -1e30
INT_MIN = np.int32(-2 ** 31)

C_CQ = 0
C_CKV = C_CQ + Q_LORA
C_KR = C_CKV + KV_LORA
C_DQ = C_KR + LANES
C_DK = C_DQ + DSA_HEADS * LANES
C_DV = C_DK + DSA_HEADS * LANES
C_IQ = C_DV + DSA_HEADS * LANES
C_IK = C_IQ + IDX_HEADS * IDX_DIM
C_IW = C_IK + LANES
C_END = C_IW + LANES


def _cparams(sem):
    return pltpu.CompilerParams(dimension_semantics=sem, vmem_limit_bytes=VMEM_LIMIT)


def _const_spec(shape):
    nd = len(shape)
    return pl.BlockSpec(shape, lambda *_: (0,) * nd)


def _nt_dot(a, b):
    return lax.dot_general(a, b, (((1,), (1,)), ((), ())), preferred_element_type=F32)


def _layer_norm(z, g, b):
    mu = jnp.mean(z, axis=-1, keepdims=True)
    zc = z - mu
    var = jnp.mean(zc * zc, axis=-1, keepdims=True)
    return zc * lax.rsqrt(var + NORM_EPS) * g + b


def _rope_tables_kernel(pos_ref, inv_ref, sgn_ref, cos_ref, sin_ref):
    pos = pos_ref[...].astype(F32)
    for p in range(2):
        ang = pos * inv_ref[p:p + 1, :]
        cos_ref[p] = jnp.cos(ang)
        sin_ref[p] = jnp.sin(ang) * sgn_ref[p:p + 1, :]


def _rope_tables(positions):
    T = positions.size
    tm = 1024
    inv32 = jnp.power(ROPE_THETA, -jnp.arange(0, IDX_DIM, 2, dtype=F32) / IDX_DIM)
    inv64 = jnp.power(ROPE_THETA, -jnp.arange(0, DSA_HEAD_DIM, 2, dtype=F32) / DSA_HEAD_DIM)
    inv = jnp.stack([jnp.tile(inv32, LANES // 16),
                     jnp.concatenate([inv64, inv64, jnp.zeros((LANES - 64,), F32)])])
    lane = np.arange(LANES)
    sgn = jnp.asarray(np.stack([np.where(lane % 32 < 16, -1.0, 1.0),
                                np.where(lane < 32, -1.0, np.where(lane < 64, 1.0, 0.0))]), F32)
    out = jax.ShapeDtypeStruct((2, T, LANES), F32)
    return pl.pallas_call(
        _rope_tables_kernel,
        out_shape=(out, out),
        grid=(T // tm,),
        in_specs=[pl.BlockSpec((tm, 1), lambda i: (i, 0)), _const_spec((2, LANES)), _const_spec((2, LANES))],
        out_specs=(pl.BlockSpec((2, tm, LANES), lambda i: (0, i, 0)),) * 2,
        compiler_params=_cparams(("parallel",)),
        name="rope_tables",
    )(positions.reshape(T, 1), inv, sgn)


def _proj_kernel(x_ref, cos_ref, sin_ref, w_in_ref, w_uq_ref, w_ukv_ref, gq_ref, gkv_ref,
                 qm_ref, km_ref, vm_ref, dq_ref, dk_ref, dv_ref, iq_ref, ik_ref, iw_ref, *, mla_scale):
    xb = x_ref[...].astype(BF16)
    lane = lax.broadcasted_iota(jnp.int32, (1, LANES), 1)
    cos32, sin32 = cos_ref[0], sin_ref[0]
    cos64, sin64 = cos_ref[1], sin_ref[1]
    in_rope = (lane >> 5) == (MLA_NOPE >> 5)
    cos_m = jnp.where(in_rope, cos32, 1.0)
    sin_m = jnp.where(in_rope, sin32, 0.0)
    first32 = (lane & (IDX_DIM - 1)) < IDX_DIM // 2
    first64 = lane < DSA_HEAD_DIM // 2

    def proj(lo, hi):
        return jnp.dot(xb, w_in_ref[:, lo:hi], preferred_element_type=F32)

    def rope(xs, cos, sin, half, first):
        rot = jnp.where(first, pltpu.roll(xs, LANES - half, 1), pltpu.roll(xs, half, 1))
        return xs * cos + rot * sin

    def rms(c, g):
        ms = jnp.mean(c * c, axis=-1, keepdims=True)
        return c * lax.rsqrt(ms + NORM_EPS) * g

    q = jnp.dot(rms(proj(C_CQ, C_CKV), gq_ref[...]).astype(BF16), w_uq_ref[...], preferred_element_type=F32)
    kv = jnp.dot(rms(proj(C_CKV, C_KR), gkv_ref[...]).astype(BF16), w_ukv_ref[...], preferred_element_type=F32)
    kr = rope(proj(C_KR, C_DQ), cos_m, sin_m, MLA_ROPE // 2, first32)
    for h in range(MLA_HEADS):
        sl = slice(h * LANES, (h + 1) * LANES)
        qm_ref[:, sl] = (rope(q[:, sl], cos_m, sin_m, MLA_ROPE // 2, first32) * mla_scale).astype(BF16)
        km_ref[:, sl] = (kv[:, sl] + kr).astype(BF16)
    vm_ref[...] = kv[:, MLA_HEADS * LANES:].astype(BF16)

    dq = proj(C_DQ, C_DK)
    dk = proj(C_DK, C_DV)
    for h in range(DSA_HEADS):
        sl = slice(h * LANES, (h + 1) * LANES)
        dq_ref[:, sl] = rope(dq[:, sl], cos64, sin64, DSA_HEAD_DIM // 2, first64).astype(BF16)
        dk_ref[:, sl] = rope(dk[:, sl], cos64, sin64, DSA_HEAD_DIM // 2, first64).astype(BF16)
    dv_ref[...] = proj(C_DV, C_IQ).astype(BF16)

    iq = proj(C_IQ, C_IK)
    for g in range(IDX_HEADS * IDX_DIM // LANES):
        sl = slice(g * LANES, (g + 1) * LANES)
        iq_ref[:, sl] = rope(iq[:, sl], cos32, sin32, IDX_DIM // 2, first32).astype(BF16)
    ik_ref[...] = rope(proj(C_IK, C_IW), cos32, sin32, IDX_DIM // 2, first32).astype(BF16)
    iw_ref[...] = proj(C_IW, C_END)


def _pad_heads(w, nh, d):
    k = w.shape[0]
    return jnp.pad(w.reshape(k, nh, d), ((0, 0), (0, 0), (0, LANES - d))).reshape(k, nh * LANES)


def _prep_in_weights(w_in, w_uq, w_ukv):
    d = w_in.shape[0]
    o = np.cumsum([0, Q_LORA, KV_LORA, MLA_ROPE, 512, 512, 512, IDX_HEADS * IDX_DIM, IDX_DIM, IDX_HEADS])
    c_q, c_kv, k_r, dq, dk, dv, iq, ik, iw = [w_in[:, o[i]:o[i + 1]] for i in range(9)]
    gates = w_in[:, o[9]:]
    kr_tile = jnp.pad(k_r, ((0, 0), (MLA_NOPE, LANES - MLA_NOPE - MLA_ROPE)))
    dsa_scale = DSA_HEAD_DIM ** -0.5
    idx_scale = (IDX_HEADS * IDX_DIM) ** -0.5
    w_in_p = jnp.concatenate([
        c_q, c_kv, kr_tile,
        _pad_heads(dq * dsa_scale, DSA_HEADS, DSA_HEAD_DIM),
        _pad_heads(dk, DSA_HEADS, DSA_HEAD_DIM),
        _pad_heads(dv, DSA_HEADS, DSA_HEAD_DIM),
        iq, jnp.tile(ik, (1, LANES // IDX_DIM)),
        jnp.pad(iw * idx_scale, ((0, 0), (0, LANES - IDX_HEADS))),
    ], axis=1).astype(BF16)
    assert w_in_p.shape == (d, C_END)
    w_uq_p = _pad_heads(w_uq, MLA_HEADS, MLA_NOPE + MLA_ROPE).astype(BF16)
    ukv = w_ukv.reshape(KV_LORA, MLA_HEADS, MLA_NOPE + MLA_V)
    w_ukv_p = jnp.concatenate([
        _pad_heads(ukv[:, :, :MLA_NOPE].reshape(KV_LORA, -1), MLA_HEADS, MLA_NOPE),
        _pad_heads(ukv[:, :, MLA_NOPE:].reshape(KV_LORA, -1), MLA_HEADS, MLA_V),
    ], axis=1).astype(BF16)
    return w_in_p, w_uq_p, w_ukv_p, gates.astype(BF16)


def _projections(x2, cos, sin, w_in_p, w_uq_p, w_ukv_p, g_q, g_kv):
    T, D = x2.shape
    tm = 256
    wide = MLA_HEADS * LANES
    row = lambda w: pl.BlockSpec((tm, w), lambda i: (i, 0))
    shapes = [(wide, BF16)] * 6 + [(IDX_HEADS * IDX_DIM, BF16), (LANES, BF16), (LANES, F32)]
    return pl.pallas_call(
        functools.partial(_proj_kernel, mla_scale=(MLA_NOPE + MLA_ROPE) ** -0.5),
        out_shape=tuple(jax.ShapeDtypeStruct((T, w), dt) for w, dt in shapes),
        grid=(T // tm,),
        in_specs=[row(D),
                  pl.BlockSpec((2, tm, LANES), lambda i: (0, i, 0)),
                  pl.BlockSpec((2, tm, LANES), lambda i: (0, i, 0)),
                  _const_spec(w_in_p.shape), _const_spec(w_uq_p.shape), _const_spec(w_ukv_p.shape),
                  _const_spec((1, Q_LORA)), _const_spec((1, KV_LORA))],
        out_specs=tuple(row(w) for w, _ in shapes),
        compiler_params=_cparams(("parallel",)),
        name="projections",
    )(x2, cos, sin, w_in_p, w_uq_p, w_ukv_p, g_q.reshape(1, -1), g_kv.reshape(1, -1))


def _softmax_step(s, v, m_sc, l_sc, acc_sc):
    m_prev = m_sc[...]
    m_new = jnp.maximum(m_prev, jnp.max(s, axis=1, keepdims=True))
    alpha = jnp.exp(m_prev - m_new)
    p = jnp.exp(s - m_new)
    l_sc[...] = alpha * l_sc[...] + jnp.sum(p, axis=1, keepdims=True)
    acc_sc[...] = alpha * acc_sc[...] + jnp.dot(p.astype(BF16), v, preferred_element_type=F32)
    m_sc[...] = m_new


def _mla_attn_kernel(q_ref, k_ref, v_ref, o_ref, m_sc, l_sc, acc_sc, *, tq):
    qi = pl.program_id(2)
    q = q_ref[...]
    m_sc[...] = jnp.full(m_sc.shape, NEG, F32)
    l_sc[...] = jnp.zeros(l_sc.shape, F32)
    acc_sc[...] = jnp.zeros(acc_sc.shape, F32)

    def chunk(j, masked):
        off = pl.multiple_of(j * tq, tq)
        s = _nt_dot(q, k_ref[pl.ds(off, tq), :])
        if masked:
            row = lax.broadcasted_iota(jnp.int32, (tq, tq), 0)
            col = lax.broadcasted_iota(jnp.int32, (tq, tq), 1)
            s = jnp.where(col <= row, s, NEG)
        _softmax_step(s, v_ref[pl.ds(off, tq), :], m_sc, l_sc, acc_sc)

    def body(j, c):
        chunk(j, False)
        return c

    lax.fori_loop(0, qi, body, 0)
    chunk(qi, True)
    o_ref[...] = (acc_sc[...] / l_sc[...]).astype(o_ref.dtype)


def _mla_attention(q, k, v):
    B, S, W = q.shape
    tq = 512
    blk_q = pl.BlockSpec((None, tq, LANES), lambda b, h, i: (b, i, h))
    blk_kv = pl.BlockSpec((None, S, LANES), lambda b, h, i: (b, 0, h))
    return pl.pallas_call(
        functools.partial(_mla_attn_kernel, tq=tq),
        out_shape=jax.ShapeDtypeStruct((B, S, W), BF16),
        grid=(B, W // LANES, S // tq),
        in_specs=[blk_q, blk_kv, blk_kv],
        out_specs=blk_q,
        scratch_shapes=[pltpu.VMEM((tq, 1), F32), pltpu.VMEM((tq, 1), F32), pltpu.VMEM((tq, LANES), F32)],
        compiler_params=_cparams(("parallel", "parallel", "arbitrary")),
        name="mla_attention",
    )(q, k, v)


def _dsa_kernel(dq_ref, dk_ref, dv_ref, iq_ref, ik_ref, iw_ref, o_ref,
                key_sc, bias_sc, wb_sc, qm_sc, tau_sc, msel_sc, m_sc, l_sc, acc_sc,
                *, tq, ck, n_sel, seq_len):
    qi = pl.program_id(1)
    nch = (qi * tq + tq + ck - 1) // ck
    nsl = ck // LANES
    lane = lax.broadcasted_iota(jnp.int32, (1, LANES), 1)
    qpos_b = qi * tq + lax.broadcasted_iota(jnp.int32, (tq, LANES), 0)
    k_sel = float(n_sel)

    for h in range(IDX_HEADS):
        wb_sc[h] = jnp.broadcast_to(iw_ref[:, h:h + 1], (tq, LANES))
        g, lo = divmod(h * IDX_DIM, LANES)
        qg = iq_ref[:, g * LANES:(g + 1) * LANES].astype(F32)
        in_head = (lane >> 5) == (lo >> 5)
        qm_sc[h] = jnp.where(in_head, qg, 0.0).astype(BF16)

    def score_body(c, carry):
        kc = ik_ref[pl.ds(pl.multiple_of(c * ck, ck), ck), :]
        for s in range(nsl):
            ks = kc[s * LANES:(s + 1) * LANES, :]
            score = jnp.zeros((tq, LANES), F32)
            for h in range(IDX_HEADS):
                score = score + wb_sc[h] * jnp.maximum(_nt_dot(qm_sc[h], ks), 0.0)
            bits = lax.bitcast_convert_type(score, jnp.int32)
            key = jnp.where(bits >= 0, bits, bits ^ np.int32(0x7FFFFFFF))
            kpos = c * ck + s * LANES + lane
            key_sc[c, :, s * LANES:(s + 1) * LANES] = jnp.where(kpos <= qpos_b, key, INT_MIN)
        return carry

    lax.fori_loop(0, nch, score_body, 0)

    def count(pred):
        def body(c, acc):
            for s in range(nsl):
                acc = acc + pred(key_sc[c, :, s * LANES:(s + 1) * LANES], c * ck + s * LANES + lane)
            return acc
        acc = lax.fori_loop(0, nch, body, jnp.zeros((tq, LANES), F32))
        return jnp.sum(acc, axis=1, keepdims=True)

    def bcast(col):
        return jnp.broadcast_to(col, (tq, LANES))

    n_nonneg = count(lambda blk, kpos: jnp.where(blk >= 0, 1.0, 0.0))
    tau0 = jnp.where(n_nonneg >= k_sel, np.int32(0), INT_MIN)

    def tau_step(i, tau):
        cand = bcast(tau | (jnp.int32(1) << (30 - i)))
        cnt = count(lambda blk, kpos: jnp.where(blk >= cand, 1.0, 0.0))
        return jnp.where(cnt >= k_sel, cand[:, :1], tau)

    tau = lax.fori_loop(0, 31, tau_step, tau0)
    tau_b = bcast(tau)
    n_ge = count(lambda blk, kpos: jnp.where(blk >= tau_b, 1.0, 0.0))
    n_gt = count(lambda blk, kpos: jnp.where(blk > tau_b, 1.0, 0.0))
    need = k_sel - n_gt
    tau_sc[...] = tau
    msel_sc[...] = jnp.full((tq, 1), seq_len - 1, jnp.int32)

    @pl.when(jnp.max(n_ge) > k_sel)
    def _():
        tb = bcast(tau_sc[...])

        def m_step(i, m):
            cand = bcast(m | (jnp.int32(1) << (seq_len.bit_length() - 2 - i)))
            cnt = count(lambda blk, kpos: jnp.where(blk == tb, jnp.where(kpos < cand, 1.0, 0.0), 0.0))
            return jnp.where(cnt < need, cand[:, :1], m)

        msel_sc[...] = lax.fori_loop(0, seq_len.bit_length() - 1, m_step, jnp.zeros((tq, 1), jnp.int32))

    tau_b = bcast(tau_sc[...])
    msel_b = bcast(msel_sc[...])

    def bias_body(c, carry):
        for s in range(nsl):
            blk = key_sc[c, :, s * LANES:(s + 1) * LANES]
            kpos = c * ck + s * LANES + lane
            tie = jnp.where(blk == tau_b, jnp.where(kpos <= msel_b, 0.0, NEG), NEG)
            sel = jnp.where(blk > tau_b, 0.0, tie)
            bias_sc[c, :, s * LANES:(s + 1) * LANES] = jnp.where(kpos <= qpos_b, sel, NEG)
        return carry

    lax.fori_loop(0, nch, bias_body, 0)

    for h in range(DSA_HEADS):
        sl = slice(h * LANES, (h + 1) * LANES)
        q = dq_ref[:, sl]
        m_sc[...] = jnp.full(m_sc.shape, NEG, F32)
        l_sc[...] = jnp.zeros(l_sc.shape, F32)
        acc_sc[...] = jnp.zeros(acc_sc.shape, F32)

        def attn_body(c, carry):
            off = pl.multiple_of(c * ck, ck)
            s = _nt_dot(q, dk_ref[pl.ds(off, ck), sl]) + bias_sc[c]
            _softmax_step(s, dv_ref[pl.ds(off, ck), sl], m_sc, l_sc, acc_sc)
            return carry

        lax.fori_loop(0, nch, attn_body, 0)
        o_ref[:, sl] = (acc_sc[...] / l_sc[...]).astype(o_ref.dtype)


def _sparse_attention(dq, dk, dv, iq, ik, iw):
    B, S, W = dq.shape
    tq, ck = 128, 512
    n_sel = min(TOPK_MAX, S // 4)
    assert S & (S - 1) == 0 and S % ck == 0
    blk_q = lambda w: pl.BlockSpec((None, tq, w), lambda b, i: (b, i, 0))
    blk_k = lambda w: pl.BlockSpec((None, S, w), lambda b, i: (b, 0, 0), pipeline_mode=pl.Buffered(1))
    return pl.pallas_call(
        functools.partial(_dsa_kernel, tq=tq, ck=ck, n_sel=n_sel, seq_len=S),
        out_shape=jax.ShapeDtypeStruct((B, S, W), BF16),
        grid=(B, S // tq),
        in_specs=[blk_q(W), blk_k(W), blk_k(W), blk_q(iq.shape[-1]), blk_k(LANES), blk_q(LANES)],
        out_specs=blk_q(W),
        scratch_shapes=[
            pltpu.VMEM((S // ck, tq, ck), jnp.int32),
            pltpu.VMEM((S // ck, tq, ck), F32),
            pltpu.VMEM((IDX_HEADS, tq, LANES), F32),
            pltpu.VMEM((IDX_HEADS, tq, LANES), BF16),
            pltpu.VMEM((tq, 1), jnp.int32), pltpu.VMEM((tq, 1), jnp.int32),
            pltpu.VMEM((tq, 1), F32), pltpu.VMEM((tq, 1), F32), pltpu.VMEM((tq, LANES), F32)],
        compiler_params=_cparams(("parallel", "arbitrary")),
        name="sparse_attention",
    )(dq, dk, dv, iq, ik, iw)


def _merge_kernel(x_ref, oa_ref, ob_ref, woa_ref, wob_ref, wg_ref, wout_ref, g_ref, b_ref, y_ref, *, alpha):
    x = x_ref[...]
    xb = x.astype(BF16)
    d = x.shape[-1]
    ya = jnp.dot(oa_ref[...], woa_ref[...], preferred_element_type=F32)
    yb = jnp.dot(ob_ref[...], wob_ref[...], preferred_element_type=F32)
    ga = jax.nn.sigmoid(jnp.dot(xb, wg_ref[:, :d], preferred_element_type=F32))
    gb = jax.nn.sigmoid(jnp.dot(xb, wg_ref[:, d:], preferred_element_type=F32))
    merged = ga * ya + gb * yb
    h = jnp.dot(merged.astype(BF16), wout_ref[...], preferred_element_type=F32)
    y_ref[...] = _layer_norm(alpha * x + h, g_ref[...], b_ref[...])


def _pad_head_rows(w, nh, d):
    n = w.shape[1]
    return jnp.pad(w.reshape(nh, d, n), ((0, 0), (0, LANES - d), (0, 0))).reshape(nh * LANES, n)


def _merge(x2, oa, ob, woa, wob, wg, wout, g, b, alpha):
    T, D = x2.shape
    tm = 256
    row = lambda w: pl.BlockSpec((tm, w), lambda i: (i, 0))
    return pl.pallas_call(
        functools.partial(_merge_kernel, alpha=alpha),
        out_shape=jax.ShapeDtypeStruct((T, D), F32),
        grid=(T // tm,),
        in_specs=[row(D), row(oa.shape[1]), row(ob.shape[1]),
                  _const_spec(woa.shape), _const_spec(wob.shape), _const_spec(wg.shape), _const_spec(wout.shape),
                  _const_spec((1, D)), _const_spec((1, D))],
        out_specs=row(D),
        compiler_params=_cparams(("parallel",)),
        name="merge_outproj_ln",
    )(x2, oa, ob, woa, wob, wg, wout, g.reshape(1, D), b.reshape(1, D))


def _ffn_kernel(x_ref, w1_ref, w3_ref, w2_ref, g_ref, b_ref, y_ref, xb_sc, acc_sc, *, alpha):
    f = pl.program_id(1)

    @pl.when(f == 0)
    def _():
        xb_sc[...] = x_ref[...].astype(BF16)
        acc_sc[...] = jnp.zeros(acc_sc.shape, F32)

    xb = xb_sc[...]
    a = jax.nn.silu(jnp.dot(xb, w1_ref[...], preferred_element_type=F32))
    a = a * jnp.dot(xb, w3_ref[...], preferred_element_type=F32)
    acc_sc[...] += jnp.dot(a.astype(BF16), w2_ref[...], preferred_element_type=F32)

    @pl.when(f == pl.num_programs(1) - 1)
    def _():
        y_ref[...] = _layer_norm(alpha * x_ref[...] + acc_sc[...], g_ref[...], b_ref[...])


def _dense_ffn(x2, w1, w3, w2, g, b, alpha):
    T, D = x2.shape
    FF = w1.shape[1]
    tm, tf = 1024, 256
    assert FF % tf == 0
    return pl.pallas_call(
        functools.partial(_ffn_kernel, alpha=alpha),
        out_shape=jax.ShapeDtypeStruct((T, D), F32),
        grid=(T // tm, FF // tf),
        in_specs=[pl.BlockSpec((tm, D), lambda i, f: (i, 0)),
                  pl.BlockSpec((D, tf), lambda i, f: (0, f)),
                  pl.BlockSpec((D, tf), lambda i, f: (0, f)),
                  pl.BlockSpec((tf, D), lambda i, f: (f, 0)),
                  _const_spec((1, D)), _const_spec((1, D))],
        out_specs=pl.BlockSpec((tm, D), lambda i, f: (i, 0)),
        scratch_shapes=[pltpu.VMEM((tm, D), BF16), pltpu.VMEM((tm, D), F32)],
        compiler_params=_cparams(("parallel", "arbitrary")),
        name="dense_ffn_ln",
    )(x2, w1.astype(BF16), w3.astype(BF16), w2.astype(BF16), g.reshape(1, D), b.reshape(1, D))


def _router_kernel(x_ref, wr_ref, comb_ref):
    logits = jnp.dot(x_ref[...], wr_ref[...], preferred_element_type=F32, precision=lax.Precision.HIGHEST)
    lane = lax.broadcasted_iota(jnp.int32, logits.shape, 1).astype(F32)
    logits = jnp.where(lane < N_EXPERTS, logits, -jnp.inf)
    v1 = jnp.max(logits, axis=1, keepdims=True)
    i1 = jnp.min(jnp.where(logits == v1, lane, float(LANES)), axis=1, keepdims=True)
    rest = jnp.where(lane == i1, -jnp.inf, logits)
    v2 = jnp.max(rest, axis=1, keepdims=True)
    i2 = jnp.min(jnp.where(rest == v2, lane, float(LANES)), axis=1, keepdims=True)
    e2 = jnp.exp(v2 - v1)
    p1 = 1.0 / (1.0 + e2)
    comb_ref[...] = jnp.where(lane == i1, p1, 0.0) + jnp.where(lane == i2, e2 * p1, 0.0)


def _router(x2, w_router):
    T, D = x2.shape
    tm = 512
    wr = jnp.pad(w_router, ((0, 0), (0, LANES - N_EXPERTS)))
    return pl.pallas_call(
        _router_kernel,
        out_shape=jax.ShapeDtypeStruct((T, LANES), F32),
        grid=(T // tm,),
        in_specs=[pl.BlockSpec((tm, D), lambda i: (i, 0)), _const_spec(wr.shape)],
        out_specs=pl.BlockSpec((tm, LANES), lambda i: (i, 0)),
        compiler_params=_cparams(("parallel",)),
        name="router",
    )(x2, wr)


def _moe_kernel(x_ref, comb_ref, w1_ref, w3_ref, w2_ref, g_ref, b_ref, y_ref, xb_sc, acc_sc, *, alpha):
    e, f = pl.program_id(1), pl.program_id(2)

    @pl.when(jnp.logical_and(e == 0, f == 0))
    def _():
        xb_sc[...] = x_ref[...].astype(BF16)
        acc_sc[...] = jnp.zeros(acc_sc.shape, F32)

    lane = lax.broadcasted_iota(jnp.int32, comb_ref.shape, 1)
    c = jnp.sum(jnp.where(lane == e, comb_ref[...], 0.0), axis=1, keepdims=True)
    xb = xb_sc[...]
    a = jax.nn.silu(jnp.dot(xb, w1_ref[...], preferred_element_type=F32))
    a = a * jnp.dot(xb, w3_ref[...], preferred_element_type=F32) * c
    acc_sc[...] += jnp.dot(a.astype(BF16), w2_ref[...], preferred_element_type=F32)

    @pl.when(jnp.logical_and(e == pl.num_programs(1) - 1, f == pl.num_programs(2) - 1))
    def _():
        y_ref[...] = _layer_norm(alpha * x_ref[...] + acc_sc[...], g_ref[...], b_ref[...])


def _moe_ffn(x2, w_router, w1, w3, w2, g, b, alpha):
    T, D = x2.shape
    E, _, FF = w1.shape
    tm, tf = 1024, 512
    assert FF % tf == 0
    comb = _router(x2, w_router)
    return pl.pallas_call(
        functools.partial(_moe_kernel, alpha=alpha),
        out_shape=jax.ShapeDtypeStruct((T, D), F32),
        grid=(T // tm, E, FF // tf),
        in_specs=[pl.BlockSpec((tm, D), lambda i, e, f: (i, 0)),
                  pl.BlockSpec((tm, LANES), lambda i, e, f: (i, 0)),
                  pl.BlockSpec((None, D, tf), lambda i, e, f: (e, 0, f)),
                  pl.BlockSpec((None, D, tf), lambda i, e, f: (e, 0, f)),
                  pl.BlockSpec((None, tf, D), lambda i, e, f: (e, f, 0)),
                  _const_spec((1, D)), _const_spec((1, D))],
        out_specs=pl.BlockSpec((tm, D), lambda i, e, f: (i, 0)),
        scratch_shapes=[pltpu.VMEM((tm, D), BF16), pltpu.VMEM((tm, D), F32)],
        compiler_params=_cparams(("parallel", "arbitrary", "arbitrary")),
        name="moe_ffn_ln",
    )(x2, comb, w1.astype(BF16), w3.astype(BF16), w2.astype(BF16), g.reshape(1, D), b.reshape(1, D))


def kernel(x, positions, w_in, mla_q_norm, w_uq, mla_kv_norm, w_ukv, w_o_mla, w_o_dsa, w_out,
           ln1_g, ln1_b, ln2_g, ln2_b, dense_w1, dense_w3, dense_w2,
           moe_router, moe_w1, moe_w3, moe_w2):
    B, S, D = x.shape
    depth = w_in.shape[0]
    alpha = (2 * depth) ** 0.25
    T = B * S
    cos, sin = _rope_tables(positions)
    x2 = x.reshape(T, D)
    for l in range(depth):
        w_in_p, w_uq_p, w_ukv_p, w_gates = _prep_in_weights(w_in[l], w_uq[l], w_ukv[l])
        qm, km, vm, dq, dk, dv, iq, ik, iw = _projections(
            x2, cos, sin, w_in_p, w_uq_p, w_ukv_p, mla_q_norm[l], mla_kv_norm[l])
        b3 = lambda a: a.reshape(B, S, a.shape[-1])
        o_a = _mla_attention(b3(qm), b3(km), b3(vm)).reshape(T, -1)
        o_b = _sparse_attention(b3(dq), b3(dk), b3(dv), b3(iq), b3(ik), b3(iw)).reshape(T, -1)
        x2 = _merge(x2, o_a, o_b,
                    _pad_head_rows(w_o_mla[l], MLA_HEADS, MLA_V).astype(BF16),
                    _pad_head_rows(w_o_dsa[l], DSA_HEADS, DSA_HEAD_DIM).astype(BF16),
                    w_gates, w_out[l].astype(BF16), ln1_g[l], ln1_b[l], alpha)
        if l % 2 == 0:
            x2 = _dense_ffn(x2, dense_w1[l // 2], dense_w3[l // 2], dense_w2[l // 2], ln2_g[l], ln2_b[l], alpha)
        else:
            x2 = _moe_ffn(x2, moe_router[l // 2], moe_w1[l // 2], moe_w3[l // 2], moe_w2[l // 2],
                          ln2_g[l], ln2_b[l], alpha)
    return x2.reshape(B, S, D)
```

```python
import functools

import numpy as np
import jax
import jax.numpy as jnp
from jax import lax
from jax.experimental import pallas as pl
from jax.experimental.pallas import tpu as pltpu

F32 = jnp.float32
BF16 = jnp.bfloat16
LANES = 128
VMEM_LIMIT = 56 * 1024 * 1024

MLA_HEADS = 8
MLA_NOPE = 64
MLA_ROPE = 32
MLA_V = 64
Q_LORA = 384
KV_LORA = 256
DSA_HEADS = 8
DSA_HEAD_DIM = 64
IDX_HEADS = 8
IDX_DIM = 32
TOPK_MAX = 256
N_EXPERTS = 8
ROPE_THETA = 10000.0
NORM_EPS = 1e-5
NEG = -1e30
INT_MIN = np.int32(-2 ** 31)
ONE_LANE = 64
assert ONE_LANE >= MLA_V and ONE_LANE >= DSA_HEAD_DIM

C_CQ = 0
C_CKV = C_CQ + Q_LORA
C_KR = C_CKV + KV_LORA
C_DQ = C_KR + LANES
C_DK = C_DQ + DSA_HEADS * LANES
C_DV = C_DK + DSA_HEADS * LANES
C_IQ = C_DV + DSA_HEADS * LANES
C_IK = C_IQ + IDX_HEADS * IDX_DIM
C_IW = C_IK + LANES
C_END = C_IW + LANES


def _cparams(sem):
    return pltpu.CompilerParams(dimension_semantics=sem, vmem_limit_bytes=VMEM_LIMIT)


def _const_spec(shape):
    nd = len(shape)
    return pl.BlockSpec(shape, lambda *_: (0,) * nd)


def _nt_dot(a, b):
    return lax.dot_general(a, b, (((1,), (1,)), ((), ())), preferred_element_type=F32)


def _layer_norm(z, g, b):
    mu = jnp.mean(z, axis=-1, keepdims=True)
    zc = z - mu
    var = jnp.mean(zc * zc, axis=-1, keepdims=True)
    return zc * lax.rsqrt(var + NORM_EPS) * g + b


def _rope_tables_kernel(pos_ref, inv_ref, sgn_ref, cos_ref, sin_ref):
    pos = pos_ref[...].astype(F32)
    for p in range(2):
        ang = pos * inv_ref[p:p + 1, :]
        cos_ref[p] = jnp.cos(ang)
        sin_ref[p] = jnp.sin(ang) * sgn_ref[p:p + 1, :]


def _rope_tables(positions):
    T = positions.size
    tm = 1024
    inv32 = jnp.power(ROPE_THETA, -jnp.arange(0, IDX_DIM, 2, dtype=F32) / IDX_DIM)
    inv64 = jnp.power(ROPE_THETA, -jnp.arange(0, DSA_HEAD_DIM, 2, dtype=F32) / DSA_HEAD_DIM)
    inv = jnp.stack([jnp.tile(inv32, LANES // 16),
                     jnp.concatenate([inv64, inv64, jnp.zeros((LANES - 64,), F32)])])
    lane = np.arange(LANES)
    sgn = jnp.asarray(np.stack([np.where(lane % 32 < 16, -1.0, 1.0),
                                np.where(lane < 32, -1.0, np.where(lane < 64, 1.0, 0.0))]), F32)
    out = jax.ShapeDtypeStruct((2, T, LANES), F32)
    return pl.pallas_call(
        _rope_tables_kernel,
        out_shape=(out, out),
        grid=(T // tm,),
        in_specs=[pl.BlockSpec((tm, 1), lambda i: (i, 0)), _const_spec((2, LANES)), _const_spec((2, LANES))],
        out_specs=(pl.BlockSpec((2, tm, LANES), lambda i: (0, i, 0)),) * 2,
        compiler_params=_cparams(("parallel",)),
        name="rope_tables",
    )(positions.reshape(T, 1), inv, sgn)


def _proj_kernel(x_ref, cos_ref, sin_ref, w_in_ref, w_uq_ref, w_ukv_ref, gq_ref, gkv_ref,
                 qm_ref, km_ref, vm_ref, dq_ref, dk_ref, dv_ref, iq_ref, ik_ref, iw_ref, *, mla_scale):
    xb = x_ref[...].astype(BF16)
    lane = lax.broadcasted_iota(jnp.int32, (1, LANES), 1)
    cos32, sin32 = cos_ref[0], sin_ref[0]
    cos64, sin64 = cos_ref[1], sin_ref[1]
    in_rope = (lane >> 5) == (MLA_NOPE >> 5)
    cos_m = jnp.where(in_rope, cos32, 1.0)
    sin_m = jnp.where(in_rope, sin32, 0.0)
    first32 = (lane & (IDX_DIM - 1)) < IDX_DIM // 2
    first64 = lane < DSA_HEAD_DIM // 2
    wide_lane = lax.broadcasted_iota(jnp.int32, (1, MLA_HEADS * LANES), 1)
    one_col = jnp.where((wide_lane & (LANES - 1)) == ONE_LANE, 1.0, 0.0)

    def proj(lo, hi):
        return jnp.dot(xb, w_in_ref[:, lo:hi], preferred_element_type=F32)

    def rope(xs, cos, sin, half, first):
        rot = jnp.where(first, pltpu.roll(xs, LANES - half, 1), pltpu.roll(xs, half, 1))
        return xs * cos + rot * sin

    def rms(c, g):
        ms = jnp.mean(c * c, axis=-1, keepdims=True)
        return c * lax.rsqrt(ms + NORM_EPS) * g

    q = jnp.dot(rms(proj(C_CQ, C_CKV), gq_ref[...]).astype(BF16), w_uq_ref[...], preferred_element_type=F32)
    kv = jnp.dot(rms(proj(C_CKV, C_KR), gkv_ref[...]).astype(BF16), w_ukv_ref[...], preferred_element_type=F32)
    kr = rope(proj(C_KR, C_DQ), cos_m, sin_m, MLA_ROPE // 2, first32)
    for h in range(MLA_HEADS):
        sl = slice(h * LANES, (h + 1) * LANES)
        qm_ref[:, sl] = (rope(q[:, sl], cos_m, sin_m, MLA_ROPE // 2, first32) * mla_scale).astype(BF16)
        km_ref[:, sl] = (kv[:, sl] + kr).astype(BF16)
    vm_ref[...] = (kv[:, MLA_HEADS * LANES:] + one_col).astype(BF16)

    dq = proj(C_DQ, C_DK)
    dk = proj(C_DK, C_DV)
    for h in range(DSA_HEADS):
        sl = slice(h * LANES, (h + 1) * LANES)
        dq_ref[:, sl] = rope(dq[:, sl], cos64, sin64, DSA_HEAD_DIM // 2, first64).astype(BF16)
        dk_ref[:, sl] = rope(dk[:, sl], cos64, sin64, DSA_HEAD_DIM // 2, first64).astype(BF16)
    dv_ref[...] = (proj(C_DV, C_IQ) + one_col).astype(BF16)

    iq = proj(C_IQ, C_IK)
    for g in range(IDX_HEADS * IDX_DIM // LANES):
        sl = slice(g * LANES, (g + 1) * LANES)
        iq_ref[:, sl] = rope(iq[:, sl], cos32, sin32, IDX_DIM // 2, first32).astype(BF16)
    ik_ref[...] = rope(proj(C_IK, C_IW), cos32, sin32, IDX_DIM // 2, first32).astype(BF16)
    iw_ref[...] = proj(C_IW, C_END)


def _pad_heads(w, nh, d):
    k = w.shape[0]
    return jnp.pad(w.reshape(k, nh, d), ((0, 0), (0, 0), (0, LANES - d))).reshape(k, nh * LANES)


def _prep_in_weights(w_in, w_uq, w_ukv):
    d = w_in.shape[0]
    o = np.cumsum([0, Q_LORA, KV_LORA, MLA_ROPE, 512, 512, 512, IDX_HEADS * IDX_DIM, IDX_DIM, IDX_HEADS])
    c_q, c_kv, k_r, dq, dk, dv, iq, ik, iw = [w_in[:, o[i]:o[i + 1]] for i in range(9)]
    gates = w_in[:, o[9]:]
    kr_tile = jnp.pad(k_r, ((0, 0), (MLA_NOPE, LANES - MLA_NOPE - MLA_ROPE)))
    dsa_scale = DSA_HEAD_DIM ** -0.5
    idx_scale = (IDX_HEADS * IDX_DIM) ** -0.5
    w_in_p = jnp.concatenate([
        c_q, c_kv, kr_tile,
        _pad_heads(dq * dsa_scale, DSA_HEADS, DSA_HEAD_DIM),
        _pad_heads(dk, DSA_HEADS, DSA_HEAD_DIM),
        _pad_heads(dv, DSA_HEADS, DSA_HEAD_DIM),
        iq, jnp.tile(ik, (1, LANES // IDX_DIM)),
        jnp.pad(iw * idx_scale, ((0, 0), (0, LANES - IDX_HEADS))),
    ], axis=1).astype(BF16)
    assert w_in_p.shape == (d, C_END)
    w_uq_p = _pad_heads(w_uq, MLA_HEADS, MLA_NOPE + MLA_ROPE).astype(BF16)
    ukv = w_ukv.reshape(KV_LORA, MLA_HEADS, MLA_NOPE + MLA_V)
    w_ukv_p = jnp.concatenate([
        _pad_heads(ukv[:, :, :MLA_NOPE].reshape(KV_LORA, -1), MLA_HEADS, MLA_NOPE),
        _pad_heads(ukv[:, :, MLA_NOPE:].reshape(KV_LORA, -1), MLA_HEADS, MLA_V),
    ], axis=1).astype(BF16)
    return w_in_p, w_uq_p, w_ukv_p, gates.astype(BF16)


def _projections(x2, cos, sin, w_in_p, w_uq_p, w_ukv_p, g_q, g_kv):
    T, D = x2.shape
    tm = 256
    wide = MLA_HEADS * LANES
    row = lambda w: pl.BlockSpec((tm, w), lambda i: (i, 0))
    shapes = [(wide, BF16)] * 6 + [(IDX_HEADS * IDX_DIM, BF16), (LANES, BF16), (LANES, F32)]
    return pl.pallas_call(
        functools.partial(_proj_kernel, mla_scale=(MLA_NOPE + MLA_ROPE) ** -0.5),
        out_shape=tuple(jax.ShapeDtypeStruct((T, w), dt) for w, dt in shapes),
        grid=(T // tm,),
        in_specs=[row(D),
                  pl.BlockSpec((2, tm, LANES), lambda i: (0, i, 0)),
                  pl.BlockSpec((2, tm, LANES), lambda i: (0, i, 0)),
                  _const_spec(w_in_p.shape), _const_spec(w_uq_p.shape), _const_spec(w_ukv_p.shape),
                  _const_spec((1, Q_LORA)), _const_spec((1, KV_LORA))],
        out_specs=tuple(row(w) for w, _ in shapes),
        compiler_params=_cparams(("parallel",)),
        name="projections",
    )(x2, cos, sin, w_in_p, w_uq_p, w_ukv_p, g_q.reshape(1, -1), g_kv.reshape(1, -1))


def _lane_tile_max(mv, s):
    for t in range(s.shape[1] // LANES):
        mv = jnp.maximum(mv, s[:, t * LANES:(t + 1) * LANES])
    return mv


def _normalise(acc):
    return acc / acc[:, ONE_LANE:ONE_LANE + 1]


def _mla_attn_kernel(q_ref, k_ref, v_ref, o_ref, s_sc, mv_sc, m_sc, acc_sc, *, tq):
    qi = pl.program_id(2)
    q = q_ref[...]
    mv_sc[...] = jnp.full(mv_sc.shape, NEG, F32)

    def scores(j, masked):
        s = _nt_dot(q, k_ref[pl.ds(pl.multiple_of(j * tq, tq), tq), :])
        if masked:
            row = lax.broadcasted_iota(jnp.int32, (tq, tq), 0)
            col = lax.broadcasted_iota(jnp.int32, (tq, tq), 1)
            s = jnp.where(col <= row, s, NEG)
        s_sc[j] = s
        mv_sc[...] = _lane_tile_max(mv_sc[...], s)

    def score_body(j, c):
        scores(j, False)
        return c

    lax.fori_loop(0, qi, score_body, 0)
    scores(qi, True)
    m_sc[...] = jnp.max(mv_sc[...], axis=1, keepdims=True)
    acc_sc[...] = jnp.zeros(acc_sc.shape, F32)

    def pv_body(j, c):
        p = jnp.exp(s_sc[j] - m_sc[...]).astype(BF16)
        acc_sc[...] += jnp.dot(p, v_ref[pl.ds(pl.multiple_of(j * tq, tq), tq), :], preferred_element_type=F32)
        return c

    lax.fori_loop(0, qi + 1, pv_body, 0)
    o_ref[...] = _normalise(acc_sc[...]).astype(o_ref.dtype)


def _mla_attention(q, k, v):
    B, S, W = q.shape
    tq = 512
    blk_q = pl.BlockSpec((None, tq, LANES), lambda b, h, i: (b, i, h))
    blk_kv = pl.BlockSpec((None, S, LANES), lambda b, h, i: (b, 0, h))
    return pl.pallas_call(
        functools.partial(_mla_attn_kernel, tq=tq),
        out_shape=jax.ShapeDtypeStruct((B, S, W), BF16),
        grid=(B, W // LANES, S // tq),
        in_specs=[blk_q, blk_kv, blk_kv],
        out_specs=blk_q,
        scratch_shapes=[pltpu.VMEM((S // tq, tq, tq), F32), pltpu.VMEM((tq, LANES), F32),
                        pltpu.VMEM((tq, 1), F32), pltpu.VMEM((tq, LANES), F32)],
        compiler_params=_cparams(("parallel", "parallel", "arbitrary")),
        name="mla_attention",
    )(q, k, v)


def _dsa_kernel(dq_ref, dk_ref, dv_ref, iq_ref, ik_ref, iw_ref, o_ref,
                key_sc, bias_sc, s_sc, wb_sc, qm_sc, tau_sc, msel_sc, mv_sc, m_sc, acc_sc,
                *, tq, ck, n_sel, seq_len):
    qi = pl.program_id(1)
    nch = (qi * tq + tq + ck - 1) // ck
    nsl = ck // LANES
    rb_rows = LANES
    lane = lax.broadcasted_iota(jnp.int32, (1, LANES), 1)
    qpos_b = qi * tq + lax.broadcasted_iota(jnp.int32, (tq, LANES), 0)
    k_sel = float(n_sel)

    for h in range(IDX_HEADS):
        wb_sc[h] = jnp.broadcast_to(iw_ref[:, h:h + 1], (tq, LANES))
        g, lo = divmod(h * IDX_DIM, LANES)
        qg = iq_ref[:, g * LANES:(g + 1) * LANES].astype(F32)
        in_head = (lane >> 5) == (lo >> 5)
        qm_sc[h * tq:(h + 1) * tq, :] = jnp.where(in_head, qg, 0.0).astype(BF16)

    kw = 2 * LANES

    def score_body(c, carry):
        for part in range(ck // kw):
            ks = ik_ref[pl.ds(pl.multiple_of(c * ck + part * kw, kw), kw), :]
            d = _nt_dot(qm_sc[...], ks)
            for sub in range(kw // LANES):
                cols = slice(sub * LANES, (sub + 1) * LANES)
                score = jnp.zeros((tq, LANES), F32)
                for h in range(IDX_HEADS):
                    score = score + wb_sc[h] * jnp.maximum(d[h * tq:(h + 1) * tq, cols], 0.0)
                bits = lax.bitcast_convert_type(score, jnp.int32)
                key = jnp.where(bits >= 0, bits, bits ^ np.int32(0x7FFFFFFF))
                s_idx = part * (kw // LANES) + sub
                kpos = c * ck + s_idx * LANES + lane
                key_sc[c, :, s_idx * LANES:(s_idx + 1) * LANES] = jnp.where(kpos <= qpos_b, key, INT_MIN)
        return carry

    lax.fori_loop(0, nch, score_body, 0)

    def count(make_pred):
        outs = []
        for rb in range(tq // rb_rows):
            rows = slice(rb * rb_rows, (rb + 1) * rb_rows)
            pred = make_pred(rows)

            def body(c, acc, rows=rows, pred=pred):
                for s in range(nsl):
                    acc = acc + pred(key_sc[c, rows, s * LANES:(s + 1) * LANES], c * ck + s * LANES + lane)
                return acc

            acc = lax.fori_loop(0, nch, body, jnp.zeros((rb_rows, LANES), F32))
            outs.append(jnp.sum(acc, axis=1, keepdims=True))
        return jnp.concatenate(outs, axis=0)

    def bcast(col, rows):
        return jnp.broadcast_to(col[rows], (rb_rows, LANES))

    def count_ge(thr):
        def make(rows):
            tb = bcast(thr, rows)
            return lambda blk, kpos: jnp.where(blk >= tb, 1.0, 0.0)
        return count(make)

    def count_gt(thr):
        def make(rows):
            tb = bcast(thr, rows)
            return lambda blk, kpos: jnp.where(blk > tb, 1.0, 0.0)
        return count(make)

    tau0 = jnp.where(count_ge(jnp.zeros((tq, 1), jnp.int32)) >= k_sel, np.int32(0), INT_MIN)

    def tau_step(i, tau):
        cand = tau | (jnp.int32(1) << (30 - i))
        return jnp.where(count_ge(cand) >= k_sel, cand, tau)

    tau = lax.fori_loop(0, 31, tau_step, tau0)
    n_ge = count_ge(tau)
    tau_sc[...] = tau
    msel_sc[...] = jnp.full((tq, 1), seq_len - 1, jnp.int32)

    @pl.when(jnp.max(n_ge) > k_sel)
    def _():
        tau_v = tau_sc[...]
        need = k_sel - count_gt(tau_v)

        def m_step(i, m):
            cand = m | (jnp.int32(1) << (seq_len.bit_length() - 2 - i))

            def make(rows):
                tb, cb = bcast(tau_v, rows), bcast(cand, rows)
                return lambda blk, kpos: jnp.where(blk == tb, jnp.where(kpos < cb, 1.0, 0.0), 0.0)

            return jnp.where(count(make) < need, cand, m)

        msel_sc[...] = lax.fori_loop(0, seq_len.bit_length() - 1, m_step, jnp.zeros((tq, 1), jnp.int32))

    tau_b = jnp.broadcast_to(tau_sc[...], (tq, LANES))
    msel_b = jnp.broadcast_to(msel_sc[...], (tq, LANES))

    def bias_body(c, carry):
        for s in range(nsl):
            blk = key_sc[c, :, s * LANES:(s + 1) * LANES]
            kpos = c * ck + s * LANES + lane
            tie = jnp.where(blk == tau_b, jnp.where(kpos <= msel_b, 0.0, NEG), NEG)
            sel = jnp.where(blk > tau_b, 0.0, tie)
            bias_sc[c, :, s * LANES:(s + 1) * LANES] = jnp.where(kpos <= qpos_b, sel, NEG)
        return carry

    lax.fori_loop(0, nch, bias_body, 0)

    for h in range(DSA_HEADS):
        sl = slice(h * LANES, (h + 1) * LANES)
        q = dq_ref[:, sl]
        mv_sc[...] = jnp.full(mv_sc.shape, NEG, F32)

        def score_h(c, carry, q=q, sl=sl):
            s = _nt_dot(q, dk_ref[pl.ds(pl.multiple_of(c * ck, ck), ck), sl]) + bias_sc[c]
            s_sc[c] = s
            mv_sc[...] = _lane_tile_max(mv_sc[...], s)
            return carry

        lax.fori_loop(0, nch, score_h, 0)
        m_sc[...] = jnp.max(mv_sc[...], axis=1, keepdims=True)
        acc_sc[...] = jnp.zeros(acc_sc.shape, F32)

        def pv_h(c, carry, sl=sl):
            p = jnp.exp(s_sc[c] - m_sc[...]).astype(BF16)
            acc_sc[...] += jnp.dot(p, dv_ref[pl.ds(pl.multiple_of(c * ck, ck), ck), sl], preferred_element_type=F32)
            return carry

        lax.fori_loop(0, nch, pv_h, 0)
        o_ref[:, sl] = _normalise(acc_sc[...]).astype(o_ref.dtype)


def _sparse_attention(dq, dk, dv, iq, ik, iw):
    B, S, W = dq.shape
    tq, ck = 256, 512
    n_sel = min(TOPK_MAX, S // 4)
    assert S & (S - 1) == 0 and S % ck == 0
    blk_q = lambda w: pl.BlockSpec((None, tq, w), lambda b, i: (b, i, 0))
    blk_k = lambda w: pl.BlockSpec((None, S, w), lambda b, i: (b, 0, 0), pipeline_mode=pl.Buffered(1))
    return pl.pallas_call(
        functools.partial(_dsa_kernel, tq=tq, ck=ck, n_sel=n_sel, seq_len=S),
        out_shape=jax.ShapeDtypeStruct((B, S, W), BF16),
        grid=(B, S // tq),
        in_specs=[blk_q(W), blk_k(W), blk_k(W), blk_q(iq.shape[-1]), blk_k(LANES), blk_q(LANES)],
        out_specs=blk_q(W),
        scratch_shapes=[
            pltpu.VMEM((S // ck, tq, ck), jnp.int32),
            pltpu.VMEM((S // ck, tq, ck), F32),
            pltpu.VMEM((S // ck, tq, ck), F32),
            pltpu.VMEM((IDX_HEADS, tq, LANES), F32),
            pltpu.VMEM((IDX_HEADS * tq, LANES), BF16),
            pltpu.VMEM((tq, 1), jnp.int32), pltpu.VMEM((tq, 1), jnp.int32),
            pltpu.VMEM((tq, LANES), F32), pltpu.VMEM((tq, 1), F32), pltpu.VMEM((tq, LANES), F32)],
        compiler_params=_cparams(("parallel", "arbitrary")),
        name="sparse_attention",
    )(dq, dk, dv, iq, ik, iw)


def _merge_kernel(x_ref, oa_ref, ob_ref, woa_ref, wob_ref, wg_ref, wout_ref, g_ref, b_ref, y_ref, *, alpha):
    x = x_ref[...]
    xb = x.astype(BF16)
    d = x.shape[-1]
    ya = jnp.dot(oa_ref[...], woa_ref[...], preferred_element_type=F32)
    yb = jnp.dot(ob_ref[...], wob_ref[...], preferred_element_type=F32)
    ga = jax.nn.sigmoid(jnp.dot(xb, wg_ref[:, :d], preferred_element_type=F32))
    gb = jax.nn.sigmoid(jnp.dot(xb, wg_ref[:, d:], preferred_element_type=F32))
    merged = ga * ya + gb * yb
    h = jnp.dot(merged.astype(BF16), wout_ref[...], preferred_element_type=F32)
    y_ref[...] = _layer_norm(alpha * x + h, g_ref[...], b_ref[...])


def _pad_head_rows(w, nh, d):
    n = w.shape[1]
    return jnp.pad(w.reshape(nh, d, n), ((0, 0), (0, LANES - d), (0, 0))).reshape(nh * LANES, n)


def _merge(x2, oa, ob, woa, wob, wg, wout, g, b, alpha):
    T, D = x2.shape
    tm = 256
    row = lambda w: pl.BlockSpec((tm, w), lambda i: (i, 0))
    return pl.pallas_call(
        functools.partial(_merge_kernel, alpha=alpha),
        out_shape=jax.ShapeDtypeStruct((T, D), F32),
        grid=(T // tm,),
        in_specs=[row(D), row(oa.shape[1]), row(ob.shape[1]),
                  _const_spec(woa.shape), _const_spec(wob.shape), _const_spec(wg.shape), _const_spec(wout.shape),
                  _const_spec((1, D)), _const_spec((1, D))],
        out_specs=row(D),
        compiler_params=_cparams(("parallel",)),
        name="merge_outproj_ln",
    )(x2, oa, ob, woa, wob, wg, wout, g.reshape(1, D), b.reshape(1, D))


def _ffn_kernel(x_ref, w1_ref, w3_ref, w2_ref, g_ref, b_ref, y_ref, xb_sc, acc_sc, *, alpha):
    f = pl.program_id(1)

    @pl.when(f == 0)
    def _():
        xb_sc[...] = x_ref[...].astype(BF16)
        acc_sc[...] = jnp.zeros(acc_sc.shape, F32)

    xb = xb_sc[...]
    a = jax.nn.silu(jnp.dot(xb, w1_ref[...], preferred_element_type=F32))
    a = a * jnp.dot(xb, w3_ref[...], preferred_element_type=F32)
    acc_sc[...] += jnp.dot(a.astype(BF16), w2_ref[...], preferred_element_type=F32)

    @pl.when(f == pl.num_programs(1) - 1)
    def _():
        y_ref[...] = _layer_norm(alpha * x_ref[...] + acc_sc[...], g_ref[...], b_ref[...])


def _dense_ffn(x2, w1, w3, w2, g, b, alpha):
    T, D = x2.shape
    FF = w1.shape[1]
    tm, tf = 1024, 256
    assert FF % tf == 0
    return pl.pallas_call(
        functools.partial(_ffn_kernel, alpha=alpha),
        out_shape=jax.ShapeDtypeStruct((T, D), F32),
        grid=(T // tm, FF // tf),
        in_specs=[pl.BlockSpec((tm, D), lambda i, f: (i, 0)),
                  pl.BlockSpec((D, tf), lambda i, f: (0, f)),
                  pl.BlockSpec((D, tf), lambda i, f: (0, f)),
                  pl.BlockSpec((tf, D), lambda i, f: (f, 0)),
                  _const_spec((1, D)), _const_spec((1, D))],
        out_specs=pl.BlockSpec((tm, D), lambda i, f: (i, 0)),
        scratch_shapes=[pltpu.VMEM((tm, D), BF16), pltpu.VMEM((tm, D), F32)],
        compiler_params=_cparams(("parallel", "arbitrary")),
        name="dense_ffn_ln",
    )(x2, w1.astype(BF16), w3.astype(BF16), w2.astype(BF16), g.reshape(1, D), b.reshape(1, D))


def _router_kernel(x_ref, wr_ref, comb_ref):
    logits = jnp.dot(x_ref[...], wr_ref[...], preferred_element_type=F32, precision=lax.Precision.HIGHEST)
    lane = lax.broadcasted_iota(jnp.int32, logits.shape, 1).astype(F32)
    logits = jnp.where(lane < N_EXPERTS, logits, -jnp.inf)
    v1 = jnp.max(logits, axis=1, keepdims=True)
    i1 = jnp.min(jnp.where(logits == v1, lane, float(LANES)), axis=1, keepdims=True)
    rest = jnp.where(lane == i1, -jnp.inf, logits)
    v2 = jnp.max(rest, axis=1, keepdims=True)
    i2 = jnp.min(jnp.where(rest == v2, lane, float(LANES)), axis=1, keepdims=True)
    e2 = jnp.exp(v2 - v1)
    p1 = 1.0 / (1.0 + e2)
    comb_ref[...] = jnp.where(lane == i1, p1, 0.0) + jnp.where(lane == i2, e2 * p1, 0.0)


def _router(x2, w_router):
    T, D = x2.shape
    tm = 512
    wr = jnp.pad(w_router, ((0, 0), (0, LANES - N_EXPERTS)))
    return pl.pallas_call(
        _router_kernel,
        out_shape=jax.ShapeDtypeStruct((T, LANES), F32),
        grid=(T // tm,),
        in_specs=[pl.BlockSpec((tm, D), lambda i: (i, 0)), _const_spec(wr.shape)],
        out_specs=pl.BlockSpec((tm, LANES), lambda i: (i, 0)),
        compiler_params=_cparams(("parallel",)),
        name="router",
    )(x2, wr)


def _moe_kernel(x_ref, comb_ref, w1_ref, w3_ref, w2_ref, g_ref, b_ref, y_ref, xb_sc, acc_sc, *, alpha):
    e, f = pl.program_id(1), pl.program_id(2)

    @pl.when(jnp.logical_and(e == 0, f == 0))
    def _():
        xb_sc[...] = x_ref[...].astype(BF16)
        acc_sc[...] = jnp.zeros(acc_sc.shape, F32)

    lane = lax.broadcasted_iota(jnp.int32, comb_ref.shape, 1)
    c = jnp.sum(jnp.where(lane == e, comb_ref[...], 0.0), axis=1, keepdims=True)
    xb = xb_sc[...]
    a = jax.nn.silu(jnp.dot(xb, w1_ref[...], preferred_element_type=F32))
    a = a * jnp.dot(xb, w3_ref[...], preferred_element_type=F32) * c
    acc_sc[...] += jnp.dot(a.astype(BF16), w2_ref[...], preferred_element_type=F32)

    @pl.when(jnp.logical_and(e == pl.num_programs(1) - 1, f == pl.num_programs(2) - 1))
    def _():
        y_ref[...] = _layer_norm(alpha * x_ref[...] + acc_sc[...], g_ref[...], b_ref[...])


def _moe_ffn(x2, w_router, w1, w3, w2, g, b, alpha):
    T, D = x2.shape
    E, _, FF = w1.shape
    tm, tf = 1024, 512
    assert FF % tf == 0
    comb = _router(x2, w_router)
    return pl.pallas_call(
        functools.partial(_moe_kernel, alpha=alpha),
        out_shape=jax.ShapeDtypeStruct((T, D), F32),
        grid=(T // tm, E, FF // tf),
        in_specs=[pl.BlockSpec((tm, D), lambda i, e, f: (i, 0)),
                  pl.BlockSpec((tm, LANES), lambda i, e, f: (i, 0)),
                  pl.BlockSpec((None, D, tf), lambda i, e, f: (e, 0, f)),
                  pl.BlockSpec((None, D, tf), lambda i, e, f: (e, 0, f)),
                  pl.BlockSpec((None, tf, D), lambda i, e, f: (e, f, 0)),
                  _const_spec((1, D)), _const_spec((1, D))],
        out_specs=pl.BlockSpec((tm, D), lambda i, e, f: (i, 0)),
        scratch_shapes=[pltpu.VMEM((tm, D), BF16), pltpu.VMEM((tm, D), F32)],
        compiler_params=_cparams(("parallel", "arbitrary", "arbitrary")),
        name="moe_ffn_ln",
    )(x2, comb, w1.astype(BF16), w3.astype(BF16), w2.astype(BF16), g.reshape(1, D), b.reshape(1, D))


def kernel(x, positions, w_in, mla_q_norm, w_uq, mla_kv_norm, w_ukv, w_o_mla, w_o_dsa, w_out,
           ln1_g, ln1_b, ln2_g, ln2_b, dense_w1, dense_w3, dense_w2,
           moe_router, moe_w1, moe_w3, moe_w2):
    B, S, D = x.shape
    depth = w_in.shape[0]
    alpha = (2 * depth) ** 0.25
    T = B * S
    cos, sin = _rope_tables(positions)
    x2 = x.reshape(T, D)
    for l in range(depth):
        w_in_p, w_uq_p, w_ukv_p, w_gates = _prep_in_weights(w_in[l], w_uq[l], w_ukv[l])
        qm, km, vm, dq, dk, dv, iq, ik, iw = _projections(
            x2, cos, sin, w_in_p, w_uq_p, w_ukv_p, mla_q_norm[l], mla_kv_norm[l])
        b3 = lambda a: a.reshape(B, S, a.shape[-1])
        o_a = _mla_attention(b3(qm), b3(km), b3(vm)).reshape(T, -1)
        o_b = _sparse_attention(b3(dq), b3(dk), b3(dv), b3(iq), b3(ik), b3(iw)).reshape(T, -1)
        x2 = _merge(x2, o_a, o_b,
                    _pad_head_rows(w_o_mla[l], MLA_HEADS, MLA_V).astype(BF16),
                    _pad_head_rows(w_o_dsa[l], DSA_HEADS, DSA_HEAD_DIM).astype(BF16),
                    w_gates, w_out[l].astype(BF16), ln1_g[l], ln1_b[l], alpha)
        if l % 2 == 0:
            x2 = _dense_ffn(x2, dense_w1[l // 2], dense_w3[l // 2], dense_w2[l // 2], ln2_g[l], ln2_b[l], alpha)
        else:
            x2 = _moe_ffn(x2, moe_router[l // 2], moe_w1[l // 2], moe_w3[l // 2], moe_w2[l // 2],
                          ln2_g[l], ln2_b[l], alpha)
    return x2.reshape(B, S, D)
```

```python
import functools

import numpy as np
import jax
import jax.numpy as jnp
from jax import lax
from jax.experimental import pallas as pl
from jax.experimental.pallas import tpu as pltpu

F32 = jnp.float32
BF16 = jnp.bfloat16
LANES = 128
VMEM_LIMIT = 56 * 1024 * 1024

MLA_HEADS = 8
MLA_NOPE = 64
MLA_ROPE = 32
MLA_V = 64
Q_LORA = 384
KV_LORA = 256
DSA_HEADS = 8
DSA_HEAD_DIM = 64
IDX_HEADS = 8
IDX_DIM = 32
TOPK_MAX = 256
N_EXPERTS = 8
ROPE_THETA = 10000.0
NORM_EPS = 1e-5
NEG = -1e30
INT_MIN = np.int32(-2 ** 31)
ONE_LANE = 64
assert ONE_LANE >= MLA_V and ONE_LANE >= DSA_HEAD_DIM

C_CQ = 0
C_CKV = C_CQ + Q_LORA
C_KR = C_CKV + KV_LORA
C_DQ = C_KR + LANES
C_DK = C_DQ + DSA_HEADS * LANES
C_DV = C_DK + DSA_HEADS * LANES
C_IQ = C_DV + DSA_HEADS * LANES
C_IK = C_IQ + IDX_HEADS * IDX_DIM
C_IW = C_IK + LANES
C_END = C_IW + LANES


def _cparams(sem):
    return pltpu.CompilerParams(dimension_semantics=sem, vmem_limit_bytes=VMEM_LIMIT)


def _const_spec(shape):
    nd = len(shape)
    return pl.BlockSpec(shape, lambda *_: (0,) * nd)


def _nt_dot(a, b):
    return lax.dot_general(a, b, (((1,), (1,)), ((), ())), preferred_element_type=F32)


def _layer_norm(z, g, b):
    mu = jnp.mean(z, axis=-1, keepdims=True)
    zc = z - mu
    var = jnp.mean(zc * zc, axis=-1, keepdims=True)
    return zc * lax.rsqrt(var + NORM_EPS) * g + b


def _rope_tables_kernel(pos_ref, inv_ref, sgn_ref, cos_ref, sin_ref):
    pos = pos_ref[...].astype(F32)
    for p in range(2):
        ang = pos * inv_ref[p:p + 1, :]
        cos_ref[p] = jnp.cos(ang)
        sin_ref[p] = jnp.sin(ang) * sgn_ref[p:p + 1, :]


def _rope_tables(positions):
    T = positions.size
    tm = 1024
    inv32 = jnp.power(ROPE_THETA, -jnp.arange(0, IDX_DIM, 2, dtype=F32) / IDX_DIM)
    inv64 = jnp.power(ROPE_THETA, -jnp.arange(0, DSA_HEAD_DIM, 2, dtype=F32) / DSA_HEAD_DIM)
    inv = jnp.stack([jnp.tile(inv32, LANES // 16),
                     jnp.concatenate([inv64, inv64, jnp.zeros((LANES - 64,), F32)])])
    lane = np.arange(LANES)
    sgn = jnp.asarray(np.stack([np.where(lane % 32 < 16, -1.0, 1.0),
                                np.where(lane < 32, -1.0, np.where(lane < 64, 1.0, 0.0))]), F32)
    out = jax.ShapeDtypeStruct((2, T, LANES), F32)
    return pl.pallas_call(
        _rope_tables_kernel,
        out_shape=(out, out),
        grid=(T // tm,),
        in_specs=[pl.BlockSpec((tm, 1), lambda i: (i, 0)), _const_spec((2, LANES)), _const_spec((2, LANES))],
        out_specs=(pl.BlockSpec((2, tm, LANES), lambda i: (0, i, 0)),) * 2,
        compiler_params=_cparams(("parallel",)),
        name="rope_tables",
    )(positions.reshape(T, 1), inv, sgn)


def _proj_kernel(x_ref, cos_ref, sin_ref, w_in_ref, w_uq_ref, w_ukv_ref, gq_ref, gkv_ref,
                 qm_ref, km_ref, vm_ref, dq_ref, dk_ref, dv_ref, iq_ref, ik_ref, iw_ref, *, mla_scale):
    xb = x_ref[...].astype(BF16)
    lane = lax.broadcasted_iota(jnp.int32, (1, LANES), 1)
    cos32, sin32 = cos_ref[0], sin_ref[0]
    cos64, sin64 = cos_ref[1], sin_ref[1]
    in_rope = (lane >> 5) == (MLA_NOPE >> 5)
    cos_m = jnp.where(in_rope, cos32, 1.0)
    sin_m = jnp.where(in_rope, sin32, 0.0)
    first32 = (lane & (IDX_DIM - 1)) < IDX_DIM // 2
    first64 = lane < DSA_HEAD_DIM // 2
    wide_lane = lax.broadcasted_iota(jnp.int32, (1, MLA_HEADS * LANES), 1)
    one_col = jnp.where((wide_lane & (LANES - 1)) == ONE_LANE, 1.0, 0.0)

    def proj(lo, hi):
        return jnp.dot(xb, w_in_ref[:, lo:hi], preferred_element_type=F32)

    def rope(xs, cos, sin, half, first):
        rot = jnp.where(first, pltpu.roll(xs, LANES - half, 1), pltpu.roll(xs, half, 1))
        return xs * cos + rot * sin

    def rms(c, g):
        ms = jnp.mean(c * c, axis=-1, keepdims=True)
        return c * lax.rsqrt(ms + NORM_EPS) * g

    q = jnp.dot(rms(proj(C_CQ, C_CKV), gq_ref[...]).astype(BF16), w_uq_ref[...], preferred_element_type=F32)
    kv = jnp.dot(rms(proj(C_CKV, C_KR), gkv_ref[...]).astype(BF16), w_ukv_ref[...], preferred_element_type=F32)
    kr = rope(proj(C_KR, C_DQ), cos_m, sin_m, MLA_ROPE // 2, first32)
    for h in range(MLA_HEADS):
        sl = slice(h * LANES, (h + 1) * LANES)
        qm_ref[:, sl] = (rope(q[:, sl], cos_m, sin_m, MLA_ROPE // 2, first32) * mla_scale).astype(BF16)
        km_ref[:, sl] = (kv[:, sl] + kr).astype(BF16)
    vm_ref[...] = (kv[:, MLA_HEADS * LANES:] + one_col).astype(BF16)

    dq = proj(C_DQ, C_DK)
    dk = proj(C_DK, C_DV)
    for h in range(DSA_HEADS):
        sl = slice(h * LANES, (h + 1) * LANES)
        dq_ref[:, sl] = rope(dq[:, sl], cos64, sin64, DSA_HEAD_DIM // 2, first64).astype(BF16)
        dk_ref[:, sl] = rope(dk[:, sl], cos64, sin64, DSA_HEAD_DIM // 2, first64).astype(BF16)
    dv_ref[...] = (proj(C_DV, C_IQ) + one_col).astype(BF16)

    iq = proj(C_IQ, C_IK)
    for g in range(IDX_HEADS * IDX_DIM // LANES):
        sl = slice(g * LANES, (g + 1) * LANES)
        iq_ref[:, sl] = rope(iq[:, sl], cos32, sin32, IDX_DIM // 2, first32).astype(BF16)
    ik_ref[...] = rope(proj(C_IK, C_IW), cos32, sin32, IDX_DIM // 2, first32).astype(BF16)
    iw_ref[...] = proj(C_IW, C_END)


def _pad_heads(w, nh, d):
    k = w.shape[0]
    return jnp.pad(w.reshape(k, nh, d), ((0, 0), (0, 0), (0, LANES - d))).reshape(k, nh * LANES)


def _prep_in_weights(w_in, w_uq, w_ukv):
    d = w_in.shape[0]
    o = np.cumsum([0, Q_LORA, KV_LORA, MLA_ROPE, 512, 512, 512, IDX_HEADS * IDX_DIM, IDX_DIM, IDX_HEADS])
    c_q, c_kv, k_r, dq, dk, dv, iq, ik, iw = [w_in[:, o[i]:o[i + 1]] for i in range(9)]
    gates = w_in[:, o[9]:]
    kr_tile = jnp.pad(k_r, ((0, 0), (MLA_NOPE, LANES - MLA_NOPE - MLA_ROPE)))
    dsa_scale = DSA_HEAD_DIM ** -0.5
    idx_scale = (IDX_HEADS * IDX_DIM) ** -0.5
    w_in_p = jnp.concatenate([
        c_q, c_kv, kr_tile,
        _pad_heads(dq * dsa_scale, DSA_HEADS, DSA_HEAD_DIM),
        _pad_heads(dk, DSA_HEADS, DSA_HEAD_DIM),
        _pad_heads(dv, DSA_HEADS, DSA_HEAD_DIM),
        iq, jnp.tile(ik, (1, LANES // IDX_DIM)),
        jnp.pad(iw * idx_scale, ((0, 0), (0, LANES - IDX_HEADS))),
    ], axis=1).astype(BF16)
    assert w_in_p.shape == (d, C_END)
    w_uq_p = _pad_heads(w_uq, MLA_HEADS, MLA_NOPE + MLA_ROPE).astype(BF16)
    ukv = w_ukv.reshape(KV_LORA, MLA_HEADS, MLA_NOPE + MLA_V)
    w_ukv_p = jnp.concatenate([
        _pad_heads(ukv[:, :, :MLA_NOPE].reshape(KV_LORA, -1), MLA_HEADS, MLA_NOPE),
        _pad_heads(ukv[:, :, MLA_NOPE:].reshape(KV_LORA, -1), MLA_HEADS, MLA_V),
    ], axis=1).astype(BF16)
    return w_in_p, w_uq_p, w_ukv_p, gates.astype(BF16)


def _projections(x2, cos, sin, w_in_p, w_uq_p, w_ukv_p, g_q, g_kv):
    T, D = x2.shape
    tm = 256
    wide = MLA_HEADS * LANES
    row = lambda w: pl.BlockSpec((tm, w), lambda i: (i, 0))
    shapes = [(wide, BF16)] * 6 + [(IDX_HEADS * IDX_DIM, BF16), (LANES, BF16), (LANES, F32)]
    return pl.pallas_call(
        functools.partial(_proj_kernel, mla_scale=(MLA_NOPE + MLA_ROPE) ** -0.5),
        out_shape=tuple(jax.ShapeDtypeStruct((T, w), dt) for w, dt in shapes),
        grid=(T // tm,),
        in_specs=[row(D),
                  pl.BlockSpec((2, tm, LANES), lambda i: (0, i, 0)),
                  pl.BlockSpec((2, tm, LANES), lambda i: (0, i, 0)),
                  _const_spec(w_in_p.shape), _const_spec(w_uq_p.shape), _const_spec(w_ukv_p.shape),
                  _const_spec((1, Q_LORA)), _const_spec((1, KV_LORA))],
        out_specs=tuple(row(w) for w, _ in shapes),
        compiler_params=_cparams(("parallel",)),
        name="projections",
    )(x2, cos, sin, w_in_p, w_uq_p, w_ukv_p, g_q.reshape(1, -1), g_kv.reshape(1, -1))


def _lane_tile_max(mv, s):
    for t in range(s.shape[1] // LANES):
        mv = jnp.maximum(mv, s[:, t * LANES:(t + 1) * LANES])
    return mv


def _normalise(acc):
    return acc / acc[:, ONE_LANE:ONE_LANE + 1]


def _mla_attn_kernel(q_ref, k_ref, v_ref, o_ref, s_sc, mv_sc, m_sc, acc_sc, *, tq):
    qi = pl.program_id(2)
    hg = s_sc.shape[0]
    slabs = [slice(j * LANES, (j + 1) * LANES) for j in range(hg)]
    mv_sc[...] = jnp.full(mv_sc.shape, NEG, F32)

    def scores(c, masked):
        off = pl.multiple_of(c * tq, tq)
        for j, sl in enumerate(slabs):
            s = _nt_dot(q_ref[:, sl], k_ref[pl.ds(off, tq), sl])
            if masked:
                row = lax.broadcasted_iota(jnp.int32, (tq, tq), 0)
                col = lax.broadcasted_iota(jnp.int32, (tq, tq), 1)
                s = jnp.where(col <= row, s, NEG)
            s_sc[j, c] = s
            mv_sc[j] = _lane_tile_max(mv_sc[j], s)

    def score_body(c, carry):
        scores(c, False)
        return carry

    lax.fori_loop(0, qi, score_body, 0)
    scores(qi, True)
    for j in range(hg):
        m_sc[j] = jnp.max(mv_sc[j], axis=1, keepdims=True)
    acc_sc[...] = jnp.zeros(acc_sc.shape, F32)

    def pv_body(c, carry):
        off = pl.multiple_of(c * tq, tq)
        for j, sl in enumerate(slabs):
            p = jnp.exp(s_sc[j, c] - m_sc[j]).astype(BF16)
            acc_sc[j] += jnp.dot(p, v_ref[pl.ds(off, tq), sl], preferred_element_type=F32)
        return carry

    lax.fori_loop(0, qi + 1, pv_body, 0)
    for j, sl in enumerate(slabs):
        o_ref[:, sl] = _normalise(acc_sc[j]).astype(o_ref.dtype)


def _mla_attention(q, k, v):
    B, S, W = q.shape
    tq, hg = 512, 2
    blk_q = pl.BlockSpec((None, tq, hg * LANES), lambda b, h, i: (b, i, h))
    blk_kv = pl.BlockSpec((None, S, hg * LANES), lambda b, h, i: (b, 0, h))
    return pl.pallas_call(
        functools.partial(_mla_attn_kernel, tq=tq),
        out_shape=jax.ShapeDtypeStruct((B, S, W), BF16),
        grid=(B, W // (hg * LANES), S // tq),
        in_specs=[blk_q, blk_kv, blk_kv],
        out_specs=blk_q,
        scratch_shapes=[pltpu.VMEM((hg, S // tq, tq, tq), F32), pltpu.VMEM((hg, tq, LANES), F32),
                        pltpu.VMEM((hg, tq, 1), F32), pltpu.VMEM((hg, tq, LANES), F32)],
        compiler_params=_cparams(("parallel", "parallel", "arbitrary")),
        name="mla_attention",
    )(q, k, v)


def _dsa_kernel(dq_ref, dk_ref, dv_ref, iq_ref, ik_ref, iw_ref, o_ref,
                key_sc, bias_sc, s_sc, wb_sc, qm_sc, tau_sc, msel_sc, mv_sc, m_sc, acc_sc,
                *, tq, ck, n_sel, seq_len):
    qi = pl.program_id(1)
    nch = (qi * tq + tq + ck - 1) // ck
    nsl = ck // LANES
    rb_rows = LANES
    lane = lax.broadcasted_iota(jnp.int32, (1, LANES), 1)
    qpos_b = qi * tq + lax.broadcasted_iota(jnp.int32, (tq, LANES), 0)
    k_sel = float(n_sel)

    for h in range(IDX_HEADS):
        wb_sc[h] = jnp.broadcast_to(iw_ref[:, h:h + 1], (tq, LANES))
        g, lo = divmod(h * IDX_DIM, LANES)
        qg = iq_ref[:, g * LANES:(g + 1) * LANES].astype(F32)
        in_head = (lane >> 5) == (lo >> 5)
        qm_sc[h * tq:(h + 1) * tq, :] = jnp.where(in_head, qg, 0.0).astype(BF16)

    kw = 2 * LANES

    def score_body(c, carry):
        for part in range(ck // kw):
            ks = ik_ref[pl.ds(pl.multiple_of(c * ck + part * kw, kw), kw), :]
            d = _nt_dot(qm_sc[...], ks)
            for sub in range(kw // LANES):
                cols = slice(sub * LANES, (sub + 1) * LANES)
                score = jnp.zeros((tq, LANES), F32)
                for h in range(IDX_HEADS):
                    score = score + wb_sc[h] * jnp.maximum(d[h * tq:(h + 1) * tq, cols], 0.0)
                bits = lax.bitcast_convert_type(score, jnp.int32)
                key = jnp.where(bits >= 0, bits, bits ^ np.int32(0x7FFFFFFF))
                s_idx = part * (kw // LANES) + sub
                kpos = c * ck + s_idx * LANES + lane
                key_sc[c, :, s_idx * LANES:(s_idx + 1) * LANES] = jnp.where(kpos <= qpos_b, key, INT_MIN)
        return carry

    lax.fori_loop(0, nch, score_body, 0)

    def count(make_pred):
        outs = []
        for rb in range(tq // rb_rows):
            rows = slice(rb * rb_rows, (rb + 1) * rb_rows)
            pred = make_pred(rows)

            def body(c, acc, rows=rows, pred=pred):
                for s in range(nsl):
                    acc = acc + pred(key_sc[c, rows, s * LANES:(s + 1) * LANES], c * ck + s * LANES + lane)
                return acc

            acc = lax.fori_loop(0, nch, body, jnp.zeros((rb_rows, LANES), F32))
            outs.append(jnp.sum(acc, axis=1, keepdims=True))
        return jnp.concatenate(outs, axis=0)

    def bcast(col, rows):
        return jnp.broadcast_to(col[rows], (rb_rows, LANES))

    def count_ge(thr):
        def make(rows):
            tb = bcast(thr, rows)
            return lambda blk, kpos: jnp.where(blk >= tb, 1.0, 0.0)
        return count(make)

    def count_gt(thr):
        def make(rows):
            tb = bcast(thr, rows)
            return lambda blk, kpos: jnp.where(blk > tb, 1.0, 0.0)
        return count(make)

    n_nonneg = count_ge(jnp.zeros((tq, 1), jnp.int32))
    n_all = (nch * ck).astype(F32)
    tau0 = jnp.where(n_nonneg >= k_sel, np.int32(0), INT_MIN)
    n_ge0 = jnp.where(n_nonneg >= k_sel, n_nonneg, n_all)

    def tau_step(i, carry):
        tau, n_ge = carry
        cand = tau | (jnp.int32(1) << (30 - i))
        cnt = count_ge(cand)
        take = cnt >= k_sel
        return jnp.where(take, cand, tau), jnp.where(take, cnt, n_ge)

    tau, n_ge = lax.fori_loop(0, 31, tau_step, (tau0, n_ge0))
    tau_sc[...] = tau
    msel_sc[...] = jnp.full((tq, 1), seq_len - 1, jnp.int32)

    @pl.when(jnp.max(n_ge) > k_sel)
    def _():
        tau_v = tau_sc[...]
        need = k_sel - count_gt(tau_v)

        def m_step(i, m):
            cand = m | (jnp.int32(1) << (seq_len.bit_length() - 2 - i))

            def make(rows):
                tb, cb = bcast(tau_v, rows), bcast(cand, rows)
                return lambda blk, kpos: jnp.where(blk == tb, jnp.where(kpos < cb, 1.0, 0.0), 0.0)

            return jnp.where(count(make) < need, cand, m)

        msel_sc[...] = lax.fori_loop(0, seq_len.bit_length() - 1, m_step, jnp.zeros((tq, 1), jnp.int32))

    tau_b = jnp.broadcast_to(tau_sc[...], (tq, LANES))
    msel_b = jnp.broadcast_to(msel_sc[...], (tq, LANES))

    def bias_body(c, carry):
        for s in range(nsl):
            blk = key_sc[c, :, s * LANES:(s + 1) * LANES]
            kpos = c * ck + s * LANES + lane
            tie = jnp.where(blk == tau_b, jnp.where(kpos <= msel_b, 0.0, NEG), NEG)
            sel = jnp.where(blk > tau_b, 0.0, tie)
            bias_sc[c, :, s * LANES:(s + 1) * LANES] = jnp.where(kpos <= qpos_b, sel, NEG)
        return carry

    lax.fori_loop(0, nch, bias_body, 0)

    hg = s_sc.shape[0]
    for g in range(DSA_HEADS // hg):
        slabs = [slice((g * hg + j) * LANES, (g * hg + j + 1) * LANES) for j in range(hg)]
        mv_sc[...] = jnp.full(mv_sc.shape, NEG, F32)

        def score_g(c, carry, slabs=slabs):
            off = pl.multiple_of(c * ck, ck)
            bias = bias_sc[c]
            for j, sl in enumerate(slabs):
                s = _nt_dot(dq_ref[:, sl], dk_ref[pl.ds(off, ck), sl]) + bias
                s_sc[j, c] = s
                mv_sc[j] = _lane_tile_max(mv_sc[j], s)
            return carry

        lax.fori_loop(0, nch, score_g, 0)
        for j in range(hg):
            m_sc[j] = jnp.max(mv_sc[j], axis=1, keepdims=True)
        acc_sc[...] = jnp.zeros(acc_sc.shape, F32)

        def pv_g(c, carry, slabs=slabs):
            off = pl.multiple_of(c * ck, ck)
            for j, sl in enumerate(slabs):
                p = jnp.exp(s_sc[j, c] - m_sc[j]).astype(BF16)
                acc_sc[j] += jnp.dot(p, dv_ref[pl.ds(off, ck), sl], preferred_element_type=F32)
            return carry

        lax.fori_loop(0, nch, pv_g, 0)
        for j, sl in enumerate(slabs):
            o_ref[:, sl] = _normalise(acc_sc[j]).astype(o_ref.dtype)


def _sparse_attention(dq, dk, dv, iq, ik, iw):
    B, S, W = dq.shape
    tq, ck, hg = 256, 512, 4
    n_sel = min(TOPK_MAX, S // 4)
    assert S & (S - 1) == 0 and S % ck == 0
    blk_q = lambda w: pl.BlockSpec((None, tq, w), lambda b, i: (b, i, 0))
    blk_k = lambda w: pl.BlockSpec((None, S, w), lambda b, i: (b, 0, 0), pipeline_mode=pl.Buffered(1))
    return pl.pallas_call(
        functools.partial(_dsa_kernel, tq=tq, ck=ck, n_sel=n_sel, seq_len=S),
        out_shape=jax.ShapeDtypeStruct((B, S, W), BF16),
        grid=(B, S // tq),
        in_specs=[blk_q(W), blk_k(W), blk_k(W), blk_q(iq.shape[-1]), blk_k(LANES), blk_q(LANES)],
        out_specs=blk_q(W),
        scratch_shapes=[
            pltpu.VMEM((S // ck, tq, ck), jnp.int32),
            pltpu.VMEM((S // ck, tq, ck), F32),
            pltpu.VMEM((hg, S // ck, tq, ck), F32),
            pltpu.VMEM((IDX_HEADS, tq, LANES), F32),
            pltpu.VMEM((IDX_HEADS * tq, LANES), BF16),
            pltpu.VMEM((tq, 1), jnp.int32), pltpu.VMEM((tq, 1), jnp.int32),
            pltpu.VMEM((hg, tq, LANES), F32), pltpu.VMEM((hg, tq, 1), F32), pltpu.VMEM((hg, tq, LANES), F32)],
        compiler_params=_cparams(("parallel", "arbitrary")),
        name="sparse_attention",
    )(dq, dk, dv, iq, ik, iw)


def _merge_kernel(x_ref, oa_ref, ob_ref, woa_ref, wob_ref, wg_ref, wout_ref, g_ref, b_ref, y_ref, *, alpha):
    x = x_ref[...]
    xb = x.astype(BF16)
    d = x.shape[-1]
    ya = jnp.dot(oa_ref[...], woa_ref[...], preferred_element_type=F32)
    yb = jnp.dot(ob_ref[...], wob_ref[...], preferred_element_type=F32)
    ga = jax.nn.sigmoid(jnp.dot(xb, wg_ref[:, :d], preferred_element_type=F32))
    gb = jax.nn.sigmoid(jnp.dot(xb, wg_ref[:, d:], preferred_element_type=F32))
    merged = ga * ya + gb * yb
    h = jnp.dot(merged.astype(BF16), wout_ref[...], preferred_element_type=F32)
    y_ref[...] = _layer_norm(alpha * x + h, g_ref[...], b_ref[...])


def _pad_head_rows(w, nh, d):
    n = w.shape[1]
    return jnp.pad(w.reshape(nh, d, n), ((0, 0), (0, LANES - d), (0, 0))).reshape(nh * LANES, n)


def _merge(x2, oa, ob, woa, wob, wg, wout, g, b, alpha):
    T, D = x2.shape
    tm = 256
    row = lambda w: pl.BlockSpec((tm, w), lambda i: (i, 0))
    return pl.pallas_call(
        functools.partial(_merge_kernel, alpha=alpha),
        out_shape=jax.ShapeDtypeStruct((T, D), F32),
        grid=(T // tm,),
        in_specs=[row(D), row(oa.shape[1]), row(ob.shape[1]),
                  _const_spec(woa.shape), _const_spec(wob.shape), _const_spec(wg.shape), _const_spec(wout.shape),
                  _const_spec((1, D)), _const_spec((1, D))],
        out_specs=row(D),
        compiler_params=_cparams(("parallel",)),
        name="merge_outproj_ln",
    )(x2, oa, ob, woa, wob, wg, wout, g.reshape(1, D), b.reshape(1, D))


def _ffn_kernel(x_ref, w1_ref, w3_ref, w2_ref, g_ref, b_ref, y_ref, xb_sc, acc_sc, *, alpha):
    f = pl.program_id(1)

    @pl.when(f == 0)
    def _():
        xb_sc[...] = x_ref[...].astype(BF16)
        acc_sc[...] = jnp.zeros(acc_sc.shape, F32)

    xb = xb_sc[...]
    a = jax.nn.silu(jnp.dot(xb, w1_ref[...], preferred_element_type=F32))
    a = a * jnp.dot(xb, w3_ref[...], preferred_element_type=F32)
    acc_sc[...] += jnp.dot(a.astype(BF16), w2_ref[...], preferred_element_type=F32)

    @pl.when(f == pl.num_programs(1) - 1)
    def _():
        y_ref[...] = _layer_norm(alpha * x_ref[...] + acc_sc[...], g_ref[...], b_ref[...])


def _dense_ffn(x2, w1, w3, w2, g, b, alpha):
    T, D = x2.shape
    FF = w1.shape[1]
    tm, tf = 1024, 256
    assert FF % tf == 0
    return pl.pallas_call(
        functools.partial(_ffn_kernel, alpha=alpha),
        out_shape=jax.ShapeDtypeStruct((T, D), F32),
        grid=(T // tm, FF // tf),
        in_specs=[pl.BlockSpec((tm, D), lambda i, f: (i, 0)),
                  pl.BlockSpec((D, tf), lambda i, f: (0, f)),
                  pl.BlockSpec((D, tf), lambda i, f: (0, f)),
                  pl.BlockSpec((tf, D), lambda i, f: (f, 0)),
                  _const_spec((1, D)), _const_spec((1, D))],
        out_specs=pl.BlockSpec((tm, D), lambda i, f: (i, 0)),
        scratch_shapes=[pltpu.VMEM((tm, D), BF16), pltpu.VMEM((tm, D), F32)],
        compiler_params=_cparams(("parallel", "arbitrary")),
        name="dense_ffn_ln",
    )(x2, w1.astype(BF16), w3.astype(BF16), w2.astype(BF16), g.reshape(1, D), b.reshape(1, D))


def _router_kernel(x_ref, wr_ref, route_ref):
    logits = jnp.dot(x_ref[...], wr_ref[...], preferred_element_type=F32, precision=lax.Precision.HIGHEST)
    lane = lax.broadcasted_iota(jnp.int32, logits.shape, 1).astype(F32)
    logits = jnp.where(lane < N_EXPERTS, logits, -jnp.inf)
    v1 = jnp.max(logits, axis=1, keepdims=True)
    i1 = jnp.min(jnp.where(logits == v1, lane, float(LANES)), axis=1, keepdims=True)
    rest = jnp.where(lane == i1, -jnp.inf, logits)
    v2 = jnp.max(rest, axis=1, keepdims=True)
    i2 = jnp.min(jnp.where(rest == v2, lane, float(LANES)), axis=1, keepdims=True)
    e2 = jnp.exp(v2 - v1)
    p1 = 1.0 / (1.0 + e2)
    route_ref[...] = jnp.where(lane == 0.0, i1, jnp.where(lane == 1.0, i2, jnp.where(lane == 2.0, p1, e2 * p1)))


def _router(x2, w_router):
    T, D = x2.shape
    tm = 512
    wr = jnp.pad(w_router, ((0, 0), (0, LANES - N_EXPERTS)))
    return pl.pallas_call(
        _router_kernel,
        out_shape=jax.ShapeDtypeStruct((T, LANES), F32),
        grid=(T // tm,),
        in_specs=[pl.BlockSpec((tm, D), lambda i: (i, 0)), _const_spec(wr.shape)],
        out_specs=pl.BlockSpec((tm, LANES), lambda i: (i, 0)),
        compiler_params=_cparams(("parallel",)),
        name="router",
    )(x2, wr)


def _row_gather(idx_ref, base, n_rows, src_hbm, dst_ref, sem):
    def issue(r, c):
        row = idx_ref[base + r]
        pltpu.make_async_copy(src_hbm.at[pl.ds(row, 1), :], dst_ref.at[pl.ds(r, 1), :], sem).start()
        return c

    lax.fori_loop(0, n_rows, issue, 0, unroll=8)
    pltpu.make_async_copy(src_hbm.at[pl.ds(0, n_rows), :], dst_ref, sem).wait()


def _dispatch_kernel(idx_ref, src_hbm, out_ref, sem, *, rows):
    _row_gather(idx_ref, pl.program_id(0) * rows, rows, src_hbm, out_ref, sem)


def _dispatch(tok_of_slot, x2, rows):
    n, D = tok_of_slot.shape[0], x2.shape[1]
    return pl.pallas_call(
        functools.partial(_dispatch_kernel, rows=rows),
        out_shape=jax.ShapeDtypeStruct((n, D), x2.dtype),
        grid_spec=pltpu.PrefetchScalarGridSpec(
            num_scalar_prefetch=1, grid=(n // rows,),
            in_specs=[pl.BlockSpec(memory_space=pl.ANY)],
            out_specs=pl.BlockSpec((rows, D), lambda t, idx: (t, 0)),
            scratch_shapes=[pltpu.SemaphoreType.DMA(())]),
        compiler_params=_cparams(("arbitrary",)),
        name="moe_dispatch",
    )(tok_of_slot, x2)


def _experts_kernel(te_ref, nu_ref, xs_ref, w1_ref, w3_ref, w2_ref, ys_ref, *, tf):
    t = pl.program_id(0)

    @pl.when(t < nu_ref[0])
    def _():
        xb = xs_ref[...].astype(BF16)
        acc = jnp.zeros(ys_ref.shape, F32)
        for f in range(w1_ref.shape[1] // tf):
            cols = slice(f * tf, (f + 1) * tf)
            a = jax.nn.silu(jnp.dot(xb, w1_ref[:, cols], preferred_element_type=F32))
            a = a * jnp.dot(xb, w3_ref[:, cols], preferred_element_type=F32)
            acc = acc + jnp.dot(a.astype(BF16), w2_ref[cols, :], preferred_element_type=F32)
        ys_ref[...] = acc

    @pl.when(t >= nu_ref[0])
    def _():
        ys_ref[...] = jnp.zeros(ys_ref.shape, F32)


def _experts(tile_expert, n_used, xs, w1, w3, w2, rows):
    P, D = xs.shape
    E, _, FF = w1.shape
    resident = lambda shape: pl.BlockSpec((None,) + shape, lambda t, te, nu: (te[t], 0, 0),
                                          pipeline_mode=pl.Buffered(1))
    return pl.pallas_call(
        functools.partial(_experts_kernel, tf=512),
        out_shape=jax.ShapeDtypeStruct((P, D), F32),
        grid_spec=pltpu.PrefetchScalarGridSpec(
            num_scalar_prefetch=2, grid=(P // rows,),
            in_specs=[pl.BlockSpec((rows, D), lambda t, te, nu: (t, 0)),
                      resident((D, FF)), resident((D, FF)), resident((FF, D))],
            out_specs=pl.BlockSpec((rows, D), lambda t, te, nu: (t, 0))),
        compiler_params=_cparams(("arbitrary",)),
        name="moe_experts",
    )(tile_expert, n_used, xs, w1, w3, w2)


def _combine_kernel(slot_ref, x_ref, route_ref, g_ref, b_ref, ys_hbm, y_ref, buf, sem, *, tm, alpha):
    _row_gather(slot_ref, pl.program_id(0) * 2 * tm, 2 * tm, ys_hbm, buf, sem)
    p1, p2 = route_ref[:, 2:3], route_ref[:, 3:4]
    f = p1 * buf[:tm, :] + p2 * buf[tm:, :]
    y_ref[...] = _layer_norm(alpha * x_ref[...] + f, g_ref[...], b_ref[...])


def _combine(slots, ys, x2, route, g, b, alpha, tm):
    T, D = x2.shape
    return pl.pallas_call(
        functools.partial(_combine_kernel, tm=tm, alpha=alpha),
        out_shape=jax.ShapeDtypeStruct((T, D), F32),
        grid_spec=pltpu.PrefetchScalarGridSpec(
            num_scalar_prefetch=1, grid=(T // tm,),
            in_specs=[pl.BlockSpec((tm, D), lambda i, s: (i, 0)),
                      pl.BlockSpec((tm, LANES), lambda i, s: (i, 0)),
                      pl.BlockSpec((1, D), lambda i, s: (0, 0)), pl.BlockSpec((1, D), lambda i, s: (0, 0)),
                      pl.BlockSpec(memory_space=pl.ANY)],
            out_specs=pl.BlockSpec((tm, D), lambda i, s: (i, 0)),
            scratch_shapes=[pltpu.VMEM((2 * tm, D), F32), pltpu.SemaphoreType.DMA(())]),
        compiler_params=_cparams(("arbitrary",)),
        name="moe_combine_ln",
    )(slots, x2, route, g.reshape(1, D), b.reshape(1, D), ys)


def _moe_ffn(x2, w_router, w1, w3, w2, g, b, alpha):
    T, D = x2.shape
    E = w1.shape[0]
    rows, tm = 256, 256
    route = _router(x2, w_router)

    e_flat = jnp.concatenate([route[:, 0], route[:, 1]]).astype(jnp.int32)
    onehot = (e_flat[:, None] == jnp.arange(E, dtype=jnp.int32)[None, :]).astype(jnp.int32)
    csum = jnp.cumsum(onehot, axis=0)
    counts = csum[-1]
    padded = (counts + rows - 1) // rows * rows
    ends = jnp.cumsum(padded)
    slot = jnp.sum(onehot * (csum - 1 + (ends - padded)[None, :]), axis=1)
    n_slots = 2 * T + E * rows
    tok_of_slot = jnp.zeros((n_slots,), jnp.int32).at[slot].set(jnp.arange(2 * T, dtype=jnp.int32) % T)
    tile_start = jnp.arange(n_slots // rows, dtype=jnp.int32) * rows
    tile_expert = jnp.minimum(jnp.sum((tile_start[:, None] >= ends[None, :]).astype(jnp.int32), axis=1), E - 1)
    n_used = (ends[-1:] // rows).astype(jnp.int32)

    xs = _dispatch(tok_of_slot, x2, rows)
    ys = _experts(tile_expert, n_used, xs, w1.astype(BF16), w3.astype(BF16), w2.astype(BF16), rows)
    slots = jnp.concatenate([slot[:T].reshape(T // tm, tm), slot[T:].reshape(T // tm, tm)], axis=1).reshape(-1)
    return _combine(slots, ys, x2, route, g, b, alpha, tm)


def kernel(x, positions, w_in, mla_q_norm, w_uq, mla_kv_norm, w_ukv, w_o_mla, w_o_dsa, w_out,
           ln1_g, ln1_b, ln2_g, ln2_b, dense_w1, dense_w3, dense_w2,
           moe_router, moe_w1, moe_w3, moe_w2):
    B, S, D = x.shape
    depth = w_in.shape[0]
    alpha = (2 * depth) ** 0.25
    T = B * S
    cos, sin = _rope_tables(positions)
    x2 = x.reshape(T, D)
    for l in range(depth):
        w_in_p, w_uq_p, w_ukv_p, w_gates = _prep_in_weights(w_in[l], w_uq[l], w_ukv[l])
        qm, km, vm, dq, dk, dv, iq, ik, iw = _projections(
            x2, cos, sin, w_in_p, w_uq_p, w_ukv_p, mla_q_norm[l], mla_kv_norm[l])
        b3 = lambda a: a.reshape(B, S, a.shape[-1])
        o_a = _mla_attention(b3(qm), b3(km), b3(vm)).reshape(T, -1)
        o_b = _sparse_attention(b3(dq), b3(dk), b3(dv), b3(iq), b3(ik), b3(iw)).reshape(T, -1)
        x2 = _merge(x2, o_a, o_b,
                    _pad_head_rows(w_o_mla[l], MLA_HEADS, MLA_V).astype(BF16),
                    _pad_head_rows(w_o_dsa[l], DSA_HEADS, DSA_HEAD_DIM).astype(BF16),
                    w_gates, w_out[l].astype(BF16), ln1_g[l], ln1_b[l], alpha)
        if l % 2 == 0:
            x2 = _dense_ffn(x2, dense_w1[l // 2], dense_w3[l // 2], dense_w2[l // 2], ln2_g[l], ln2_b[l], alpha)
        else:
            x2 = _moe_ffn(x2, moe_router[l // 2], moe_w1[l // 2], moe_w3[l // 2], moe_w2[l // 2],
                          ln2_g[l], ln2_b[l], alpha)
    return x2.reshape(B, S, D)
```

```python
import functools

import numpy as np
import jax
import jax.numpy as jnp
from jax import lax
from jax.experimental import pallas as pl
from jax.experimental.pallas import tpu as pltpu

F32 = jnp.float32
BF16 = jnp.bfloat16
LANES = 128
VMEM_LIMIT = 56 * 1024 * 1024

MLA_HEADS = 8
MLA_NOPE = 64
MLA_ROPE = 32
MLA_V = 64
Q_LORA = 384
KV_LORA = 256
DSA_HEADS = 8
DSA_HEAD_DIM = 64
IDX_HEADS = 8
IDX_DIM = 32
TOPK_MAX = 256
N_EXPERTS = 8
ROPE_THETA = 10000.0
NORM_EPS = 1e-5
NEG = -1e30
LOG2E = 1.4426950408889634
INT_MIN = np.int32(-2 ** 31)
ONE_LANE = 64
assert ONE_LANE >= MLA_V and ONE_LANE >= DSA_HEAD_DIM

C_CQ = 0
C_CKV = C_CQ + Q_LORA
C_KR = C_CKV + KV_LORA
C_DQ = C_KR + LANES
C_DK = C_DQ + DSA_HEADS * LANES
C_DV = C_DK + DSA_HEADS * LANES
C_IQ = C_DV + DSA_HEADS * LANES
C_IK = C_IQ + IDX_HEADS * IDX_DIM
C_IW = C_IK + LANES
C_END = C_IW + LANES


def _cparams(sem):
    return pltpu.CompilerParams(dimension_semantics=sem, vmem_limit_bytes=VMEM_LIMIT)


def _const_spec(shape):
    nd = len(shape)
    return pl.BlockSpec(shape, lambda *_: (0,) * nd)


def _nt_dot(a, b):
    return lax.dot_general(a, b, (((1,), (1,)), ((), ())), preferred_element_type=F32)


def _layer_norm(z, g, b):
    mu = jnp.mean(z, axis=-1, keepdims=True)
    zc = z - mu
    var = jnp.mean(zc * zc, axis=-1, keepdims=True)
    return zc * lax.rsqrt(var + NORM_EPS) * g + b


def _rope_tables_kernel(pos_ref, inv_ref, sgn_ref, cos_ref, sin_ref):
    pos = pos_ref[...].astype(F32)
    for p in range(2):
        ang = pos * inv_ref[p:p + 1, :]
        cos_ref[p] = jnp.cos(ang)
        sin_ref[p] = jnp.sin(ang) * sgn_ref[p:p + 1, :]


def _rope_tables(positions):
    T = positions.size
    tm = 1024
    inv32 = jnp.power(ROPE_THETA, -jnp.arange(0, IDX_DIM, 2, dtype=F32) / IDX_DIM)
    inv64 = jnp.power(ROPE_THETA, -jnp.arange(0, DSA_HEAD_DIM, 2, dtype=F32) / DSA_HEAD_DIM)
    inv = jnp.stack([jnp.tile(inv32, LANES // 16),
                     jnp.concatenate([inv64, inv64, jnp.zeros((LANES - 64,), F32)])])
    lane = np.arange(LANES)
    sgn = jnp.asarray(np.stack([np.where(lane % 32 < 16, -1.0, 1.0),
                                np.where(lane < 32, -1.0, np.where(lane < 64, 1.0, 0.0))]), F32)
    out = jax.ShapeDtypeStruct((2, T, LANES), F32)
    return pl.pallas_call(
        _rope_tables_kernel,
        out_shape=(out, out),
        grid=(T // tm,),
        in_specs=[pl.BlockSpec((tm, 1), lambda i: (i, 0)), _const_spec((2, LANES)), _const_spec((2, LANES))],
        out_specs=(pl.BlockSpec((2, tm, LANES), lambda i: (0, i, 0)),) * 2,
        compiler_params=_cparams(("parallel",)),
        name="rope_tables",
    )(positions.reshape(T, 1), inv, sgn)


def _proj_kernel(x_ref, cos_ref, sin_ref, w_in_ref, w_uq_ref, w_ukv_ref, gq_ref, gkv_ref,
                 qm_ref, km_ref, vm_ref, dq_ref, dk_ref, dv_ref, iq_ref, ik_ref, iw_ref, *, mla_scale):
    xb = x_ref[...].astype(BF16)
    lane = lax.broadcasted_iota(jnp.int32, (1, LANES), 1)
    cos32, sin32 = cos_ref[0], sin_ref[0]
    cos64, sin64 = cos_ref[1], sin_ref[1]
    in_rope = (lane >> 5) == (MLA_NOPE >> 5)
    cos_m = jnp.where(in_rope, cos32, 1.0)
    sin_m = jnp.where(in_rope, sin32, 0.0)
    first32 = (lane & (IDX_DIM - 1)) < IDX_DIM // 2
    first64 = lane < DSA_HEAD_DIM // 2
    wide_lane = lax.broadcasted_iota(jnp.int32, (1, MLA_HEADS * LANES), 1)
    one_col = jnp.where((wide_lane & (LANES - 1)) == ONE_LANE, 1.0, 0.0)

    def proj(lo, hi):
        return jnp.dot(xb, w_in_ref[:, lo:hi], preferred_element_type=F32)

    def rope(xs, cos, sin, half, first):
        rot = jnp.where(first, pltpu.roll(xs, LANES - half, 1), pltpu.roll(xs, half, 1))
        return xs * cos + rot * sin

    def rms(c, g):
        ms = jnp.mean(c * c, axis=-1, keepdims=True)
        return c * lax.rsqrt(ms + NORM_EPS) * g

    q = jnp.dot(rms(proj(C_CQ, C_CKV), gq_ref[...]).astype(BF16), w_uq_ref[...], preferred_element_type=F32)
    kv = jnp.dot(rms(proj(C_CKV, C_KR), gkv_ref[...]).astype(BF16), w_ukv_ref[...], preferred_element_type=F32)
    kr = rope(proj(C_KR, C_DQ), cos_m, sin_m, MLA_ROPE // 2, first32)
    for h in range(MLA_HEADS):
        sl = slice(h * LANES, (h + 1) * LANES)
        qm_ref[:, sl] = (rope(q[:, sl], cos_m, sin_m, MLA_ROPE // 2, first32) * mla_scale).astype(BF16)
        km_ref[:, sl] = (kv[:, sl] + kr).astype(BF16)
    vm_ref[...] = (kv[:, MLA_HEADS * LANES:] + one_col).astype(BF16)

    dq = proj(C_DQ, C_DK)
    dk = proj(C_DK, C_DV)
    for h in range(DSA_HEADS):
        sl = slice(h * LANES, (h + 1) * LANES)
        dq_ref[:, sl] = (rope(dq[:, sl], cos64, sin64, DSA_HEAD_DIM // 2, first64) * LOG2E).astype(BF16)
        dk_ref[:, sl] = rope(dk[:, sl], cos64, sin64, DSA_HEAD_DIM // 2, first64).astype(BF16)
    dv_ref[...] = (proj(C_DV, C_IQ) + one_col).astype(BF16)

    iq = proj(C_IQ, C_IK)
    for g in range(IDX_HEADS * IDX_DIM // LANES):
        sl = slice(g * LANES, (g + 1) * LANES)
        iq_ref[:, sl] = rope(iq[:, sl], cos32, sin32, IDX_DIM // 2, first32).astype(BF16)
    ik_ref[...] = rope(proj(C_IK, C_IW), cos32, sin32, IDX_DIM // 2, first32).astype(BF16)
    iw_ref[...] = proj(C_IW, C_END)


def _pad_heads(w, nh, d):
    k = w.shape[0]
    return jnp.pad(w.reshape(k, nh, d), ((0, 0), (0, 0), (0, LANES - d))).reshape(k, nh * LANES)


def _prep_in_weights(w_in, w_uq, w_ukv):
    d = w_in.shape[0]
    o = np.cumsum([0, Q_LORA, KV_LORA, MLA_ROPE, 512, 512, 512, IDX_HEADS * IDX_DIM, IDX_DIM, IDX_HEADS])
    c_q, c_kv, k_r, dq, dk, dv, iq, ik, iw = [w_in[:, o[i]:o[i + 1]] for i in range(9)]
    gates = w_in[:, o[9]:]
    kr_tile = jnp.pad(k_r, ((0, 0), (MLA_NOPE, LANES - MLA_NOPE - MLA_ROPE)))
    dsa_scale = DSA_HEAD_DIM ** -0.5
    idx_scale = (IDX_HEADS * IDX_DIM) ** -0.5
    w_in_p = jnp.concatenate([
        c_q, c_kv, kr_tile,
        _pad_heads(dq * dsa_scale, DSA_HEADS, DSA_HEAD_DIM),
        _pad_heads(dk, DSA_HEADS, DSA_HEAD_DIM),
        _pad_heads(dv, DSA_HEADS, DSA_HEAD_DIM),
        iq, jnp.tile(ik, (1, LANES // IDX_DIM)),
        jnp.pad(iw * idx_scale, ((0, 0), (0, LANES - IDX_HEADS))),
    ], axis=1).astype(BF16)
    assert w_in_p.shape == (d, C_END)
    w_uq_p = _pad_heads(w_uq, MLA_HEADS, MLA_NOPE + MLA_ROPE).astype(BF16)
    ukv = w_ukv.reshape(KV_LORA, MLA_HEADS, MLA_NOPE + MLA_V)
    w_ukv_p = jnp.concatenate([
        _pad_heads(ukv[:, :, :MLA_NOPE].reshape(KV_LORA, -1), MLA_HEADS, MLA_NOPE),
        _pad_heads(ukv[:, :, MLA_NOPE:].reshape(KV_LORA, -1), MLA_HEADS, MLA_V),
    ], axis=1).astype(BF16)
    return w_in_p, w_uq_p, w_ukv_p, gates.astype(BF16)


def _projections(x2, cos, sin, w_in_p, w_uq_p, w_ukv_p, g_q, g_kv):
    T, D = x2.shape
    tm = 256
    wide = MLA_HEADS * LANES
    row = lambda w: pl.BlockSpec((tm, w), lambda i: (i, 0))
    shapes = [(wide, BF16)] * 6 + [(IDX_HEADS * IDX_DIM, BF16), (LANES, BF16), (LANES, F32)]
    return pl.pallas_call(
        functools.partial(_proj_kernel, mla_scale=(MLA_NOPE + MLA_ROPE) ** -0.5 * LOG2E),
        out_shape=tuple(jax.ShapeDtypeStruct((T, w), dt) for w, dt in shapes),
        grid=(T // tm,),
        in_specs=[row(D),
                  pl.BlockSpec((2, tm, LANES), lambda i: (0, i, 0)),
                  pl.BlockSpec((2, tm, LANES), lambda i: (0, i, 0)),
                  _const_spec(w_in_p.shape), _const_spec(w_uq_p.shape), _const_spec(w_ukv_p.shape),
                  _const_spec((1, Q_LORA)), _const_spec((1, KV_LORA))],
        out_specs=tuple(row(w) for w, _ in shapes),
        compiler_params=_cparams(("parallel",)),
        name="projections",
    )(x2, cos, sin, w_in_p, w_uq_p, w_ukv_p, g_q.reshape(1, -1), g_kv.reshape(1, -1))


def _lane_tile_max(mv, s):
    for t in range(s.shape[1] // LANES):
        mv = jnp.maximum(mv, s[:, t * LANES:(t + 1) * LANES])
    return mv


def _normalise(acc):
    return acc / acc[:, ONE_LANE:ONE_LANE + 1]


def _mla_attn_kernel(q_ref, k_ref, v_ref, o_ref, s_sc, mv_sc, m_sc, acc_sc, *, tq):
    qi = pl.program_id(2)
    hg = s_sc.shape[0]
    slabs = [slice(j * LANES, (j + 1) * LANES) for j in range(hg)]
    mv_sc[...] = jnp.full(mv_sc.shape, NEG, F32)

    def scores(c, masked):
        off = pl.multiple_of(c * tq, tq)
        for j, sl in enumerate(slabs):
            s = _nt_dot(q_ref[:, sl], k_ref[pl.ds(off, tq), sl])
            if masked:
                row = lax.broadcasted_iota(jnp.int32, (tq, tq), 0)
                col = lax.broadcasted_iota(jnp.int32, (tq, tq), 1)
                s = jnp.where(col <= row, s, NEG)
            s_sc[j, c] = s
            mv_sc[j] = _lane_tile_max(mv_sc[j], s)

    def score_body(c, carry):
        scores(c, False)
        return carry

    lax.fori_loop(0, qi, score_body, 0)
    scores(qi, True)
    for j in range(hg):
        m_sc[j] = jnp.max(mv_sc[j], axis=1, keepdims=True)
    acc_sc[...] = jnp.zeros(acc_sc.shape, F32)

    def pv_body(c, carry):
        off = pl.multiple_of(c * tq, tq)
        for j, sl in enumerate(slabs):
            p = jnp.exp2(s_sc[j, c] - m_sc[j]).astype(BF16)
            acc_sc[j] += jnp.dot(p, v_ref[pl.ds(off, tq), sl], preferred_element_type=F32)
        return carry

    lax.fori_loop(0, qi + 1, pv_body, 0)
    for j, sl in enumerate(slabs):
        o_ref[:, sl] = _normalise(acc_sc[j]).astype(o_ref.dtype)


def _mla_attention(q, k, v):
    B, S, W = q.shape
    tq, hg = 512, 2
    blk_q = pl.BlockSpec((None, tq, hg * LANES), lambda b, h, i: (b, i, h))
    blk_kv = pl.BlockSpec((None, S, hg * LANES), lambda b, h, i: (b, 0, h))
    return pl.pallas_call(
        functools.partial(_mla_attn_kernel, tq=tq),
        out_shape=jax.ShapeDtypeStruct((B, S, W), BF16),
        grid=(B, W // (hg * LANES), S // tq),
        in_specs=[blk_q, blk_kv, blk_kv],
        out_specs=blk_q,
        scratch_shapes=[pltpu.VMEM((hg, S // tq, tq, tq), F32), pltpu.VMEM((hg, tq, LANES), F32),
                        pltpu.VMEM((hg, tq, 1), F32), pltpu.VMEM((hg, tq, LANES), F32)],
        compiler_params=_cparams(("parallel", "parallel", "arbitrary")),
        name="mla_attention",
    )(q, k, v)


def _dsa_kernel(dq_ref, dk_ref, dv_ref, iq_ref, ik_ref, iw_ref, o_ref,
                key_sc, bias_sc, s_sc, wb_sc, qm_sc, mv_sc, m_sc, acc_sc,
                *, tq, ck, n_sel, seq_len):
    qi = pl.program_id(1)
    nch = (qi * tq + tq + ck - 1) // ck
    nsl = ck // LANES
    rb_rows = LANES
    lane = lax.broadcasted_iota(jnp.int32, (1, LANES), 1)
    qpos_b = qi * tq + lax.broadcasted_iota(jnp.int32, (tq, LANES), 0)
    k_sel = float(n_sel)

    for h in range(IDX_HEADS):
        wb_sc[h] = jnp.broadcast_to(iw_ref[:, h:h + 1], (tq, LANES))
        g, lo = divmod(h * IDX_DIM, LANES)
        qg = iq_ref[:, g * LANES:(g + 1) * LANES].astype(F32)
        in_head = (lane >> 5) == (lo >> 5)
        qm_sc[h * tq:(h + 1) * tq, :] = jnp.where(in_head, qg, 0.0).astype(BF16)

    kw = 2 * LANES

    def score_body(c, carry):
        for part in range(ck // kw):
            ks = ik_ref[pl.ds(pl.multiple_of(c * ck + part * kw, kw), kw), :]
            d = _nt_dot(qm_sc[...], ks)
            for sub in range(kw // LANES):
                cols = slice(sub * LANES, (sub + 1) * LANES)
                score = jnp.zeros((tq, LANES), F32)
                for h in range(IDX_HEADS):
                    score = score + wb_sc[h] * jnp.maximum(d[h * tq:(h + 1) * tq, cols], 0.0)
                bits = lax.bitcast_convert_type(score, jnp.int32)
                key = jnp.where(bits >= 0, bits, bits ^ np.int32(0x7FFFFFFF))
                s_idx = part * (kw // LANES) + sub
                kpos = c * ck + s_idx * LANES + lane
                key_sc[c, :, s_idx * LANES:(s_idx + 1) * LANES] = jnp.where(kpos <= qpos_b, key, INT_MIN)
        return carry

    lax.fori_loop(0, nch, score_body, 0)

    def count(make_pred):
        outs = []
        for rb in range(tq // rb_rows):
            rows = slice(rb * rb_rows, (rb + 1) * rb_rows)
            pred = make_pred(rows)

            def body(c, acc, rows=rows, pred=pred):
                for s in range(nsl):
                    acc = acc + pred(key_sc[c, rows, s * LANES:(s + 1) * LANES], c * ck + s * LANES + lane)
                return acc

            acc = lax.fori_loop(0, nch, body, jnp.zeros((rb_rows, LANES), F32))
            outs.append(jnp.sum(acc, axis=1, keepdims=True))
        return jnp.concatenate(outs, axis=0)

    def bcast(col, rows):
        return jnp.broadcast_to(col[rows], (rb_rows, LANES))

    def count_ge(thr):
        def make(rows):
            tb = bcast(thr, rows)
            return lambda blk, kpos: jnp.where(blk >= tb, 1.0, 0.0)
        return count(make)

    n_nonneg = count_ge(jnp.zeros((tq, 1), jnp.int32))
    n_all = (nch * ck).astype(F32)
    tau0 = jnp.where(n_nonneg >= k_sel, np.int32(0), INT_MIN)
    n_ge0 = jnp.where(n_nonneg >= k_sel, n_nonneg, n_all)

    def tau_step(i, carry):
        tau, n_ge = carry
        cand = tau | (jnp.int32(1) << (30 - i))
        cnt = count_ge(cand)
        take = cnt >= k_sel
        return jnp.where(take, cand, tau), jnp.where(take, cnt, n_ge)

    tau, n_ge = lax.fori_loop(0, 31, tau_step, (tau0, n_ge0))

    tau_w = jnp.broadcast_to(tau, (tq, ck))
    excess_w = jnp.broadcast_to(n_ge - k_sel, (tq, ck))
    qpos_w = qi * tq + lax.broadcasted_iota(jnp.int32, (tq, ck), 0)
    kcol = lax.broadcasted_iota(jnp.int32, (tq, ck), 1)
    tri = jnp.where(lax.broadcasted_iota(jnp.int32, (ck, ck), 0) >= lax.broadcasted_iota(jnp.int32, (ck, ck), 1),
                    1.0, 0.0).astype(BF16)

    def bias_body(i, ties_after):
        c = nch - 1 - i
        blk = key_sc[c]
        is_tie = blk == tau_w
        suffix = jnp.dot(jnp.where(is_tie, 1.0, 0.0).astype(BF16), tri, preferred_element_type=F32)
        tie = jnp.where(is_tie, jnp.where(ties_after + suffix > excess_w, 0.0, NEG), NEG)
        sel = jnp.where(blk > tau_w, 0.0, tie)
        bias_sc[c] = jnp.where(c * ck + kcol <= qpos_w, sel, NEG)
        return ties_after + suffix[:, 0:1]

    lax.fori_loop(0, nch, bias_body, jnp.zeros((tq, 1), F32))

    hg = s_sc.shape[0]
    for g in range(DSA_HEADS // hg):
        slabs = [slice((g * hg + j) * LANES, (g * hg + j + 1) * LANES) for j in range(hg)]
        mv_sc[...] = jnp.full(mv_sc.shape, NEG, F32)

        def score_g(c, carry, slabs=slabs):
            off = pl.multiple_of(c * ck, ck)
            bias = bias_sc[c]
            for j, sl in enumerate(slabs):
                s = _nt_dot(dq_ref[:, sl], dk_ref[pl.ds(off, ck), sl]) + bias
                s_sc[j, c] = s
                mv_sc[j] = _lane_tile_max(mv_sc[j], s)
            return carry

        lax.fori_loop(0, nch, score_g, 0)
        for j in range(hg):
            m_sc[j] = jnp.max(mv_sc[j], axis=1, keepdims=True)
        acc_sc[...] = jnp.zeros(acc_sc.shape, F32)

        def pv_g(c, carry, slabs=slabs):
            off = pl.multiple_of(c * ck, ck)
            for j, sl in enumerate(slabs):
                p = jnp.exp2(s_sc[j, c] - m_sc[j]).astype(BF16)
                acc_sc[j] += jnp.dot(p, dv_ref[pl.ds(off, ck), sl], preferred_element_type=F32)
            return carry

        lax.fori_loop(0, nch, pv_g, 0)
        for j, sl in enumerate(slabs):
            o_ref[:, sl] = _normalise(acc_sc[j]).astype(o_ref.dtype)


def _sparse_attention(dq, dk, dv, iq, ik, iw):
    B, S, W = dq.shape
    tq, ck, hg = 256, 512, 4
    n_sel = min(TOPK_MAX, S // 4)
    assert S & (S - 1) == 0 and S % ck == 0
    blk_q = lambda w: pl.BlockSpec((None, tq, w), lambda b, i: (b, i, 0))
    blk_k = lambda w: pl.BlockSpec((None, S, w), lambda b, i: (b, 0, 0), pipeline_mode=pl.Buffered(1))
    return pl.pallas_call(
        functools.partial(_dsa_kernel, tq=tq, ck=ck, n_sel=n_sel, seq_len=S),
        out_shape=jax.ShapeDtypeStruct((B, S, W), BF16),
        grid=(B, S // tq),
        in_specs=[blk_q(W), blk_k(W), blk_k(W), blk_q(iq.shape[-1]), blk_k(LANES), blk_q(LANES)],
        out_specs=blk_q(W),
        scratch_shapes=[
            pltpu.VMEM((S // ck, tq, ck), jnp.int32),
            pltpu.VMEM((S // ck, tq, ck), F32),
            pltpu.VMEM((hg, S // ck, tq, ck), F32),
            pltpu.VMEM((IDX_HEADS, tq, LANES), F32),
            pltpu.VMEM((IDX_HEADS * tq, LANES), BF16),
            pltpu.VMEM((hg, tq, LANES), F32), pltpu.VMEM((hg, tq, 1), F32), pltpu.VMEM((hg, tq, LANES), F32)],
        compiler_params=_cparams(("parallel", "arbitrary")),
        name="sparse_attention",
    )(dq, dk, dv, iq, ik, iw)


def _merge_kernel(x_ref, oa_ref, ob_ref, woa_ref, wob_ref, wg_ref, wout_ref, g_ref, b_ref, y_ref, *, alpha):
    x = x_ref[...]
    xb = x.astype(BF16)
    d = x.shape[-1]
    ya = jnp.dot(oa_ref[...], woa_ref[...], preferred_element_type=F32)
    yb = jnp.dot(ob_ref[...], wob_ref[...], preferred_element_type=F32)
    ga = jax.nn.sigmoid(jnp.dot(xb, wg_ref[:, :d], preferred_element_type=F32))
    gb = jax.nn.sigmoid(jnp.dot(xb, wg_ref[:, d:], preferred_element_type=F32))
    merged = ga * ya + gb * yb
    h = jnp.dot(merged.astype(BF16), wout_ref[...], preferred_element_type=F32)
    y_ref[...] = _layer_norm(alpha * x + h, g_ref[...], b_ref[...])


def _pad_head_rows(w, nh, d):
    n = w.shape[1]
    return jnp.pad(w.reshape(nh, d, n), ((0, 0), (0, LANES - d), (0, 0))).reshape(nh * LANES, n)


def _merge(x2, oa, ob, woa, wob, wg, wout, g, b, alpha):
    T, D = x2.shape
    tm = 256
    row = lambda w: pl.BlockSpec((tm, w), lambda i: (i, 0))
    return pl.pallas_call(
        functools.partial(_merge_kernel, alpha=alpha),
        out_shape=jax.ShapeDtypeStruct((T, D), F32),
        grid=(T // tm,),
        in_specs=[row(D), row(oa.shape[1]), row(ob.shape[1]),
                  _const_spec(woa.shape), _const_spec(wob.shape), _const_spec(wg.shape), _const_spec(wout.shape),
                  _const_spec((1, D)), _const_spec((1, D))],
        out_specs=row(D),
        compiler_params=_cparams(("parallel",)),
        name="merge_outproj_ln",
    )(x2, oa, ob, woa, wob, wg, wout, g.reshape(1, D), b.reshape(1, D))


def _ffn_kernel(x_ref, w1_ref, w3_ref, w2_ref, g_ref, b_ref, y_ref, xb_sc, acc_sc, *, alpha):
    f = pl.program_id(1)

    @pl.when(f == 0)
    def _():
        xb_sc[...] = x_ref[...].astype(BF16)
        acc_sc[...] = jnp.zeros(acc_sc.shape, F32)

    xb = xb_sc[...]
    a = jax.nn.silu(jnp.dot(xb, w1_ref[...], preferred_element_type=F32))
    a = a * jnp.dot(xb, w3_ref[...], preferred_element_type=F32)
    acc_sc[...] += jnp.dot(a.astype(BF16), w2_ref[...], preferred_element_type=F32)

    @pl.when(f == pl.num_programs(1) - 1)
    def _():
        y_ref[...] = _layer_norm(alpha * x_ref[...] + acc_sc[...], g_ref[...], b_ref[...])


def _dense_ffn(x2, w1, w3, w2, g, b, alpha):
    T, D = x2.shape
    FF = w1.shape[1]
    tm, tf = 1024, 256
    assert FF % tf == 0
    return pl.pallas_call(
        functools.partial(_ffn_kernel, alpha=alpha),
        out_shape=jax.ShapeDtypeStruct((T, D), F32),
        grid=(T // tm, FF // tf),
        in_specs=[pl.BlockSpec((tm, D), lambda i, f: (i, 0)),
                  pl.BlockSpec((D, tf), lambda i, f: (0, f)),
                  pl.BlockSpec((D, tf), lambda i, f: (0, f)),
                  pl.BlockSpec((tf, D), lambda i, f: (f, 0)),
                  _const_spec((1, D)), _const_spec((1, D))],
        out_specs=pl.BlockSpec((tm, D), lambda i, f: (i, 0)),
        scratch_shapes=[pltpu.VMEM((tm, D), BF16), pltpu.VMEM((tm, D), F32)],
        compiler_params=_cparams(("parallel", "arbitrary")),
        name="dense_ffn_ln",
    )(x2, w1.astype(BF16), w3.astype(BF16), w2.astype(BF16), g.reshape(1, D), b.reshape(1, D))


def _router_kernel(x_ref, wr_ref, route_ref):
    logits = jnp.dot(x_ref[...], wr_ref[...], preferred_element_type=F32, precision=lax.Precision.HIGHEST)
    lane = lax.broadcasted_iota(jnp.int32, logits.shape, 1).astype(F32)
    logits = jnp.where(lane < N_EXPERTS, logits, -jnp.inf)
    v1 = jnp.max(logits, axis=1, keepdims=True)
    i1 = jnp.min(jnp.where(logits == v1, lane, float(LANES)), axis=1, keepdims=True)
    rest = jnp.where(lane == i1, -jnp.inf, logits)
    v2 = jnp.max(rest, axis=1, keepdims=True)
    i2 = jnp.min(jnp.where(rest == v2, lane, float(LANES)), axis=1, keepdims=True)
    e2 = jnp.exp(v2 - v1)
    p1 = 1.0 / (1.0 + e2)
    route_ref[...] = jnp.where(lane == 0.0, i1, jnp.where(lane == 1.0, i2, jnp.where(lane == 2.0, p1, e2 * p1)))


def _router(x2, w_router):
    T, D = x2.shape
    tm = 512
    wr = jnp.pad(w_router, ((0, 0), (0, LANES - N_EXPERTS)))
    return pl.pallas_call(
        _router_kernel,
        out_shape=jax.ShapeDtypeStruct((T, LANES), F32),
        grid=(T // tm,),
        in_specs=[pl.BlockSpec((tm, D), lambda i: (i, 0)), _const_spec(wr.shape)],
        out_specs=pl.BlockSpec((tm, LANES), lambda i: (i, 0)),
        compiler_params=_cparams(("parallel",)),
        name="router",
    )(x2, wr)


def _row_gather(idx_ref, base, n_rows, src_hbm, dst_ref, sem):
    def issue(r, c):
        row = idx_ref[base + r]
        pltpu.make_async_copy(src_hbm.at[pl.ds(row, 1), :], dst_ref.at[pl.ds(r, 1), :], sem).start()
        return c

    lax.fori_loop(0, n_rows, issue, 0, unroll=8)
    pltpu.make_async_copy(src_hbm.at[pl.ds(0, n_rows), :], dst_ref, sem).wait()


def _dispatch_kernel(idx_ref, src_hbm, out_ref, sem, *, rows):
    _row_gather(idx_ref, pl.program_id(0) * rows, rows, src_hbm, out_ref, sem)


def _dispatch(tok_of_slot, x2, rows):
    n, D = tok_of_slot.shape[0], x2.shape[1]
    return pl.pallas_call(
        functools.partial(_dispatch_kernel, rows=rows),
        out_shape=jax.ShapeDtypeStruct((n, D), x2.dtype),
        grid_spec=pltpu.PrefetchScalarGridSpec(
            num_scalar_prefetch=1, grid=(n // rows,),
            in_specs=[pl.BlockSpec(memory_space=pl.ANY)],
            out_specs=pl.BlockSpec((rows, D), lambda t, idx: (t, 0)),
            scratch_shapes=[pltpu.SemaphoreType.DMA(())]),
        compiler_params=_cparams(("arbitrary",)),
        name="moe_dispatch",
    )(tok_of_slot, x2)


def _experts_kernel(te_ref, nu_ref, xs_ref, w1_ref, w3_ref, w2_ref, ys_ref, *, tf):
    t = pl.program_id(0)

    @pl.when(t < nu_ref[0])
    def _():
        xb = xs_ref[...].astype(BF16)
        acc = jnp.zeros(ys_ref.shape, F32)
        for f in range(w1_ref.shape[1] // tf):
            cols = slice(f * tf, (f + 1) * tf)
            a = jax.nn.silu(jnp.dot(xb, w1_ref[:, cols], preferred_element_type=F32))
            a = a * jnp.dot(xb, w3_ref[:, cols], preferred_element_type=F32)
            acc = acc + jnp.dot(a.astype(BF16), w2_ref[cols, :], preferred_element_type=F32)
        ys_ref[...] = acc

    @pl.when(t >= nu_ref[0])
    def _():
        ys_ref[...] = jnp.zeros(ys_ref.shape, F32)


def _experts(tile_expert, n_used, xs, w1, w3, w2, rows):
    P, D = xs.shape
    E, _, FF = w1.shape
    resident = lambda shape: pl.BlockSpec((None,) + shape, lambda t, te, nu: (te[t], 0, 0),
                                          pipeline_mode=pl.Buffered(1))
    return pl.pallas_call(
        functools.partial(_experts_kernel, tf=512),
        out_shape=jax.ShapeDtypeStruct((P, D), F32),
        grid_spec=pltpu.PrefetchScalarGridSpec(
            num_scalar_prefetch=2, grid=(P // rows,),
            in_specs=[pl.BlockSpec((rows, D), lambda t, te, nu: (t, 0)),
                      resident((D, FF)), resident((D, FF)), resident((FF, D))],
            out_specs=pl.BlockSpec((rows, D), lambda t, te, nu: (t, 0))),
        compiler_params=_cparams(("arbitrary",)),
        name="moe_experts",
    )(tile_expert, n_used, xs, w1, w3, w2)


def _combine_kernel(slot_ref, x_ref, route_ref, g_ref, b_ref, ys_hbm, y_ref, buf, sem, *, tm, alpha):
    _row_gather(slot_ref, pl.program_id(0) * 2 * tm, 2 * tm, ys_hbm, buf, sem)
    p1, p2 = route_ref[:, 2:3], route_ref[:, 3:4]
    f = p1 * buf[:tm, :] + p2 * buf[tm:, :]
    y_ref[...] = _layer_norm(alpha * x_ref[...] + f, g_ref[...], b_ref[...])


def _combine(slots, ys, x2, route, g, b, alpha, tm):
    T, D = x2.shape
    return pl.pallas_call(
        functools.partial(_combine_kernel, tm=tm, alpha=alpha),
        out_shape=jax.ShapeDtypeStruct((T, D), F32),
        grid_spec=pltpu.PrefetchScalarGridSpec(
            num_scalar_prefetch=1, grid=(T // tm,),
            in_specs=[pl.BlockSpec((tm, D), lambda i, s: (i, 0)),
                      pl.BlockSpec((tm, LANES), lambda i, s: (i, 0)),
                      pl.BlockSpec((1, D), lambda i, s: (0, 0)), pl.BlockSpec((1, D), lambda i, s: (0, 0)),
                      pl.BlockSpec(memory_space=pl.ANY)],
            out_specs=pl.BlockSpec((tm, D), lambda i, s: (i, 0)),
            scratch_shapes=[pltpu.VMEM((2 * tm, D), F32), pltpu.SemaphoreType.DMA(())]),
        compiler_params=_cparams(("arbitrary",)),
        name="moe_combine_ln",
    )(slots, x2, route, g.reshape(1, D), b.reshape(1, D), ys)


def _moe_ffn(x2, w_router, w1, w3, w2, g, b, alpha):
    T, D = x2.shape
    E = w1.shape[0]
    rows, tm = 256, 512
    gather_rows = 4 * rows
    route = _router(x2, w_router)

    e_flat = jnp.concatenate([route[:, 0], route[:, 1]]).astype(jnp.int32)
    onehot = (e_flat[:, None] == jnp.arange(E, dtype=jnp.int32)[None, :]).astype(jnp.int32)
    csum = jnp.cumsum(onehot, axis=0)
    counts = csum[-1]
    padded = (counts + rows - 1) // rows * rows
    ends = jnp.cumsum(padded)
    slot = jnp.sum(onehot * (csum - 1 + (ends - padded)[None, :]), axis=1)
    n_slots = 2 * T + E * rows
    tok_of_slot = jnp.zeros((n_slots,), jnp.int32).at[slot].set(jnp.arange(2 * T, dtype=jnp.int32) % T)
    tile_start = jnp.arange(n_slots // rows, dtype=jnp.int32) * rows
    tile_expert = jnp.minimum(jnp.sum((tile_start[:, None] >= ends[None, :]).astype(jnp.int32), axis=1), E - 1)
    n_used = (ends[-1:] // rows).astype(jnp.int32)

    assert n_slots % gather_rows == 0
    xs = _dispatch(tok_of_slot, x2, gather_rows)
    ys = _experts(tile_expert, n_used, xs, w1.astype(BF16), w3.astype(BF16), w2.astype(BF16), rows)
    slots = jnp.concatenate([slot[:T].reshape(T // tm, tm), slot[T:].reshape(T // tm, tm)], axis=1).reshape(-1)
    return _combine(slots, ys, x2, route, g, b, alpha, tm)


def kernel(x, positions, w_in, mla_q_norm, w_uq, mla_kv_norm, w_ukv, w_o_mla, w_o_dsa, w_out,
           ln1_g, ln1_b, ln2_g, ln2_b, dense_w1, dense_w3, dense_w2,
           moe_router, moe_w1, moe_w3, moe_w2):
    B, S, D = x.shape
    depth = w_in.shape[0]
    alpha = (2 * depth) ** 0.25
    T = B * S
    cos, sin = _rope_tables(positions)
    x2 = x.reshape(T, D)
    for l in range(depth):
        w_in_p, w_uq_p, w_ukv_p, w_gates = _prep_in_weights(w_in[l], w_uq[l], w_ukv[l])
        qm, km, vm, dq, dk, dv, iq, ik, iw = _projections(
            x2, cos, sin, w_in_p, w_uq_p, w_ukv_p, mla_q_norm[l], mla_kv_norm[l])
        b3 = lambda a: a.reshape(B, S, a.shape[-1])
        o_a = _mla_attention(b3(qm), b3(km), b3(vm)).reshape(T, -1)
        o_b = _sparse_attention(b3(dq), b3(dk), b3(dv), b3(iq), b3(ik), b3(iw)).reshape(T, -1)
        x2 = _merge(x2, o_a, o_b,
                    _pad_head_rows(w_o_mla[l], MLA_HEADS, MLA_V).astype(BF16),
                    _pad_head_rows(w_o_dsa[l], DSA_HEADS, DSA_HEAD_DIM).astype(BF16),
                    w_gates, w_out[l].astype(BF16), ln1_g[l], ln1_b[l], alpha)
        if l % 2 == 0:
            x2 = _dense_ffn(x2, dense_w1[l // 2], dense_w3[l // 2], dense_w2[l // 2], ln2_g[l], ln2_b[l], alpha)
        else:
            x2 = _moe_ffn(x2, moe_router[l // 2], moe_w1[l // 2], moe_w3[l // 2], moe_w2[l // 2],
                          ln2_g[l], ln2_b[l], alpha)
    return x2.reshape(B, S, D)
```

```python
import functools

import numpy as np
import jax
import jax.numpy as jnp
from jax import lax
from jax.experimental import pallas as pl
from jax.experimental.pallas import tpu as pltpu

F32 = jnp.float32
BF16 = jnp.bfloat16
LANES = 128
VMEM_LIMIT = 56 * 1024 * 1024

MLA_HEADS = 8
MLA_NOPE = 64
MLA_ROPE = 32
MLA_V = 64
Q_LORA = 384
KV_LORA = 256
DSA_HEADS = 8
DSA_HEAD_DIM = 64
IDX_HEADS = 8
IDX_DIM = 32
TOPK_MAX = 256
N_EXPERTS = 8
ROPE_THETA = 10000.0
NORM_EPS = 1e-5
NEG = -1e30
LOG2E = 1.4426950408889634
INT_MIN = np.int32(-2 ** 31)
ONE_LANE = 64
assert ONE_LANE >= MLA_V and ONE_LANE >= DSA_HEAD_DIM

C_CQ = 0
C_CKV = C_CQ + Q_LORA
C_KR = C_CKV + KV_LORA
C_DQ = C_KR + LANES
C_DK = C_DQ + DSA_HEADS * LANES
C_DV = C_DK + DSA_HEADS * LANES
C_IQ = C_DV + DSA_HEADS * LANES
C_IK = C_IQ + IDX_HEADS * IDX_DIM
C_IW = C_IK + LANES
C_END = C_IW + LANES


def _cparams(sem):
    return pltpu.CompilerParams(dimension_semantics=sem, vmem_limit_bytes=VMEM_LIMIT)


def _const_spec(shape):
    nd = len(shape)
    return pl.BlockSpec(shape, lambda *_: (0,) * nd)


def _nt_dot(a, b):
    return lax.dot_general(a, b, (((1,), (1,)), ((), ())), preferred_element_type=F32)


def _layer_norm(z, g, b):
    mu = jnp.mean(z, axis=-1, keepdims=True)
    zc = z - mu
    var = jnp.mean(zc * zc, axis=-1, keepdims=True)
    return zc * lax.rsqrt(var + NORM_EPS) * g + b


def _rope_tables_kernel(pos_ref, inv_ref, sgn_ref, cos_ref, sin_ref):
    pos = pos_ref[...].astype(F32)
    for p in range(2):
        ang = pos * inv_ref[p:p + 1, :]
        cos_ref[p] = jnp.cos(ang)
        sin_ref[p] = jnp.sin(ang) * sgn_ref[p:p + 1, :]


def _rope_tables(positions):
    T = positions.size
    tm = 1024
    inv32 = jnp.power(ROPE_THETA, -jnp.arange(0, IDX_DIM, 2, dtype=F32) / IDX_DIM)
    inv64 = jnp.power(ROPE_THETA, -jnp.arange(0, DSA_HEAD_DIM, 2, dtype=F32) / DSA_HEAD_DIM)
    inv = jnp.stack([jnp.tile(inv32, LANES // 16),
                     jnp.concatenate([inv64, inv64, jnp.zeros((LANES - 64,), F32)])])
    lane = np.arange(LANES)
    sgn = jnp.asarray(np.stack([np.where(lane % 32 < 16, -1.0, 1.0),
                                np.where(lane < 32, -1.0, np.where(lane < 64, 1.0, 0.0))]), F32)
    out = jax.ShapeDtypeStruct((2, T, LANES), F32)
    return pl.pallas_call(
        _rope_tables_kernel,
        out_shape=(out, out),
        grid=(T // tm,),
        in_specs=[pl.BlockSpec((tm, 1), lambda i: (i, 0)), _const_spec((2, LANES)), _const_spec((2, LANES))],
        out_specs=(pl.BlockSpec((2, tm, LANES), lambda i: (0, i, 0)),) * 2,
        compiler_params=_cparams(("parallel",)),
        name="rope_tables",
    )(positions.reshape(T, 1), inv, sgn)


def _proj_kernel(x_ref, cos_ref, sin_ref, w_in_ref, w_uq_ref, w_ukv_ref, gq_ref, gkv_ref,
                 qm_ref, km_ref, vm_ref, dq_ref, dk_ref, dv_ref, iq_ref, ik_ref, iw_ref, *, mla_scale):
    xb = x_ref[...].astype(BF16)
    lane = lax.broadcasted_iota(jnp.int32, (1, LANES), 1)
    cos32, sin32 = cos_ref[0], sin_ref[0]
    cos64, sin64 = cos_ref[1], sin_ref[1]
    in_rope = (lane >> 5) == (MLA_NOPE >> 5)
    cos_m = jnp.where(in_rope, cos32, 1.0)
    sin_m = jnp.where(in_rope, sin32, 0.0)
    first32 = (lane & (IDX_DIM - 1)) < IDX_DIM // 2
    first64 = lane < DSA_HEAD_DIM // 2
    wide_lane = lax.broadcasted_iota(jnp.int32, (1, MLA_HEADS * LANES), 1)
    one_col = jnp.where((wide_lane & (LANES - 1)) == ONE_LANE, 1.0, 0.0)

    def proj(lo, hi):
        return jnp.dot(xb, w_in_ref[:, lo:hi], preferred_element_type=F32)

    def rope(xs, cos, sin, half, first):
        rot = jnp.where(first, pltpu.roll(xs, LANES - half, 1), pltpu.roll(xs, half, 1))
        return xs * cos + rot * sin

    def rms(c, g):
        ms = jnp.mean(c * c, axis=-1, keepdims=True)
        return c * lax.rsqrt(ms + NORM_EPS) * g

    q = jnp.dot(rms(proj(C_CQ, C_CKV), gq_ref[...]).astype(BF16), w_uq_ref[...], preferred_element_type=F32)
    kv = jnp.dot(rms(proj(C_CKV, C_KR), gkv_ref[...]).astype(BF16), w_ukv_ref[...], preferred_element_type=F32)
    kr = rope(proj(C_KR, C_DQ), cos_m, sin_m, MLA_ROPE // 2, first32)
    for h in range(MLA_HEADS):
        sl = slice(h * LANES, (h + 1) * LANES)
        qm_ref[:, sl] = (rope(q[:, sl], cos_m, sin_m, MLA_ROPE // 2, first32) * mla_scale).astype(BF16)
        km_ref[:, sl] = (kv[:, sl] + kr).astype(BF16)
    vm_ref[...] = (kv[:, MLA_HEADS * LANES:] + one_col).astype(BF16)

    dq = proj(C_DQ, C_DK)
    dk = proj(C_DK, C_DV)
    for h in range(DSA_HEADS):
        sl = slice(h * LANES, (h + 1) * LANES)
        dq_ref[:, sl] = (rope(dq[:, sl], cos64, sin64, DSA_HEAD_DIM // 2, first64) * LOG2E).astype(BF16)
        dk_ref[:, sl] = rope(dk[:, sl], cos64, sin64, DSA_HEAD_DIM // 2, first64).astype(BF16)
    dv_ref[...] = (proj(C_DV, C_IQ) + one_col).astype(BF16)

    iq = proj(C_IQ, C_IK)
    for g in range(IDX_HEADS * IDX_DIM // LANES):
        sl = slice(g * LANES, (g + 1) * LANES)
        iq_ref[:, sl] = rope(iq[:, sl], cos32, sin32, IDX_DIM // 2, first32).astype(BF16)
    ik_ref[...] = rope(proj(C_IK, C_IW), cos32, sin32, IDX_DIM // 2, first32).astype(BF16)
    iw_ref[...] = proj(C_IW, C_END)


def _pad_heads(w, nh, d):
    k = w.shape[0]
    return jnp.pad(w.reshape(k, nh, d), ((0, 0), (0, 0), (0, LANES - d))).reshape(k, nh * LANES)


def _prep_in_weights(w_in, w_uq, w_ukv):
    d = w_in.shape[0]
    o = np.cumsum([0, Q_LORA, KV_LORA, MLA_ROPE, 512, 512, 512, IDX_HEADS * IDX_DIM, IDX_DIM, IDX_HEADS])
    c_q, c_kv, k_r, dq, dk, dv, iq, ik, iw = [w_in[:, o[i]:o[i + 1]] for i in range(9)]
    gates = w_in[:, o[9]:]
    kr_tile = jnp.pad(k_r, ((0, 0), (MLA_NOPE, LANES - MLA_NOPE - MLA_ROPE)))
    dsa_scale = DSA_HEAD_DIM ** -0.5
    idx_scale = (IDX_HEADS * IDX_DIM) ** -0.5
    w_in_p = jnp.concatenate([
        c_q, c_kv, kr_tile,
        _pad_heads(dq * dsa_scale, DSA_HEADS, DSA_HEAD_DIM),
        _pad_heads(dk, DSA_HEADS, DSA_HEAD_DIM),
        _pad_heads(dv, DSA_HEADS, DSA_HEAD_DIM),
        iq, jnp.tile(ik, (1, LANES // IDX_DIM)),
        jnp.pad(iw * idx_scale, ((0, 0), (0, LANES - IDX_HEADS))),
    ], axis=1).astype(BF16)
    assert w_in_p.shape == (d, C_END)
    w_uq_p = _pad_heads(w_uq, MLA_HEADS, MLA_NOPE + MLA_ROPE).astype(BF16)
    ukv = w_ukv.reshape(KV_LORA, MLA_HEADS, MLA_NOPE + MLA_V)
    w_ukv_p = jnp.concatenate([
        _pad_heads(ukv[:, :, :MLA_NOPE].reshape(KV_LORA, -1), MLA_HEADS, MLA_NOPE),
        _pad_heads(ukv[:, :, MLA_NOPE:].reshape(KV_LORA, -1), MLA_HEADS, MLA_V),
    ], axis=1).astype(BF16)
    return w_in_p, w_uq_p, w_ukv_p, gates.astype(BF16)


def _projections(x2, cos, sin, w_in_p, w_uq_p, w_ukv_p, g_q, g_kv):
    T, D = x2.shape
    tm = 256
    wide = MLA_HEADS * LANES
    row = lambda w: pl.BlockSpec((tm, w), lambda i: (i, 0))
    shapes = [(wide, BF16)] * 6 + [(IDX_HEADS * IDX_DIM, BF16), (LANES, BF16), (LANES, F32)]
    return pl.pallas_call(
        functools.partial(_proj_kernel, mla_scale=(MLA_NOPE + MLA_ROPE) ** -0.5 * LOG2E),
        out_shape=tuple(jax.ShapeDtypeStruct((T, w), dt) for w, dt in shapes),
        grid=(T // tm,),
        in_specs=[row(D),
                  pl.BlockSpec((2, tm, LANES), lambda i: (0, i, 0)),
                  pl.BlockSpec((2, tm, LANES), lambda i: (0, i, 0)),
                  _const_spec(w_in_p.shape), _const_spec(w_uq_p.shape), _const_spec(w_ukv_p.shape),
                  _const_spec((1, Q_LORA)), _const_spec((1, KV_LORA))],
        out_specs=tuple(row(w) for w, _ in shapes),
        compiler_params=_cparams(("parallel",)),
        name="projections",
    )(x2, cos, sin, w_in_p, w_uq_p, w_ukv_p, g_q.reshape(1, -1), g_kv.reshape(1, -1))


def _lane_tile_max(mv, s):
    for t in range(s.shape[1] // LANES):
        mv = jnp.maximum(mv, s[:, t * LANES:(t + 1) * LANES])
    return mv


def _normalise(acc):
    return acc / acc[:, ONE_LANE:ONE_LANE + 1]


def _mla_attn_kernel(q_ref, k_ref, v_ref, o_ref, s_sc, mv_sc, m_sc, acc_sc, *, tq):
    qi = pl.program_id(2)
    hg = s_sc.shape[0]
    slabs = [slice(j * LANES, (j + 1) * LANES) for j in range(hg)]
    mv_sc[...] = jnp.full(mv_sc.shape, NEG, F32)

    def scores(c, masked):
        off = pl.multiple_of(c * tq, tq)
        for j, sl in enumerate(slabs):
            s = _nt_dot(q_ref[:, sl], k_ref[pl.ds(off, tq), sl])
            if masked:
                row = lax.broadcasted_iota(jnp.int32, (tq, tq), 0)
                col = lax.broadcasted_iota(jnp.int32, (tq, tq), 1)
                s = jnp.where(col <= row, s, NEG)
            s_sc[j, c] = s
            mv_sc[j] = _lane_tile_max(mv_sc[j], s)

    def score_body(c, carry):
        scores(c, False)
        return carry

    lax.fori_loop(0, qi, score_body, 0)
    scores(qi, True)
    for j in range(hg):
        m_sc[j] = jnp.max(mv_sc[j], axis=1, keepdims=True)
    acc_sc[...] = jnp.zeros(acc_sc.shape, F32)

    def pv_body(c, carry):
        off = pl.multiple_of(c * tq, tq)
        for j, sl in enumerate(slabs):
            p = jnp.exp2(s_sc[j, c] - m_sc[j]).astype(BF16)
            acc_sc[j] += jnp.dot(p, v_ref[pl.ds(off, tq), sl], preferred_element_type=F32)
        return carry

    lax.fori_loop(0, qi + 1, pv_body, 0)
    for j, sl in enumerate(slabs):
        o_ref[:, sl] = _normalise(acc_sc[j]).astype(o_ref.dtype)


def _mla_attention(q, k, v):
    B, S, W = q.shape
    tq, hg = 512, 2
    blk_q = pl.BlockSpec((None, tq, hg * LANES), lambda b, h, i: (b, i, h))
    blk_kv = pl.BlockSpec((None, S, hg * LANES), lambda b, h, i: (b, 0, h))
    return pl.pallas_call(
        functools.partial(_mla_attn_kernel, tq=tq),
        out_shape=jax.ShapeDtypeStruct((B, S, W), BF16),
        grid=(B, W // (hg * LANES), S // tq),
        in_specs=[blk_q, blk_kv, blk_kv],
        out_specs=blk_q,
        scratch_shapes=[pltpu.VMEM((hg, S // tq, tq, tq), F32), pltpu.VMEM((hg, tq, LANES), F32),
                        pltpu.VMEM((hg, tq, 1), F32), pltpu.VMEM((hg, tq, LANES), F32)],
        compiler_params=_cparams(("parallel", "parallel", "arbitrary")),
        name="mla_attention",
    )(q, k, v)


def _dsa_kernel(dq_ref, dk_ref, dv_ref, iq_ref, ik_ref, iw_ref, o_ref,
                key_sc, bias_sc, s_sc, wt_sc, qmt_sc, mv_sc, m_sc, acc_sc,
                *, tq, ck, n_sel):
    qi = pl.program_id(1)
    nch = (qi * tq + tq + ck - 1) // ck
    lane = lax.broadcasted_iota(jnp.int32, (1, LANES), 1)
    k_sel = float(n_sel)

    wt_sc[...] = iw_ref[...].T
    for h in range(IDX_HEADS):
        g, lo = divmod(h * IDX_DIM, LANES)
        qg = iq_ref[:, g * LANES:(g + 1) * LANES].astype(F32)
        in_head = (lane >> 5) == (lo >> 5)
        qmt_sc[:, h * tq:(h + 1) * tq] = jnp.where(in_head, qg, 0.0).T.astype(BF16)

    kw = 2 * LANES
    qpos_t = qi * tq + lax.broadcasted_iota(jnp.int32, (kw, tq), 1)
    krow = lax.broadcasted_iota(jnp.int32, (kw, tq), 0)

    def score_body(c, carry):
        for part in range(ck // kw):
            ks = ik_ref[pl.ds(pl.multiple_of(c * ck + part * kw, kw), kw), :]
            d = jnp.dot(ks, qmt_sc[...], preferred_element_type=F32)
            score = jnp.zeros((kw, tq), F32)
            for h in range(IDX_HEADS):
                score = score + wt_sc[h:h + 1, :] * jnp.maximum(d[:, h * tq:(h + 1) * tq], 0.0)
            bits = lax.bitcast_convert_type(score, jnp.int32)
            key = jnp.where(bits >= 0, bits, bits ^ np.int32(0x7FFFFFFF))
            kpos = c * ck + part * kw + krow
            key_sc[c, part * kw:(part + 1) * kw, :] = jnp.where(kpos <= qpos_t, key, INT_MIN)
        return carry

    lax.fori_loop(0, nch, score_body, 0)

    sub = 8

    n_acc = 4

    def count_ge(thr):
        thr_t = jnp.broadcast_to(thr, (sub, tq))

        def body(c, accs):
            accs = list(accs)
            for r in range(ck // sub):
                ind = jnp.where(key_sc[c, r * sub:(r + 1) * sub, :] >= thr_t, 1.0, 0.0)
                accs[r % n_acc] = accs[r % n_acc] + ind
            return tuple(accs)

        accs = lax.fori_loop(0, nch, body, (jnp.zeros((sub, tq), F32),) * n_acc)
        return jnp.sum(sum(accs[1:], accs[0]), axis=0, keepdims=True)

    n_nonneg = count_ge(jnp.zeros((1, tq), jnp.int32))
    n_all = (nch * ck).astype(F32)
    tau0 = jnp.where(n_nonneg >= k_sel, np.int32(0), INT_MIN)
    n_ge0 = jnp.where(n_nonneg >= k_sel, n_nonneg, n_all)

    def tau_step(i, carry):
        tau, n_ge = carry
        cand = tau | (jnp.int32(1) << (30 - i))
        cnt = count_ge(cand)
        take = cnt >= k_sel
        return jnp.where(take, cand, tau), jnp.where(take, cnt, n_ge)

    tau, n_ge = lax.fori_loop(0, 31, tau_step, (tau0, n_ge0))

    excess = n_ge - k_sel
    qpos_c = qi * tq + lax.broadcasted_iota(jnp.int32, (ck, tq), 1)
    krow_c = lax.broadcasted_iota(jnp.int32, (ck, tq), 0)
    tri = jnp.where(lax.broadcasted_iota(jnp.int32, (ck, ck), 0) <= lax.broadcasted_iota(jnp.int32, (ck, ck), 1),
                    1.0, 0.0).astype(BF16)

    def bias_body(i, ties_after):
        c = nch - 1 - i
        blk = key_sc[c]
        is_tie = blk == tau
        suffix = jnp.dot(tri, jnp.where(is_tie, 1.0, 0.0).astype(BF16), preferred_element_type=F32)
        tie = jnp.where(is_tie, jnp.where(ties_after + suffix > excess, 0.0, NEG), NEG)
        sel = jnp.where(blk > tau, 0.0, tie)
        bias_sc[c] = jnp.where(c * ck + krow_c <= qpos_c, sel, NEG).T
        return ties_after + suffix[0:1, :]

    lax.fori_loop(0, nch, bias_body, jnp.zeros((1, tq), F32))

    hg = s_sc.shape[0]
    for g in range(DSA_HEADS // hg):
        slabs = [slice((g * hg + j) * LANES, (g * hg + j + 1) * LANES) for j in range(hg)]
        mv_sc[...] = jnp.full(mv_sc.shape, NEG, F32)

        def score_g(c, carry, slabs=slabs):
            off = pl.multiple_of(c * ck, ck)
            bias = bias_sc[c]
            for j, sl in enumerate(slabs):
                s = _nt_dot(dq_ref[:, sl], dk_ref[pl.ds(off, ck), sl]) + bias
                s_sc[j, c] = s
                mv_sc[j] = _lane_tile_max(mv_sc[j], s)
            return carry

        lax.fori_loop(0, nch, score_g, 0)
        for j in range(hg):
            m_sc[j] = jnp.max(mv_sc[j], axis=1, keepdims=True)
        acc_sc[...] = jnp.zeros(acc_sc.shape, F32)

        def pv_g(c, carry, slabs=slabs):
            off = pl.multiple_of(c * ck, ck)
            for j, sl in enumerate(slabs):
                p = jnp.exp2(s_sc[j, c] - m_sc[j]).astype(BF16)
                acc_sc[j] += jnp.dot(p, dv_ref[pl.ds(off, ck), sl], preferred_element_type=F32)
            return carry

        lax.fori_loop(0, nch, pv_g, 0)
        for j, sl in enumerate(slabs):
            o_ref[:, sl] = _normalise(acc_sc[j]).astype(o_ref.dtype)


def _sparse_attention(dq, dk, dv, iq, ik, iw):
    B, S, W = dq.shape
    tq, ck, hg = 256, 512, 4
    n_sel = min(TOPK_MAX, S // 4)
    assert S & (S - 1) == 0 and S % ck == 0
    blk_q = lambda w: pl.BlockSpec((None, tq, w), lambda b, i: (b, i, 0))
    blk_k = lambda w: pl.BlockSpec((None, S, w), lambda b, i: (b, 0, 0), pipeline_mode=pl.Buffered(1))
    return pl.pallas_call(
        functools.partial(_dsa_kernel, tq=tq, ck=ck, n_sel=n_sel),
        out_shape=jax.ShapeDtypeStruct((B, S, W), BF16),
        grid=(B, S // tq),
        in_specs=[blk_q(W), blk_k(W), blk_k(W), blk_q(iq.shape[-1]), blk_k(LANES), blk_q(LANES)],
        out_specs=blk_q(W),
        scratch_shapes=[
            pltpu.VMEM((S // ck, ck, tq), jnp.int32),
            pltpu.VMEM((S // ck, tq, ck), F32),
            pltpu.VMEM((hg, S // ck, tq, ck), F32),
            pltpu.VMEM((LANES, tq), F32),
            pltpu.VMEM((LANES, IDX_HEADS * tq), BF16),
            pltpu.VMEM((hg, tq, LANES), F32), pltpu.VMEM((hg, tq, 1), F32), pltpu.VMEM((hg, tq, LANES), F32)],
        compiler_params=_cparams(("parallel", "arbitrary")),
        name="sparse_attention",
    )(dq, dk, dv, iq, ik, iw)


def _merge_kernel(x_ref, oa_ref, ob_ref, woa_ref, wob_ref, wg_ref, wout_ref, g_ref, b_ref, y_ref, *, alpha):
    x = x_ref[...]
    xb = x.astype(BF16)
    d = x.shape[-1]
    ya = jnp.dot(oa_ref[...], woa_ref[...], preferred_element_type=F32)
    yb = jnp.dot(ob_ref[...], wob_ref[...], preferred_element_type=F32)
    ga = jax.nn.sigmoid(jnp.dot(xb, wg_ref[:, :d], preferred_element_type=F32))
    gb = jax.nn.sigmoid(jnp.dot(xb, wg_ref[:, d:], preferred_element_type=F32))
    merged = ga * ya + gb * yb
    h = jnp.dot(merged.astype(BF16), wout_ref[...], preferred_element_type=F32)
    y_ref[...] = _layer_norm(alpha * x + h, g_ref[...], b_ref[...])


def _pad_head_rows(w, nh, d):
    n = w.shape[1]
    return jnp.pad(w.reshape(nh, d, n), ((0, 0), (0, LANES - d), (0, 0))).reshape(nh * LANES, n)


def _merge(x2, oa, ob, woa, wob, wg, wout, g, b, alpha):
    T, D = x2.shape
    tm = 256
    row = lambda w: pl.BlockSpec((tm, w), lambda i: (i, 0))
    return pl.pallas_call(
        functools.partial(_merge_kernel, alpha=alpha),
        out_shape=jax.ShapeDtypeStruct((T, D), F32),
        grid=(T // tm,),
        in_specs=[row(D), row(oa.shape[1]), row(ob.shape[1]),
                  _const_spec(woa.shape), _const_spec(wob.shape), _const_spec(wg.shape), _const_spec(wout.shape),
                  _const_spec((1, D)), _const_spec((1, D))],
        out_specs=row(D),
        compiler_params=_cparams(("parallel",)),
        name="merge_outproj_ln",
    )(x2, oa, ob, woa, wob, wg, wout, g.reshape(1, D), b.reshape(1, D))


def _ffn_kernel(x_ref, w1_ref, w3_ref, w2_ref, g_ref, b_ref, y_ref, xb_sc, acc_sc, *, alpha):
    f = pl.program_id(1)

    @pl.when(f == 0)
    def _():
        xb_sc[...] = x_ref[...].astype(BF16)
        acc_sc[...] = jnp.zeros(acc_sc.shape, F32)

    xb = xb_sc[...]
    a = jax.nn.silu(jnp.dot(xb, w1_ref[...], preferred_element_type=F32))
    a = a * jnp.dot(xb, w3_ref[...], preferred_element_type=F32)
    acc_sc[...] += jnp.dot(a.astype(BF16), w2_ref[...], preferred_element_type=F32)

    @pl.when(f == pl.num_programs(1) - 1)
    def _():
        y_ref[...] = _layer_norm(alpha * x_ref[...] + acc_sc[...], g_ref[...], b_ref[...])


def _dense_ffn(x2, w1, w3, w2, g, b, alpha):
    T, D = x2.shape
    FF = w1.shape[1]
    tm, tf = 1024, 256
    assert FF % tf == 0
    return pl.pallas_call(
        functools.partial(_ffn_kernel, alpha=alpha),
        out_shape=jax.ShapeDtypeStruct((T, D), F32),
        grid=(T // tm, FF // tf),
        in_specs=[pl.BlockSpec((tm, D), lambda i, f: (i, 0)),
                  pl.BlockSpec((D, tf), lambda i, f: (0, f)),
                  pl.BlockSpec((D, tf), lambda i, f: (0, f)),
                  pl.BlockSpec((tf, D), lambda i, f: (f, 0)),
                  _const_spec((1, D)), _const_spec((1, D))],
        out_specs=pl.BlockSpec((tm, D), lambda i, f: (i, 0)),
        scratch_shapes=[pltpu.VMEM((tm, D), BF16), pltpu.VMEM((tm, D), F32)],
        compiler_params=_cparams(("parallel", "arbitrary")),
        name="dense_ffn_ln",
    )(x2, w1.astype(BF16), w3.astype(BF16), w2.astype(BF16), g.reshape(1, D), b.reshape(1, D))


def _router_kernel(x_ref, wr_ref, route_ref):
    logits = jnp.dot(x_ref[...], wr_ref[...], preferred_element_type=F32, precision=lax.Precision.HIGHEST)
    lane = lax.broadcasted_iota(jnp.int32, logits.shape, 1).astype(F32)
    logits = jnp.where(lane < N_EXPERTS, logits, -jnp.inf)
    v1 = jnp.max(logits, axis=1, keepdims=True)
    i1 = jnp.min(jnp.where(logits == v1, lane, float(LANES)), axis=1, keepdims=True)
    rest = jnp.where(lane == i1, -jnp.inf, logits)
    v2 = jnp.max(rest, axis=1, keepdims=True)
    i2 = jnp.min(jnp.where(rest == v2, lane, float(LANES)), axis=1, keepdims=True)
    e2 = jnp.exp(v2 - v1)
    p1 = 1.0 / (1.0 + e2)
    route_ref[...] = jnp.where(lane == 0.0, i1, jnp.where(lane == 1.0, i2, jnp.where(lane == 2.0, p1, e2 * p1)))


def _router(x2, w_router):
    T, D = x2.shape
    tm = 512
    wr = jnp.pad(w_router, ((0, 0), (0, LANES - N_EXPERTS)))
    return pl.pallas_call(
        _router_kernel,
        out_shape=jax.ShapeDtypeStruct((T, LANES), F32),
        grid=(T // tm,),
        in_specs=[pl.BlockSpec((tm, D), lambda i: (i, 0)), _const_spec(wr.shape)],
        out_specs=pl.BlockSpec((tm, LANES), lambda i: (i, 0)),
        compiler_params=_cparams(("parallel",)),
        name="router",
    )(x2, wr)


def _row_gather(idx_ref, base, n_rows, src_hbm, dst_ref, sem):
    def issue(r, c):
        row = idx_ref[base + r]
        pltpu.make_async_copy(src_hbm.at[pl.ds(row, 1), :], dst_ref.at[pl.ds(r, 1), :], sem).start()
        return c

    lax.fori_loop(0, n_rows, issue, 0, unroll=8)
    pltpu.make_async_copy(src_hbm.at[pl.ds(0, n_rows), :], dst_ref, sem).wait()


def _dispatch_kernel(idx_ref, src_hbm, out_ref, sem, *, rows):
    _row_gather(idx_ref, pl.program_id(0) * rows, rows, src_hbm, out_ref, sem)


def _dispatch(tok_of_slot, x2, rows):
    n, D = tok_of_slot.shape[0], x2.shape[1]
    return pl.pallas_call(
        functools.partial(_dispatch_kernel, rows=rows),
        out_shape=jax.ShapeDtypeStruct((n, D), x2.dtype),
        grid_spec=pltpu.PrefetchScalarGridSpec(
            num_scalar_prefetch=1, grid=(n // rows,),
            in_specs=[pl.BlockSpec(memory_space=pl.ANY)],
            out_specs=pl.BlockSpec((rows, D), lambda t, idx: (t, 0)),
            scratch_shapes=[pltpu.SemaphoreType.DMA(())]),
        compiler_params=_cparams(("arbitrary",)),
        name="moe_dispatch",
    )(tok_of_slot, x2)


def _experts_kernel(te_ref, nu_ref, xs_ref, w1_ref, w3_ref, w2_ref, ys_ref, *, tf):
    t = pl.program_id(0)

    @pl.when(t < nu_ref[0])
    def _():
        xb = xs_ref[...].astype(BF16)
        acc = jnp.zeros(ys_ref.shape, F32)
        for f in range(w1_ref.shape[1] // tf):
            cols = slice(f * tf, (f + 1) * tf)
            a = jax.nn.silu(jnp.dot(xb, w1_ref[:, cols], preferred_element_type=F32))
            a = a * jnp.dot(xb, w3_ref[:, cols], preferred_element_type=F32)
            acc = acc + jnp.dot(a.astype(BF16), w2_ref[cols, :], preferred_element_type=F32)
        ys_ref[...] = acc

    @pl.when(t >= nu_ref[0])
    def _():
        ys_ref[...] = jnp.zeros(ys_ref.shape, F32)


def _experts(tile_expert, n_used, xs, w1, w3, w2, rows):
    P, D = xs.shape
    E, _, FF = w1.shape
    resident = lambda shape: pl.BlockSpec((None,) + shape, lambda t, te, nu: (te[t], 0, 0),
                                          pipeline_mode=pl.Buffered(1))
    return pl.pallas_call(
        functools.partial(_experts_kernel, tf=512),
        out_shape=jax.ShapeDtypeStruct((P, D), F32),
        grid_spec=pltpu.PrefetchScalarGridSpec(
            num_scalar_prefetch=2, grid=(P // rows,),
            in_specs=[pl.BlockSpec((rows, D), lambda t, te, nu: (t, 0)),
                      resident((D, FF)), resident((D, FF)), resident((FF, D))],
            out_specs=pl.BlockSpec((rows, D), lambda t, te, nu: (t, 0))),
        compiler_params=_cparams(("arbitrary",)),
        name="moe_experts",
    )(tile_expert, n_used, xs, w1, w3, w2)


def _combine_kernel(slot_ref, x_ref, route_ref, g_ref, b_ref, ys_hbm, y_ref, buf, sem, *, tm, alpha):
    _row_gather(slot_ref, pl.program_id(0) * 2 * tm, 2 * tm, ys_hbm, buf, sem)
    p1, p2 = route_ref[:, 2:3], route_ref[:, 3:4]
    f = p1 * buf[:tm, :] + p2 * buf[tm:, :]
    y_ref[...] = _layer_norm(alpha * x_ref[...] + f, g_ref[...], b_ref[...])


def _combine(slots, ys, x2, route, g, b, alpha, tm):
    T, D = x2.shape
    return pl.pallas_call(
        functools.partial(_combine_kernel, tm=tm, alpha=alpha),
        out_shape=jax.ShapeDtypeStruct((T, D), F32),
        grid_spec=pltpu.PrefetchScalarGridSpec(
            num_scalar_prefetch=1, grid=(T // tm,),
            in_specs=[pl.BlockSpec((tm, D), lambda i, s: (i, 0)),
                      pl.BlockSpec((tm, LANES), lambda i, s: (i, 0)),
                      pl.BlockSpec((1, D), lambda i, s: (0, 0)), pl.BlockSpec((1, D), lambda i, s: (0, 0)),
                      pl.BlockSpec(memory_space=pl.ANY)],
            out_specs=pl.BlockSpec((tm, D), lambda i, s: (i, 0)),
            scratch_shapes=[pltpu.VMEM((2 * tm, D), F32), pltpu.SemaphoreType.DMA(())]),
        compiler_params=_cparams(("arbitrary",)),
        name="moe_combine_ln",
    )(slots, x2, route, g.reshape(1, D), b.reshape(1, D), ys)


def _moe_ffn(x2, w_router, w1, w3, w2, g, b, alpha):
    T, D = x2.shape
    E = w1.shape[0]
    rows, tm = 256, 512
    gather_rows = 4 * rows
    route = _router(x2, w_router)

    e_flat = jnp.concatenate([route[:, 0], route[:, 1]]).astype(jnp.int32)
    onehot = (e_flat[:, None] == jnp.arange(E, dtype=jnp.int32)[None, :]).astype(jnp.int32)
    csum = jnp.cumsum(onehot, axis=0)
    counts = csum[-1]
    padded = (counts + rows - 1) // rows * rows
    ends = jnp.cumsum(padded)
    slot = jnp.sum(onehot * (csum - 1 + (ends - padded)[None, :]), axis=1)
    n_slots = 2 * T + E * rows
    tok_of_slot = jnp.zeros((n_slots,), jnp.int32).at[slot].set(jnp.arange(2 * T, dtype=jnp.int32) % T)
    tile_start = jnp.arange(n_slots // rows, dtype=jnp.int32) * rows
    tile_expert = jnp.minimum(jnp.sum((tile_start[:, None] >= ends[None, :]).astype(jnp.int32), axis=1), E - 1)
    n_used = (ends[-1:] // rows).astype(jnp.int32)

    assert n_slots % gather_rows == 0
    xs = _dispatch(tok_of_slot, x2, gather_rows)
    ys = _experts(tile_expert, n_used, xs, w1.astype(BF16), w3.astype(BF16), w2.astype(BF16), rows)
    slots = jnp.concatenate([slot[:T].reshape(T // tm, tm), slot[T:].reshape(T // tm, tm)], axis=1).reshape(-1)
    return _combine(slots, ys, x2, route, g, b, alpha, tm)


def kernel(x, positions, w_in, mla_q_norm, w_uq, mla_kv_norm, w_ukv, w_o_mla, w_o_dsa, w_out,
           ln1_g, ln1_b, ln2_g, ln2_b, dense_w1, dense_w3, dense_w2,
           moe_router, moe_w1, moe_w3, moe_w2):
    B, S, D = x.shape
    depth = w_in.shape[0]
    alpha = (2 * depth) ** 0.25
    T = B * S
    cos, sin = _rope_tables(positions)
    x2 = x.reshape(T, D)
    for l in range(depth):
        w_in_p, w_uq_p, w_ukv_p, w_gates = _prep_in_weights(w_in[l], w_uq[l], w_ukv[l])
        qm, km, vm, dq, dk, dv, iq, ik, iw = _projections(
            x2, cos, sin, w_in_p, w_uq_p, w_ukv_p, mla_q_norm[l], mla_kv_norm[l])
        b3 = lambda a: a.reshape(B, S, a.shape[-1])
        o_a = _mla_attention(b3(qm), b3(km), b3(vm)).reshape(T, -1)
        o_b = _sparse_attention(b3(dq), b3(dk), b3(dv), b3(iq), b3(ik), b3(iw)).reshape(T, -1)
        x2 = _merge(x2, o_a, o_b,
                    _pad_head_rows(w_o_mla[l], MLA_HEADS, MLA_V).astype(BF16),
                    _pad_head_rows(w_o_dsa[l], DSA_HEADS, DSA_HEAD_DIM).astype(BF16),
                    w_gates, w_out[l].astype(BF16), ln1_g[l], ln1_b[l], alpha)
        if l % 2 == 0:
            x2 = _dense_ffn(x2, dense_w1[l // 2], dense_w3[l // 2], dense_w2[l // 2], ln2_g[l], ln2_b[l], alpha)
        else:
            x2 = _moe_ffn(x2, moe_router[l // 2], moe_w1[l // 2], moe_w3[l // 2], moe_w2[l // 2],
                          ln2_g[l], ln2_b[l], alpha)
    return x2.reshape(B, S, D)
```

```python
import functools

import numpy as np
import jax
import jax.numpy as jnp
from jax import lax
from jax.experimental import pallas as pl
from jax.experimental.pallas import tpu as pltpu

F32 = jnp.float32
BF16 = jnp.bfloat16
LANES = 128
VMEM_LIMIT = 56 * 1024 * 1024

MLA_HEADS = 8
MLA_NOPE = 64
MLA_ROPE = 32
MLA_V = 64
Q_LORA = 384
KV_LORA = 256
DSA_HEADS = 8
DSA_HEAD_DIM = 64
IDX_HEADS = 8
IDX_DIM = 32
TOPK_MAX = 256
N_EXPERTS = 8
ROPE_THETA = 10000.0
NORM_EPS = 1e-5
NEG = -1e30
LOG2E = 1.4426950408889634
INT_MIN = np.int32(-2 ** 31)
ONE_LANE = 64
assert ONE_LANE >= MLA_V and ONE_LANE >= DSA_HEAD_DIM

C_CQ = 0
C_CKV = C_CQ + Q_LORA
C_KR = C_CKV + KV_LORA
C_DQ = C_KR + LANES
C_DK = C_DQ + DSA_HEADS * LANES
C_DV = C_DK + DSA_HEADS * LANES
C_IQ = C_DV + DSA_HEADS * LANES
C_IK = C_IQ + IDX_HEADS * IDX_DIM
C_IW = C_IK + LANES
C_END = C_IW + LANES


def _cparams(sem):
    return pltpu.CompilerParams(dimension_semantics=sem, vmem_limit_bytes=VMEM_LIMIT)


def _const_spec(shape):
    nd = len(shape)
    return pl.BlockSpec(shape, lambda *_: (0,) * nd)


def _layer_norm(z, g, b):
    mu = jnp.mean(z, axis=-1, keepdims=True)
    zc = z - mu
    var = jnp.mean(zc * zc, axis=-1, keepdims=True)
    return zc * lax.rsqrt(var + NORM_EPS) * g + b


def _rope_tables_kernel(pos_ref, inv_ref, sgn_ref, cos_ref, sin_ref):
    pos = pos_ref[...].astype(F32)
    for p in range(2):
        ang = pos * inv_ref[p:p + 1, :]
        cos_ref[p] = jnp.cos(ang)
        sin_ref[p] = jnp.sin(ang) * sgn_ref[p:p + 1, :]


def _rope_tables(positions):
    T = positions.size
    tm = 1024
    inv32 = jnp.power(ROPE_THETA, -jnp.arange(0, IDX_DIM, 2, dtype=F32) / IDX_DIM)
    inv64 = jnp.power(ROPE_THETA, -jnp.arange(0, DSA_HEAD_DIM, 2, dtype=F32) / DSA_HEAD_DIM)
    inv = jnp.stack([jnp.tile(inv32, LANES // 16),
                     jnp.concatenate([inv64, inv64, jnp.zeros((LANES - 64,), F32)])])
    lane = np.arange(LANES)
    sgn = jnp.asarray(np.stack([np.where(lane % 32 < 16, -1.0, 1.0),
                                np.where(lane < 32, -1.0, np.where(lane < 64, 1.0, 0.0))]), F32)
    out = jax.ShapeDtypeStruct((2, T, LANES), F32)
    return pl.pallas_call(
        _rope_tables_kernel,
        out_shape=(out, out),
        grid=(T // tm,),
        in_specs=[pl.BlockSpec((tm, 1), lambda i: (i, 0)), _const_spec((2, LANES)), _const_spec((2, LANES))],
        out_specs=(pl.BlockSpec((2, tm, LANES), lambda i: (0, i, 0)),) * 2,
        compiler_params=_cparams(("parallel",)),
        name="rope_tables",
    )(positions.reshape(T, 1), inv, sgn)


def _proj_kernel(x_ref, cos_ref, sin_ref, w_in_ref, w_uq_ref, w_ukv_ref, gq_ref, gkv_ref,
                 qm_ref, km_ref, vm_ref, dq_ref, dk_ref, dv_ref, iq_ref, ik_ref, iw_ref, *, mla_scale):
    xb = x_ref[...].astype(BF16)
    lane = lax.broadcasted_iota(jnp.int32, (1, LANES), 1)
    cos32, sin32 = cos_ref[0], sin_ref[0]
    cos64, sin64 = cos_ref[1], sin_ref[1]
    in_rope = (lane >> 5) == (MLA_NOPE >> 5)
    cos_m = jnp.where(in_rope, cos32, 1.0)
    sin_m = jnp.where(in_rope, sin32, 0.0)
    first32 = (lane & (IDX_DIM - 1)) < IDX_DIM // 2
    first64 = lane < DSA_HEAD_DIM // 2
    wide_lane = lax.broadcasted_iota(jnp.int32, (1, MLA_HEADS * LANES), 1)
    one_col = jnp.where((wide_lane & (LANES - 1)) == ONE_LANE, 1.0, 0.0)

    def proj(lo, hi):
        return jnp.dot(xb, w_in_ref[:, lo:hi], preferred_element_type=F32)

    def rope(xs, cos, sin, half, first):
        rot = jnp.where(first, pltpu.roll(xs, LANES - half, 1), pltpu.roll(xs, half, 1))
        return xs * cos + rot * sin

    def rms(c, g):
        ms = jnp.mean(c * c, axis=-1, keepdims=True)
        return c * lax.rsqrt(ms + NORM_EPS) * g

    q = jnp.dot(rms(proj(C_CQ, C_CKV), gq_ref[...]).astype(BF16), w_uq_ref[...], preferred_element_type=F32)
    kv = jnp.dot(rms(proj(C_CKV, C_KR), gkv_ref[...]).astype(BF16), w_ukv_ref[...], preferred_element_type=F32)
    kr = rope(proj(C_KR, C_DQ), cos_m, sin_m, MLA_ROPE // 2, first32)
    for h in range(MLA_HEADS):
        sl = slice(h * LANES, (h + 1) * LANES)
        qm_ref[:, sl] = (rope(q[:, sl], cos_m, sin_m, MLA_ROPE // 2, first32) * mla_scale).astype(BF16)
        km_ref[:, sl] = (kv[:, sl] + kr).astype(BF16)
    vm_ref[...] = (kv[:, MLA_HEADS * LANES:] + one_col).astype(BF16)

    dq = proj(C_DQ, C_DK)
    dk = proj(C_DK, C_DV)
    for h in range(DSA_HEADS):
        sl = slice(h * LANES, (h + 1) * LANES)
        dq_ref[:, sl] = (rope(dq[:, sl], cos64, sin64, DSA_HEAD_DIM // 2, first64) * LOG2E).astype(BF16)
        dk_ref[:, sl] = rope(dk[:, sl], cos64, sin64, DSA_HEAD_DIM // 2, first64).astype(BF16)
    dv_ref[...] = (proj(C_DV, C_IQ) + one_col).astype(BF16)

    iq = proj(C_IQ, C_IK)
    for g in range(IDX_HEADS * IDX_DIM // LANES):
        sl = slice(g * LANES, (g + 1) * LANES)
        iq_ref[:, sl] = rope(iq[:, sl], cos32, sin32, IDX_DIM // 2, first32).astype(BF16)
    ik_ref[...] = rope(proj(C_IK, C_IW), cos32, sin32, IDX_DIM // 2, first32).astype(BF16)
    iw_ref[...] = proj(C_IW, C_END)


def _pad_heads(w, nh, d):
    k = w.shape[0]
    return jnp.pad(w.reshape(k, nh, d), ((0, 0), (0, 0), (0, LANES - d))).reshape(k, nh * LANES)


def _prep_in_weights(w_in, w_uq, w_ukv):
    d = w_in.shape[0]
    o = np.cumsum([0, Q_LORA, KV_LORA, MLA_ROPE, 512, 512, 512, IDX_HEADS * IDX_DIM, IDX_DIM, IDX_HEADS])
    c_q, c_kv, k_r, dq, dk, dv, iq, ik, iw = [w_in[:, o[i]:o[i + 1]] for i in range(9)]
    gates = w_in[:, o[9]:]
    kr_tile = jnp.pad(k_r, ((0, 0), (MLA_NOPE, LANES - MLA_NOPE - MLA_ROPE)))
    dsa_scale = DSA_HEAD_DIM ** -0.5
    idx_scale = (IDX_HEADS * IDX_DIM) ** -0.5
    w_in_p = jnp.concatenate([
        c_q, c_kv, kr_tile,
        _pad_heads(dq * dsa_scale, DSA_HEADS, DSA_HEAD_DIM),
        _pad_heads(dk, DSA_HEADS, DSA_HEAD_DIM),
        _pad_heads(dv, DSA_HEADS, DSA_HEAD_DIM),
        iq, jnp.tile(ik, (1, LANES // IDX_DIM)),
        jnp.pad(iw * idx_scale, ((0, 0), (0, LANES - IDX_HEADS))),
    ], axis=1).astype(BF16)
    assert w_in_p.shape == (d, C_END)
    w_uq_p = _pad_heads(w_uq, MLA_HEADS, MLA_NOPE + MLA_ROPE).astype(BF16)
    ukv = w_ukv.reshape(KV_LORA, MLA_HEADS, MLA_NOPE + MLA_V)
    w_ukv_p = jnp.concatenate([
        _pad_heads(ukv[:, :, :MLA_NOPE].reshape(KV_LORA, -1), MLA_HEADS, MLA_NOPE),
        _pad_heads(ukv[:, :, MLA_NOPE:].reshape(KV_LORA, -1), MLA_HEADS, MLA_V),
    ], axis=1).astype(BF16)
    return w_in_p, w_uq_p, w_ukv_p, gates.astype(BF16)


def _projections(x2, cos, sin, w_in_p, w_uq_p, w_ukv_p, g_q, g_kv):
    T, D = x2.shape
    tm = 256
    wide = MLA_HEADS * LANES
    row = lambda w: pl.BlockSpec((tm, w), lambda i: (i, 0))
    shapes = [(wide, BF16)] * 6 + [(IDX_HEADS * IDX_DIM, BF16), (LANES, BF16), (LANES, F32)]
    return pl.pallas_call(
        functools.partial(_proj_kernel, mla_scale=(MLA_NOPE + MLA_ROPE) ** -0.5 * LOG2E),
        out_shape=tuple(jax.ShapeDtypeStruct((T, w), dt) for w, dt in shapes),
        grid=(T // tm,),
        in_specs=[row(D),
                  pl.BlockSpec((2, tm, LANES), lambda i: (0, i, 0)),
                  pl.BlockSpec((2, tm, LANES), lambda i: (0, i, 0)),
                  _const_spec(w_in_p.shape), _const_spec(w_uq_p.shape), _const_spec(w_ukv_p.shape),
                  _const_spec((1, Q_LORA)), _const_spec((1, KV_LORA))],
        out_specs=tuple(row(w) for w, _ in shapes),
        compiler_params=_cparams(("parallel",)),
        name="projections",
    )(x2, cos, sin, w_in_p, w_uq_p, w_ukv_p, g_q.reshape(1, -1), g_kv.reshape(1, -1))


SUBLANES = 8


def _sublane_tile_max(mv, s):
    parts = [mv] + [s[r * SUBLANES:(r + 1) * SUBLANES, :] for r in range(s.shape[0] // SUBLANES)]
    while len(parts) > 1:
        parts = [jnp.maximum(a, b) for a, b in zip(parts[::2], parts[1::2])] + ([parts[-1]] if len(parts) % 2 else [])
    return parts[0]


def _transpose_bf16(x):
    return x.astype(F32).T.astype(BF16)


def _normalise_t(acc_t):
    return acc_t / acc_t[ONE_LANE:ONE_LANE + 1, :]


def _mla_attn_kernel(q_ref, k_ref, v_ref, o_ref, s_sc, qt_sc, mv_sc, m_sc, acc_sc, *, tq):
    qi = pl.program_id(2)
    hg = s_sc.shape[0]
    slabs = [slice(j * LANES, (j + 1) * LANES) for j in range(hg)]
    for j, sl in enumerate(slabs):
        qt_sc[j] = _transpose_bf16(q_ref[:, sl])
    mv_sc[...] = jnp.full(mv_sc.shape, NEG, F32)

    def scores(c, masked):
        off = pl.multiple_of(c * tq, tq)
        for j, sl in enumerate(slabs):
            s = jnp.dot(k_ref[pl.ds(off, tq), sl], qt_sc[j], preferred_element_type=F32)
            if masked:
                krow = lax.broadcasted_iota(jnp.int32, (tq, tq), 0)
                qcol = lax.broadcasted_iota(jnp.int32, (tq, tq), 1)
                s = jnp.where(krow <= qcol, s, NEG)
            s_sc[j, c] = s
            mv_sc[j] = _sublane_tile_max(mv_sc[j], s)

    def score_body(c, carry):
        scores(c, False)
        return carry

    lax.fori_loop(0, qi, score_body, 0)
    scores(qi, True)
    for j in range(hg):
        m_sc[j] = jnp.max(mv_sc[j], axis=0, keepdims=True)
    acc_sc[...] = jnp.zeros(acc_sc.shape, F32)

    def pv_body(c, carry):
        off = pl.multiple_of(c * tq, tq)
        for j, sl in enumerate(slabs):
            p = jnp.exp2(s_sc[j, c] - m_sc[j]).astype(BF16)
            acc_sc[j] += jnp.dot(_transpose_bf16(v_ref[pl.ds(off, tq), sl]), p, preferred_element_type=F32)
        return carry

    lax.fori_loop(0, qi + 1, pv_body, 0)
    for j, sl in enumerate(slabs):
        o_ref[:, sl] = _normalise_t(acc_sc[j]).T.astype(o_ref.dtype)


def _mla_attention(q, k, v):
    B, S, W = q.shape
    tq, hg = 512, 2
    blk_q = pl.BlockSpec((None, tq, hg * LANES), lambda b, h, i: (b, i, h))
    blk_kv = pl.BlockSpec((None, S, hg * LANES), lambda b, h, i: (b, 0, h))
    return pl.pallas_call(
        functools.partial(_mla_attn_kernel, tq=tq),
        out_shape=jax.ShapeDtypeStruct((B, S, W), BF16),
        grid=(B, W // (hg * LANES), S // tq),
        in_specs=[blk_q, blk_kv, blk_kv],
        out_specs=blk_q,
        scratch_shapes=[pltpu.VMEM((hg, S // tq, tq, tq), F32), pltpu.VMEM((hg, LANES, tq), BF16),
                        pltpu.VMEM((hg, SUBLANES, tq), F32), pltpu.VMEM((hg, 1, tq), F32),
                        pltpu.VMEM((hg, LANES, tq), F32)],
        compiler_params=_cparams(("parallel", "parallel", "arbitrary")),
        name="mla_attention",
    )(q, k, v)


def _dsa_kernel(dq_ref, dk_ref, dv_ref, iq_ref, ik_ref, iw_ref, o_ref,
                key_sc, bias_sc, s_sc, wt_sc, qmt_sc, qt_sc, mv_sc, m_sc, acc_sc,
                *, tq, ck, n_sel):
    qi = pl.program_id(1)
    nch = (qi * tq + tq + ck - 1) // ck
    lane = lax.broadcasted_iota(jnp.int32, (1, LANES), 1)
    k_sel = float(n_sel)

    wt_sc[...] = iw_ref[...].T
    for h in range(IDX_HEADS):
        g, lo = divmod(h * IDX_DIM, LANES)
        qg = iq_ref[:, g * LANES:(g + 1) * LANES].astype(F32)
        in_head = (lane >> 5) == (lo >> 5)
        qmt_sc[:, h * tq:(h + 1) * tq] = jnp.where(in_head, qg, 0.0).T.astype(BF16)

    kw = 2 * LANES
    qpos_t = qi * tq + lax.broadcasted_iota(jnp.int32, (kw, tq), 1)
    krow = lax.broadcasted_iota(jnp.int32, (kw, tq), 0)

    def score_body(c, carry):
        for part in range(ck // kw):
            ks = ik_ref[pl.ds(pl.multiple_of(c * ck + part * kw, kw), kw), :]
            d = jnp.dot(ks, qmt_sc[...], preferred_element_type=F32)
            score = jnp.zeros((kw, tq), F32)
            for h in range(IDX_HEADS):
                score = score + wt_sc[h:h + 1, :] * jnp.maximum(d[:, h * tq:(h + 1) * tq], 0.0)
            bits = lax.bitcast_convert_type(score, jnp.int32)
            key = jnp.where(bits >= 0, bits, bits ^ np.int32(0x7FFFFFFF))
            kpos = c * ck + part * kw + krow
            key_sc[c, part * kw:(part + 1) * kw, :] = jnp.where(kpos <= qpos_t, key, INT_MIN)
        return carry

    lax.fori_loop(0, nch, score_body, 0)

    sub = 8

    n_acc = 4

    def count_ge(thr):
        thr_t = jnp.broadcast_to(thr, (sub, tq))

        def body(c, accs):
            accs = list(accs)
            for r in range(ck // sub):
                ind = jnp.where(key_sc[c, r * sub:(r + 1) * sub, :] >= thr_t, 1.0, 0.0)
                accs[r % n_acc] = accs[r % n_acc] + ind
            return tuple(accs)

        accs = lax.fori_loop(0, nch, body, (jnp.zeros((sub, tq), F32),) * n_acc)
        return jnp.sum(sum(accs[1:], accs[0]), axis=0, keepdims=True)

    n_nonneg = count_ge(jnp.zeros((1, tq), jnp.int32))
    n_all = (nch * ck).astype(F32)
    tau0 = jnp.where(n_nonneg >= k_sel, np.int32(0), INT_MIN)
    n_ge0 = jnp.where(n_nonneg >= k_sel, n_nonneg, n_all)

    def tau_step(i, carry):
        tau, n_ge = carry
        cand = tau | (jnp.int32(1) << (30 - i))
        cnt = count_ge(cand)
        take = cnt >= k_sel
        return jnp.where(take, cand, tau), jnp.where(take, cnt, n_ge)

    tau, n_ge = lax.fori_loop(0, 31, tau_step, (tau0, n_ge0))

    excess = n_ge - k_sel
    qpos_c = qi * tq + lax.broadcasted_iota(jnp.int32, (ck, tq), 1)
    krow_c = lax.broadcasted_iota(jnp.int32, (ck, tq), 0)
    tri = jnp.where(lax.broadcasted_iota(jnp.int32, (ck, ck), 0) <= lax.broadcasted_iota(jnp.int32, (ck, ck), 1),
                    1.0, 0.0).astype(BF16)

    def bias_body(i, ties_after):
        c = nch - 1 - i
        blk = key_sc[c]
        is_tie = blk == tau
        suffix = jnp.dot(tri, jnp.where(is_tie, 1.0, 0.0).astype(BF16), preferred_element_type=F32)
        tie = jnp.where(is_tie, jnp.where(ties_after + suffix > excess, 0.0, NEG), NEG)
        sel = jnp.where(blk > tau, 0.0, tie)
        bias_sc[c] = jnp.where(c * ck + krow_c <= qpos_c, sel, NEG)
        return ties_after + suffix[0:1, :]

    lax.fori_loop(0, nch, bias_body, jnp.zeros((1, tq), F32))

    hg = s_sc.shape[0]
    for g in range(DSA_HEADS // hg):
        slabs = [slice((g * hg + j) * LANES, (g * hg + j + 1) * LANES) for j in range(hg)]
        mv_sc[...] = jnp.full(mv_sc.shape, NEG, F32)
        for j, sl in enumerate(slabs):
            qt_sc[j] = _transpose_bf16(dq_ref[:, sl])

        def score_g(c, carry, slabs=slabs):
            off = pl.multiple_of(c * ck, ck)
            bias = bias_sc[c]
            for j, sl in enumerate(slabs):
                s = jnp.dot(dk_ref[pl.ds(off, ck), sl], qt_sc[j], preferred_element_type=F32) + bias
                s_sc[j, c] = s
                mv_sc[j] = _sublane_tile_max(mv_sc[j], s)
            return carry

        lax.fori_loop(0, nch, score_g, 0)
        for j in range(hg):
            m_sc[j] = jnp.max(mv_sc[j], axis=0, keepdims=True)
        acc_sc[...] = jnp.zeros(acc_sc.shape, F32)

        def pv_g(c, carry, slabs=slabs):
            off = pl.multiple_of(c * ck, ck)
            for j, sl in enumerate(slabs):
                p = jnp.exp2(s_sc[j, c] - m_sc[j]).astype(BF16)
                acc_sc[j] += jnp.dot(_transpose_bf16(dv_ref[pl.ds(off, ck), sl]), p, preferred_element_type=F32)
            return carry

        lax.fori_loop(0, nch, pv_g, 0)
        for j, sl in enumerate(slabs):
            o_ref[:, sl] = _normalise_t(acc_sc[j]).T.astype(o_ref.dtype)


def _sparse_attention(dq, dk, dv, iq, ik, iw):
    B, S, W = dq.shape
    tq, ck, hg = 256, 512, 4
    n_sel = min(TOPK_MAX, S // 4)
    assert S & (S - 1) == 0 and S % ck == 0
    blk_q = lambda w: pl.BlockSpec((None, tq, w), lambda b, i: (b, i, 0))
    blk_k = lambda w: pl.BlockSpec((None, S, w), lambda b, i: (b, 0, 0), pipeline_mode=pl.Buffered(1))
    return pl.pallas_call(
        functools.partial(_dsa_kernel, tq=tq, ck=ck, n_sel=n_sel),
        out_shape=jax.ShapeDtypeStruct((B, S, W), BF16),
        grid=(B, S // tq),
        in_specs=[blk_q(W), blk_k(W), blk_k(W), blk_q(iq.shape[-1]), blk_k(LANES), blk_q(LANES)],
        out_specs=blk_q(W),
        scratch_shapes=[
            pltpu.VMEM((S // ck, ck, tq), jnp.int32),
            pltpu.VMEM((S // ck, ck, tq), F32),
            pltpu.VMEM((hg, S // ck, ck, tq), F32),
            pltpu.VMEM((LANES, tq), F32),
            pltpu.VMEM((LANES, IDX_HEADS * tq), BF16),
            pltpu.VMEM((hg, LANES, tq), BF16),
            pltpu.VMEM((hg, SUBLANES, tq), F32), pltpu.VMEM((hg, 1, tq), F32), pltpu.VMEM((hg, LANES, tq), F32)],
        compiler_params=_cparams(("parallel", "arbitrary")),
        name="sparse_attention",
    )(dq, dk, dv, iq, ik, iw)


def _merge_kernel(x_ref, oa_ref, ob_ref, woa_ref, wob_ref, wg_ref, wout_ref, g_ref, b_ref, y_ref, *, alpha):
    x = x_ref[...]
    xb = x.astype(BF16)
    d = x.shape[-1]
    ya = jnp.dot(oa_ref[...], woa_ref[...], preferred_element_type=F32)
    yb = jnp.dot(ob_ref[...], wob_ref[...], preferred_element_type=F32)
    ga = jax.nn.sigmoid(jnp.dot(xb, wg_ref[:, :d], preferred_element_type=F32))
    gb = jax.nn.sigmoid(jnp.dot(xb, wg_ref[:, d:], preferred_element_type=F32))
    merged = ga * ya + gb * yb
    h = jnp.dot(merged.astype(BF16), wout_ref[...], preferred_element_type=F32)
    y_ref[...] = _layer_norm(alpha * x + h, g_ref[...], b_ref[...])


def _pad_head_rows(w, nh, d):
    n = w.shape[1]
    return jnp.pad(w.reshape(nh, d, n), ((0, 0), (0, LANES - d), (0, 0))).reshape(nh * LANES, n)


def _merge(x2, oa, ob, woa, wob, wg, wout, g, b, alpha):
    T, D = x2.shape
    tm = 256
    row = lambda w: pl.BlockSpec((tm, w), lambda i: (i, 0))
    return pl.pallas_call(
        functools.partial(_merge_kernel, alpha=alpha),
        out_shape=jax.ShapeDtypeStruct((T, D), F32),
        grid=(T // tm,),
        in_specs=[row(D), row(oa.shape[1]), row(ob.shape[1]),
                  _const_spec(woa.shape), _const_spec(wob.shape), _const_spec(wg.shape), _const_spec(wout.shape),
                  _const_spec((1, D)), _const_spec((1, D))],
        out_specs=row(D),
        compiler_params=_cparams(("parallel",)),
        name="merge_outproj_ln",
    )(x2, oa, ob, woa, wob, wg, wout, g.reshape(1, D), b.reshape(1, D))


def _ffn_kernel(x_ref, w1_ref, w3_ref, w2_ref, g_ref, b_ref, y_ref, xb_sc, acc_sc, *, alpha):
    f = pl.program_id(1)

    @pl.when(f == 0)
    def _():
        xb_sc[...] = x_ref[...].astype(BF16)
        acc_sc[...] = jnp.zeros(acc_sc.shape, F32)

    xb = xb_sc[...]
    a = jax.nn.silu(jnp.dot(xb, w1_ref[...], preferred_element_type=F32))
    a = a * jnp.dot(xb, w3_ref[...], preferred_element_type=F32)
    acc_sc[...] += jnp.dot(a.astype(BF16), w2_ref[...], preferred_element_type=F32)

    @pl.when(f == pl.num_programs(1) - 1)
    def _():
        y_ref[...] = _layer_norm(alpha * x_ref[...] + acc_sc[...], g_ref[...], b_ref[...])


def _dense_ffn(x2, w1, w3, w2, g, b, alpha):
    T, D = x2.shape
    FF = w1.shape[1]
    tm, tf = 1024, 256
    assert FF % tf == 0
    return pl.pallas_call(
        functools.partial(_ffn_kernel, alpha=alpha),
        out_shape=jax.ShapeDtypeStruct((T, D), F32),
        grid=(T // tm, FF // tf),
        in_specs=[pl.BlockSpec((tm, D), lambda i, f: (i, 0)),
                  pl.BlockSpec((D, tf), lambda i, f: (0, f)),
                  pl.BlockSpec((D, tf), lambda i, f: (0, f)),
                  pl.BlockSpec((tf, D), lambda i, f: (f, 0)),
                  _const_spec((1, D)), _const_spec((1, D))],
        out_specs=pl.BlockSpec((tm, D), lambda i, f: (i, 0)),
        scratch_shapes=[pltpu.VMEM((tm, D), BF16), pltpu.VMEM((tm, D), F32)],
        compiler_params=_cparams(("parallel", "arbitrary")),
        name="dense_ffn_ln",
    )(x2, w1.astype(BF16), w3.astype(BF16), w2.astype(BF16), g.reshape(1, D), b.reshape(1, D))


def _router_kernel(x_ref, wr_ref, route_ref):
    logits = jnp.dot(x_ref[...], wr_ref[...], preferred_element_type=F32, precision=lax.Precision.HIGHEST)
    lane = lax.broadcasted_iota(jnp.int32, logits.shape, 1).astype(F32)
    logits = jnp.where(lane < N_EXPERTS, logits, -jnp.inf)
    v1 = jnp.max(logits, axis=1, keepdims=True)
    i1 = jnp.min(jnp.where(logits == v1, lane, float(LANES)), axis=1, keepdims=True)
    rest = jnp.where(lane == i1, -jnp.inf, logits)
    v2 = jnp.max(rest, axis=1, keepdims=True)
    i2 = jnp.min(jnp.where(rest == v2, lane, float(LANES)), axis=1, keepdims=True)
    e2 = jnp.exp(v2 - v1)
    p1 = 1.0 / (1.0 + e2)
    route_ref[...] = jnp.where(lane == 0.0, i1, jnp.where(lane == 1.0, i2, jnp.where(lane == 2.0, p1, e2 * p1)))


def _router(x2, w_router):
    T, D = x2.shape
    tm = 512
    wr = jnp.pad(w_router, ((0, 0), (0, LANES - N_EXPERTS)))
    return pl.pallas_call(
        _router_kernel,
        out_shape=jax.ShapeDtypeStruct((T, LANES), F32),
        grid=(T // tm,),
        in_specs=[pl.BlockSpec((tm, D), lambda i: (i, 0)), _const_spec(wr.shape)],
        out_specs=pl.BlockSpec((tm, LANES), lambda i: (i, 0)),
        compiler_params=_cparams(("parallel",)),
        name="router",
    )(x2, wr)


def _row_gather(idx_ref, base, n_rows, src_hbm, dst_ref, sem):
    def issue(r, c):
        row = idx_ref[base + r]
        pltpu.make_async_copy(src_hbm.at[pl.ds(row, 1), :], dst_ref.at[pl.ds(r, 1), :], sem).start()
        return c

    lax.fori_loop(0, n_rows, issue, 0, unroll=8)
    pltpu.make_async_copy(src_hbm.at[pl.ds(0, n_rows), :], dst_ref, sem).wait()


def _dispatch_kernel(idx_ref, src_hbm, out_ref, sem, *, rows):
    _row_gather(idx_ref, pl.program_id(0) * rows, rows, src_hbm, out_ref, sem)


def _dispatch(tok_of_slot, x2, rows):
    n, D = tok_of_slot.shape[0], x2.shape[1]
    return pl.pallas_call(
        functools.partial(_dispatch_kernel, rows=rows),
        out_shape=jax.ShapeDtypeStruct((n, D), x2.dtype),
        grid_spec=pltpu.PrefetchScalarGridSpec(
            num_scalar_prefetch=1, grid=(n // rows,),
            in_specs=[pl.BlockSpec(memory_space=pl.ANY)],
            out_specs=pl.BlockSpec((rows, D), lambda t, idx: (t, 0)),
            scratch_shapes=[pltpu.SemaphoreType.DMA(())]),
        compiler_params=_cparams(("arbitrary",)),
        name="moe_dispatch",
    )(tok_of_slot, x2)


def _experts_kernel(te_ref, nu_ref, xs_ref, w1_ref, w3_ref, w2_ref, ys_ref, *, tf):
    t = pl.program_id(0)

    @pl.when(t < nu_ref[0])
    def _():
        xb = xs_ref[...].astype(BF16)
        acc = jnp.zeros(ys_ref.shape, F32)
        for f in range(w1_ref.shape[1] // tf):
            cols = slice(f * tf, (f + 1) * tf)
            a = jax.nn.silu(jnp.dot(xb, w1_ref[:, cols], preferred_element_type=F32))
            a = a * jnp.dot(xb, w3_ref[:, cols], preferred_element_type=F32)
            acc = acc + jnp.dot(a.astype(BF16), w2_ref[cols, :], preferred_element_type=F32)
        ys_ref[...] = acc

    @pl.when(t >= nu_ref[0])
    def _():
        ys_ref[...] = jnp.zeros(ys_ref.shape, F32)


def _experts(tile_expert, n_used, xs, w1, w3, w2, rows):
    P, D = xs.shape
    E, _, FF = w1.shape
    resident = lambda shape: pl.BlockSpec((None,) + shape, lambda t, te, nu: (te[t], 0, 0),
                                          pipeline_mode=pl.Buffered(1))
    return pl.pallas_call(
        functools.partial(_experts_kernel, tf=512),
        out_shape=jax.ShapeDtypeStruct((P, D), F32),
        grid_spec=pltpu.PrefetchScalarGridSpec(
            num_scalar_prefetch=2, grid=(P // rows,),
            in_specs=[pl.BlockSpec((rows, D), lambda t, te, nu: (t, 0)),
                      resident((D, FF)), resident((D, FF)), resident((FF, D))],
            out_specs=pl.BlockSpec((rows, D), lambda t, te, nu: (t, 0))),
        compiler_params=_cparams(("arbitrary",)),
        name="moe_experts",
    )(tile_expert, n_used, xs, w1, w3, w2)


def _combine_kernel(slot_ref, x_ref, route_ref, g_ref, b_ref, ys_hbm, y_ref, buf, sem, *, tm, alpha):
    _row_gather(slot_ref, pl.program_id(0) * 2 * tm, 2 * tm, ys_hbm, buf, sem)
    p1, p2 = route_ref[:, 2:3], route_ref[:, 3:4]
    f = p1 * buf[:tm, :] + p2 * buf[tm:, :]
    y_ref[...] = _layer_norm(alpha * x_ref[...] + f, g_ref[...], b_ref[...])


def _combine(slots, ys, x2, route, g, b, alpha, tm):
    T, D = x2.shape
    return pl.pallas_call(
        functools.partial(_combine_kernel, tm=tm, alpha=alpha),
        out_shape=jax.ShapeDtypeStruct((T, D), F32),
        grid_spec=pltpu.PrefetchScalarGridSpec(
            num_scalar_prefetch=1, grid=(T // tm,),
            in_specs=[pl.BlockSpec((tm, D), lambda i, s: (i, 0)),
                      pl.BlockSpec((tm, LANES), lambda i, s: (i, 0)),
                      pl.BlockSpec((1, D), lambda i, s: (0, 0)), pl.BlockSpec((1, D), lambda i, s: (0, 0)),
                      pl.BlockSpec(memory_space=pl.ANY)],
            out_specs=pl.BlockSpec((tm, D), lambda i, s: (i, 0)),
            scratch_shapes=[pltpu.VMEM((2 * tm, D), F32), pltpu.SemaphoreType.DMA(())]),
        compiler_params=_cparams(("arbitrary",)),
        name="moe_combine_ln",
    )(slots, x2, route, g.reshape(1, D), b.reshape(1, D), ys)


def _moe_ffn(x2, w_router, w1, w3, w2, g, b, alpha):
    T, D = x2.shape
    E = w1.shape[0]
    rows, tm = 256, 512
    gather_rows = 4 * rows
    route = _router(x2, w_router)

    e_flat = jnp.concatenate([route[:, 0], route[:, 1]]).astype(jnp.int32)
    onehot = (e_flat[:, None] == jnp.arange(E, dtype=jnp.int32)[None, :]).astype(jnp.int32)
    csum = jnp.cumsum(onehot, axis=0)
    counts = csum[-1]
    padded = (counts + rows - 1) // rows * rows
    ends = jnp.cumsum(padded)
    slot = jnp.sum(onehot * (csum - 1 + (ends - padded)[None, :]), axis=1)
    n_slots = 2 * T + E * rows
    tok_of_slot = jnp.zeros((n_slots,), jnp.int32).at[slot].set(jnp.arange(2 * T, dtype=jnp.int32) % T)
    tile_start = jnp.arange(n_slots // rows, dtype=jnp.int32) * rows
    tile_expert = jnp.minimum(jnp.sum((tile_start[:, None] >= ends[None, :]).astype(jnp.int32), axis=1), E - 1)
    n_used = (ends[-1:] // rows).astype(jnp.int32)

    assert n_slots % gather_rows == 0
    xs = _dispatch(tok_of_slot, x2, gather_rows)
    ys = _experts(tile_expert, n_used, xs, w1.astype(BF16), w3.astype(BF16), w2.astype(BF16), rows)
    slots = jnp.concatenate([slot[:T].reshape(T // tm, tm), slot[T:].reshape(T // tm, tm)], axis=1).reshape(-1)
    return _combine(slots, ys, x2, route, g, b, alpha, tm)


def kernel(x, positions, w_in, mla_q_norm, w_uq, mla_kv_norm, w_ukv, w_o_mla, w_o_dsa, w_out,
           ln1_g, ln1_b, ln2_g, ln2_b, dense_w1, dense_w3, dense_w2,
           moe_router, moe_w1, moe_w3, moe_w2):
    B, S, D = x.shape
    depth = w_in.shape[0]
    alpha = (2 * depth) ** 0.25
    T = B * S
    cos, sin = _rope_tables(positions)
    x2 = x.reshape(T, D)
    for l in range(depth):
        w_in_p, w_uq_p, w_ukv_p, w_gates = _prep_in_weights(w_in[l], w_uq[l], w_ukv[l])
        qm, km, vm, dq, dk, dv, iq, ik, iw = _projections(
            x2, cos, sin, w_in_p, w_uq_p, w_ukv_p, mla_q_norm[l], mla_kv_norm[l])
        b3 = lambda a: a.reshape(B, S, a.shape[-1])
        o_a = _mla_attention(b3(qm), b3(km), b3(vm)).reshape(T, -1)
        o_b = _sparse_attention(b3(dq), b3(dk), b3(dv), b3(iq), b3(ik), b3(iw)).reshape(T, -1)
        x2 = _merge(x2, o_a, o_b,
                    _pad_head_rows(w_o_mla[l], MLA_HEADS, MLA_V).astype(BF16),
                    _pad_head_rows(w_o_dsa[l], DSA_HEADS, DSA_HEAD_DIM).astype(BF16),
                    w_gates, w_out[l].astype(BF16), ln1_g[l], ln1_b[l], alpha)
        if l % 2 == 0:
            x2 = _dense_ffn(x2, dense_w1[l // 2], dense_w3[l // 2], dense_w2[l // 2], ln2_g[l], ln2_b[l], alpha)
        else:
            x2 = _moe_ffn(x2, moe_router[l // 2], moe_w1[l // 2], moe_w3[l // 2], moe_w2[l // 2],
                          ln2_g[l], ln2_b[l], alpha)
    return x2.reshape(B, S, D)
```

```python
import functools

import numpy as np
import jax
import jax.numpy as jnp
from jax import lax
from jax.experimental import pallas as pl
from jax.experimental.pallas import tpu as pltpu

F32 = jnp.float32
BF16 = jnp.bfloat16
LANES = 128
VMEM_LIMIT = 56 * 1024 * 1024

MLA_HEADS = 8
MLA_NOPE = 64
MLA_ROPE = 32
MLA_V = 64
Q_LORA = 384
KV_LORA = 256
DSA_HEADS = 8
DSA_HEAD_DIM = 64
IDX_HEADS = 8
IDX_DIM = 32
TOPK_MAX = 256
N_EXPERTS = 8
ROPE_THETA = 10000.0
NORM_EPS = 1e-5
NEG = -1e30
LOG2E = 1.4426950408889634
INT_MIN = np.int32(-2 ** 31)
ONE_LANE = 64
assert ONE_LANE >= MLA_V and ONE_LANE >= DSA_HEAD_DIM

C_CQ = 0
C_CKV = C_CQ + Q_LORA
C_KR = C_CKV + KV_LORA
C_DQ = C_KR + LANES
C_DK = C_DQ + DSA_HEADS * LANES
C_DV = C_DK + DSA_HEADS * LANES
C_IQ = C_DV + DSA_HEADS * LANES
C_IK = C_IQ + IDX_HEADS * IDX_DIM
C_IW = C_IK + LANES
C_END = C_IW + LANES


def _cparams(sem):
    return pltpu.CompilerParams(dimension_semantics=sem, vmem_limit_bytes=VMEM_LIMIT)


def _const_spec(shape):
    nd = len(shape)
    return pl.BlockSpec(shape, lambda *_: (0,) * nd)


def _layer_norm(z, g, b):
    mu = jnp.mean(z, axis=-1, keepdims=True)
    zc = z - mu
    var = jnp.mean(zc * zc, axis=-1, keepdims=True)
    return zc * lax.rsqrt(var + NORM_EPS) * g + b


def _rope_tables_kernel(pos_ref, inv_ref, sgn_ref, cos_ref, sin_ref):
    pos = pos_ref[...].astype(F32)
    for p in range(2):
        ang = pos * inv_ref[p:p + 1, :]
        cos_ref[p] = jnp.cos(ang)
        sin_ref[p] = jnp.sin(ang) * sgn_ref[p:p + 1, :]


def _rope_tables(positions):
    T = positions.size
    tm = 1024
    inv32 = jnp.power(ROPE_THETA, -jnp.arange(0, IDX_DIM, 2, dtype=F32) / IDX_DIM)
    inv64 = jnp.power(ROPE_THETA, -jnp.arange(0, DSA_HEAD_DIM, 2, dtype=F32) / DSA_HEAD_DIM)
    inv = jnp.stack([jnp.tile(inv32, LANES // 16),
                     jnp.concatenate([inv64, inv64, jnp.zeros((LANES - 64,), F32)])])
    lane = np.arange(LANES)
    sgn = jnp.asarray(np.stack([np.where(lane % 32 < 16, -1.0, 1.0),
                                np.where(lane < 32, -1.0, np.where(lane < 64, 1.0, 0.0))]), F32)
    out = jax.ShapeDtypeStruct((2, T, LANES), F32)
    return pl.pallas_call(
        _rope_tables_kernel,
        out_shape=(out, out),
        grid=(T // tm,),
        in_specs=[pl.BlockSpec((tm, 1), lambda i: (i, 0)), _const_spec((2, LANES)), _const_spec((2, LANES))],
        out_specs=(pl.BlockSpec((2, tm, LANES), lambda i: (0, i, 0)),) * 2,
        compiler_params=_cparams(("parallel",)),
        name="rope_tables",
    )(positions.reshape(T, 1), inv, sgn)


def _proj_kernel(x_ref, cos_ref, sin_ref, w_in_ref, w_uq_ref, w_ukv_ref, gq_ref, gkv_ref,
                 qm_ref, km_ref, vm_ref, dq_ref, dk_ref, dv_ref, iq_ref, ik_ref, iw_ref, *, mla_scale):
    xb = x_ref[...].astype(BF16)
    lane = lax.broadcasted_iota(jnp.int32, (1, LANES), 1)
    cos32, sin32 = cos_ref[0], sin_ref[0]
    cos64, sin64 = cos_ref[1], sin_ref[1]
    in_rope = (lane >> 5) == (MLA_NOPE >> 5)
    cos_m = jnp.where(in_rope, cos32, 1.0)
    sin_m = jnp.where(in_rope, sin32, 0.0)
    first32 = (lane & (IDX_DIM - 1)) < IDX_DIM // 2
    first64 = lane < DSA_HEAD_DIM // 2
    wide_lane = lax.broadcasted_iota(jnp.int32, (1, MLA_HEADS * LANES), 1)
    one_col = jnp.where((wide_lane & (LANES - 1)) == ONE_LANE, 1.0, 0.0)

    def proj(lo, hi):
        return jnp.dot(xb, w_in_ref[:, lo:hi], preferred_element_type=F32)

    def rope(xs, cos, sin, half, first):
        rot = jnp.where(first, pltpu.roll(xs, LANES - half, 1), pltpu.roll(xs, half, 1))
        return xs * cos + rot * sin

    def rms(c, g):
        ms = jnp.mean(c * c, axis=-1, keepdims=True)
        return c * lax.rsqrt(ms + NORM_EPS) * g

    q = jnp.dot(rms(proj(C_CQ, C_CKV), gq_ref[...]).astype(BF16), w_uq_ref[...], preferred_element_type=F32)
    kv = jnp.dot(rms(proj(C_CKV, C_KR), gkv_ref[...]).astype(BF16), w_ukv_ref[...], preferred_element_type=F32)
    kr = rope(proj(C_KR, C_DQ), cos_m, sin_m, MLA_ROPE // 2, first32)
    for h in range(MLA_HEADS):
        sl = slice(h * LANES, (h + 1) * LANES)
        qm_ref[:, sl] = (rope(q[:, sl], cos_m, sin_m, MLA_ROPE // 2, first32) * mla_scale).astype(BF16)
        km_ref[:, sl] = (kv[:, sl] + kr).astype(BF16)
    vm_ref[...] = (kv[:, MLA_HEADS * LANES:] + one_col).astype(BF16)

    dq = proj(C_DQ, C_DK)
    dk = proj(C_DK, C_DV)
    for h in range(DSA_HEADS):
        sl = slice(h * LANES, (h + 1) * LANES)
        dq_ref[:, sl] = (rope(dq[:, sl], cos64, sin64, DSA_HEAD_DIM // 2, first64) * LOG2E).astype(BF16)
        dk_ref[:, sl] = rope(dk[:, sl], cos64, sin64, DSA_HEAD_DIM // 2, first64).astype(BF16)
    dv_ref[...] = (proj(C_DV, C_IQ) + one_col).astype(BF16)

    iq = proj(C_IQ, C_IK)
    for g in range(IDX_HEADS * IDX_DIM // LANES):
        sl = slice(g * LANES, (g + 1) * LANES)
        iq_ref[:, sl] = rope(iq[:, sl], cos32, sin32, IDX_DIM // 2, first32).astype(BF16)
    ik_ref[...] = rope(proj(C_IK, C_IW), cos32, sin32, IDX_DIM // 2, first32).astype(BF16)
    iw_ref[...] = proj(C_IW, C_END)


def _pad_heads(w, nh, d):
    k = w.shape[0]
    return jnp.pad(w.reshape(k, nh, d), ((0, 0), (0, 0), (0, LANES - d))).reshape(k, nh * LANES)


def _prep_in_weights(w_in, w_uq, w_ukv):
    d = w_in.shape[0]
    o = np.cumsum([0, Q_LORA, KV_LORA, MLA_ROPE, 512, 512, 512, IDX_HEADS * IDX_DIM, IDX_DIM, IDX_HEADS])
    c_q, c_kv, k_r, dq, dk, dv, iq, ik, iw = [w_in[:, o[i]:o[i + 1]] for i in range(9)]
    gates = w_in[:, o[9]:]
    kr_tile = jnp.pad(k_r, ((0, 0), (MLA_NOPE, LANES - MLA_NOPE - MLA_ROPE)))
    dsa_scale = DSA_HEAD_DIM ** -0.5
    idx_scale = (IDX_HEADS * IDX_DIM) ** -0.5
    w_in_p = jnp.concatenate([
        c_q, c_kv, kr_tile,
        _pad_heads(dq * dsa_scale, DSA_HEADS, DSA_HEAD_DIM),
        _pad_heads(dk, DSA_HEADS, DSA_HEAD_DIM),
        _pad_heads(dv, DSA_HEADS, DSA_HEAD_DIM),
        iq, jnp.tile(ik, (1, LANES // IDX_DIM)),
        jnp.pad(iw * idx_scale, ((0, 0), (0, LANES - IDX_HEADS))),
    ], axis=1).astype(BF16)
    assert w_in_p.shape == (d, C_END)
    w_uq_p = _pad_heads(w_uq, MLA_HEADS, MLA_NOPE + MLA_ROPE).astype(BF16)
    ukv = w_ukv.reshape(KV_LORA, MLA_HEADS, MLA_NOPE + MLA_V)
    w_ukv_p = jnp.concatenate([
        _pad_heads(ukv[:, :, :MLA_NOPE].reshape(KV_LORA, -1), MLA_HEADS, MLA_NOPE),
        _pad_heads(ukv[:, :, MLA_NOPE:].reshape(KV_LORA, -1), MLA_HEADS, MLA_V),
    ], axis=1).astype(BF16)
    return w_in_p, w_uq_p, w_ukv_p, gates.astype(BF16)


def _projections(x2, cos, sin, w_in_p, w_uq_p, w_ukv_p, g_q, g_kv):
    T, D = x2.shape
    tm = 256
    wide = MLA_HEADS * LANES
    row = lambda w: pl.BlockSpec((tm, w), lambda i: (i, 0))
    shapes = [(wide, BF16)] * 6 + [(IDX_HEADS * IDX_DIM, BF16), (LANES, BF16), (LANES, F32)]
    return pl.pallas_call(
        functools.partial(_proj_kernel, mla_scale=(MLA_NOPE + MLA_ROPE) ** -0.5 * LOG2E),
        out_shape=tuple(jax.ShapeDtypeStruct((T, w), dt) for w, dt in shapes),
        grid=(T // tm,),
        in_specs=[row(D),
                  pl.BlockSpec((2, tm, LANES), lambda i: (0, i, 0)),
                  pl.BlockSpec((2, tm, LANES), lambda i: (0, i, 0)),
                  _const_spec(w_in_p.shape), _const_spec(w_uq_p.shape), _const_spec(w_ukv_p.shape),
                  _const_spec((1, Q_LORA)), _const_spec((1, KV_LORA))],
        out_specs=tuple(row(w) for w, _ in shapes),
        compiler_params=_cparams(("parallel",)),
        name="projections",
    )(x2, cos, sin, w_in_p, w_uq_p, w_ukv_p, g_q.reshape(1, -1), g_kv.reshape(1, -1))


SUBLANES = 8


def _sublane_tile_max(mv, s):
    parts = [mv] + [s[r * SUBLANES:(r + 1) * SUBLANES, :] for r in range(s.shape[0] // SUBLANES)]
    while len(parts) > 1:
        parts = [jnp.maximum(a, b) for a, b in zip(parts[::2], parts[1::2])] + ([parts[-1]] if len(parts) % 2 else [])
    return parts[0]


def _transpose_bf16(x):
    return x.astype(F32).T.astype(BF16)


def _online_softmax_step(s, v, m_ref, acc_ref):
    m_old = m_ref[...]
    mx = _sublane_tile_max(s[:SUBLANES, :], s[SUBLANES:, :])
    m_new = jnp.maximum(m_old, jnp.max(mx, axis=0, keepdims=True))
    p = jnp.exp2(s - m_new).astype(BF16)
    acc_ref[...] = jnp.exp2(m_old - m_new) * acc_ref[...] + jnp.dot(_transpose_bf16(v), p, preferred_element_type=F32)
    m_ref[...] = m_new


def _normalise_t(acc_t):
    return acc_t / acc_t[ONE_LANE:ONE_LANE + 1, :]


def _mla_attn_kernel(q_ref, k_ref, v_ref, o_ref, s_sc, qt_sc, mv_sc, m_sc, acc_sc, *, tq):
    qi = pl.program_id(2)
    hg = s_sc.shape[0]
    slabs = [slice(j * LANES, (j + 1) * LANES) for j in range(hg)]
    for j, sl in enumerate(slabs):
        qt_sc[j] = _transpose_bf16(q_ref[:, sl])
    mv_sc[...] = jnp.full(mv_sc.shape, NEG, F32)

    def scores(c, masked):
        off = pl.multiple_of(c * tq, tq)
        for j, sl in enumerate(slabs):
            s = jnp.dot(k_ref[pl.ds(off, tq), sl], qt_sc[j], preferred_element_type=F32)
            if masked:
                krow = lax.broadcasted_iota(jnp.int32, (tq, tq), 0)
                qcol = lax.broadcasted_iota(jnp.int32, (tq, tq), 1)
                s = jnp.where(krow <= qcol, s, NEG)
            s_sc[j, c] = s
            mv_sc[j] = _sublane_tile_max(mv_sc[j], s)

    def score_body(c, carry):
        scores(c, False)
        return carry

    lax.fori_loop(0, qi, score_body, 0)
    scores(qi, True)
    for j in range(hg):
        m_sc[j] = jnp.max(mv_sc[j], axis=0, keepdims=True)
    acc_sc[...] = jnp.zeros(acc_sc.shape, F32)

    def pv_body(c, carry):
        off = pl.multiple_of(c * tq, tq)
        for j, sl in enumerate(slabs):
            p = jnp.exp2(s_sc[j, c] - m_sc[j]).astype(BF16)
            acc_sc[j] += jnp.dot(_transpose_bf16(v_ref[pl.ds(off, tq), sl]), p, preferred_element_type=F32)
        return carry

    lax.fori_loop(0, qi + 1, pv_body, 0)
    for j, sl in enumerate(slabs):
        o_ref[:, sl] = _normalise_t(acc_sc[j]).T.astype(o_ref.dtype)


def _mla_attention(q, k, v):
    B, S, W = q.shape
    tq, hg = 512, 2
    blk_q = pl.BlockSpec((None, tq, hg * LANES), lambda b, h, i: (b, i, h))
    blk_kv = pl.BlockSpec((None, S, hg * LANES), lambda b, h, i: (b, 0, h))
    return pl.pallas_call(
        functools.partial(_mla_attn_kernel, tq=tq),
        out_shape=jax.ShapeDtypeStruct((B, S, W), BF16),
        grid=(B, W // (hg * LANES), S // tq),
        in_specs=[blk_q, blk_kv, blk_kv],
        out_specs=blk_q,
        scratch_shapes=[pltpu.VMEM((hg, S // tq, tq, tq), F32), pltpu.VMEM((hg, LANES, tq), BF16),
                        pltpu.VMEM((hg, SUBLANES, tq), F32), pltpu.VMEM((hg, 1, tq), F32),
                        pltpu.VMEM((hg, LANES, tq), F32)],
        compiler_params=_cparams(("parallel", "parallel", "arbitrary")),
        name="mla_attention",
    )(q, k, v)


def _dsa_kernel(dq_ref, dk_ref, dv_ref, iq_ref, ik_ref, iw_ref, o_ref,
                key_sc, bias_sc, wt_sc, qmt_sc, qt_sc, m_sc, acc_sc, p_sc, al_sc,
                *, tq, ck, n_sel):
    qi = pl.program_id(1)
    nch = (qi * tq + tq + ck - 1) // ck
    lane = lax.broadcasted_iota(jnp.int32, (1, LANES), 1)
    k_sel = float(n_sel)

    wt_sc[...] = iw_ref[...].T
    for h in range(IDX_HEADS):
        g, lo = divmod(h * IDX_DIM, LANES)
        qg = iq_ref[:, g * LANES:(g + 1) * LANES].astype(F32)
        in_head = (lane >> 5) == (lo >> 5)
        qmt_sc[:, h * tq:(h + 1) * tq] = jnp.where(in_head, qg, 0.0).T.astype(BF16)

    kw = 2 * LANES
    qpos_t = qi * tq + lax.broadcasted_iota(jnp.int32, (kw, tq), 1)
    krow = lax.broadcasted_iota(jnp.int32, (kw, tq), 0)

    def score_body(c, carry):
        for part in range(ck // kw):
            ks = ik_ref[pl.ds(pl.multiple_of(c * ck + part * kw, kw), kw), :]
            d = jnp.dot(ks, qmt_sc[...], preferred_element_type=F32)
            score = jnp.zeros((kw, tq), F32)
            for h in range(IDX_HEADS):
                score = score + wt_sc[h:h + 1, :] * jnp.maximum(d[:, h * tq:(h + 1) * tq], 0.0)
            bits = lax.bitcast_convert_type(score, jnp.int32)
            key = jnp.where(bits >= 0, bits, bits ^ np.int32(0x7FFFFFFF))
            kpos = c * ck + part * kw + krow
            key_sc[c, part * kw:(part + 1) * kw, :] = jnp.where(kpos <= qpos_t, key, INT_MIN)
        return carry

    lax.fori_loop(0, nch, score_body, 0)

    sub = 8

    n_acc = 4

    def count_ge(thr):
        thr_t = jnp.broadcast_to(thr, (sub, tq))

        def body(c, accs):
            accs = list(accs)
            for r in range(ck // sub):
                ind = jnp.where(key_sc[c, r * sub:(r + 1) * sub, :] >= thr_t, 1.0, 0.0)
                accs[r % n_acc] = accs[r % n_acc] + ind
            return tuple(accs)

        accs = lax.fori_loop(0, nch, body, (jnp.zeros((sub, tq), F32),) * n_acc)
        return jnp.sum(sum(accs[1:], accs[0]), axis=0, keepdims=True)

    n_nonneg = count_ge(jnp.zeros((1, tq), jnp.int32))
    n_all = (nch * ck).astype(F32)
    tau0 = jnp.where(n_nonneg >= k_sel, np.int32(0), INT_MIN)
    n_ge0 = jnp.where(n_nonneg >= k_sel, n_nonneg, n_all)

    def tau_step(i, carry):
        tau, n_ge = carry
        cand = tau | (jnp.int32(1) << (30 - i))
        cnt = count_ge(cand)
        take = cnt >= k_sel
        return jnp.where(take, cand, tau), jnp.where(take, cnt, n_ge)

    tau, n_ge = lax.fori_loop(0, 31, tau_step, (tau0, n_ge0))

    excess = n_ge - k_sel
    qpos_c = qi * tq + lax.broadcasted_iota(jnp.int32, (ck, tq), 1)
    krow_c = lax.broadcasted_iota(jnp.int32, (ck, tq), 0)
    tri = jnp.where(lax.broadcasted_iota(jnp.int32, (ck, ck), 0) <= lax.broadcasted_iota(jnp.int32, (ck, ck), 1),
                    1.0, 0.0).astype(BF16)

    def bias_body(i, ties_after):
        c = nch - 1 - i
        blk = key_sc[c]
        is_tie = blk == tau
        suffix = jnp.dot(tri, jnp.where(is_tie, 1.0, 0.0).astype(BF16), preferred_element_type=F32)
        tie = jnp.where(is_tie, jnp.where(ties_after + suffix > excess, 0.0, NEG), NEG)
        sel = jnp.where(blk > tau, 0.0, tie)
        bias_sc[c] = jnp.where(c * ck + krow_c <= qpos_c, sel, NEG)
        return ties_after + suffix[0:1, :]

    lax.fori_loop(0, nch, bias_body, jnp.zeros((1, tq), F32))

    for h in range(DSA_HEADS):
        qt_sc[h] = _transpose_bf16(dq_ref[:, h * LANES:(h + 1) * LANES])
    m_sc[...] = jnp.full(m_sc.shape, NEG, F32)
    acc_sc[...] = jnp.zeros(acc_sc.shape, F32)

    def attn_body(c, carry):
        off = pl.multiple_of(c * ck, ck)
        bias = bias_sc[c]
        for h in range(DSA_HEADS):
            sl = slice(h * LANES, (h + 1) * LANES)
            s = jnp.dot(dk_ref[pl.ds(off, ck), sl], qt_sc[h], preferred_element_type=F32) + bias
            m_old = m_sc[h]
            mx = _sublane_tile_max(s[:SUBLANES, :], s[SUBLANES:, :])
            m_new = jnp.maximum(m_old, jnp.max(mx, axis=0, keepdims=True))
            p_sc[h] = jnp.exp2(s - m_new).astype(BF16)
            al_sc[h] = jnp.exp2(m_old - m_new)
            m_sc[h] = m_new
        for h in range(DSA_HEADS):
            sl = slice(h * LANES, (h + 1) * LANES)
            pv = jnp.dot(_transpose_bf16(dv_ref[pl.ds(off, ck), sl]), p_sc[h], preferred_element_type=F32)
            acc_sc[h] = al_sc[h] * acc_sc[h] + pv
        return carry

    lax.fori_loop(0, nch, attn_body, 0)
    for h in range(DSA_HEADS):
        o_ref[:, h * LANES:(h + 1) * LANES] = _normalise_t(acc_sc[h]).T.astype(o_ref.dtype)


def _sparse_attention(dq, dk, dv, iq, ik, iw):
    B, S, W = dq.shape
    tq, ck = 256, 512
    n_sel = min(TOPK_MAX, S // 4)
    assert S & (S - 1) == 0 and S % ck == 0
    blk_q = lambda w: pl.BlockSpec((None, tq, w), lambda b, i: (b, i, 0))
    blk_k = lambda w: pl.BlockSpec((None, S, w), lambda b, i: (b, 0, 0), pipeline_mode=pl.Buffered(1))
    return pl.pallas_call(
        functools.partial(_dsa_kernel, tq=tq, ck=ck, n_sel=n_sel),
        out_shape=jax.ShapeDtypeStruct((B, S, W), BF16),
        grid=(B, S // tq),
        in_specs=[blk_q(W), blk_k(W), blk_k(W), blk_q(iq.shape[-1]), blk_k(LANES), blk_q(LANES)],
        out_specs=blk_q(W),
        scratch_shapes=[
            pltpu.VMEM((S // ck, ck, tq), jnp.int32),
            pltpu.VMEM((S // ck, ck, tq), F32),
            pltpu.VMEM((LANES, tq), F32),
            pltpu.VMEM((LANES, IDX_HEADS * tq), BF16),
            pltpu.VMEM((DSA_HEADS, LANES, tq), BF16),
            pltpu.VMEM((DSA_HEADS, 1, tq), F32),
            pltpu.VMEM((DSA_HEADS, LANES, tq), F32),
            pltpu.VMEM((DSA_HEADS, ck, tq), BF16),
            pltpu.VMEM((DSA_HEADS, 1, tq), F32)],
        compiler_params=_cparams(("parallel", "arbitrary")),
        name="sparse_attention",
    )(dq, dk, dv, iq, ik, iw)


def _merge_kernel(x_ref, oa_ref, ob_ref, woa_ref, wob_ref, wg_ref, wout_ref, g_ref, b_ref, y_ref, *, alpha):
    x = x_ref[...]
    xb = x.astype(BF16)
    d = x.shape[-1]
    ya = jnp.dot(oa_ref[...], woa_ref[...], preferred_element_type=F32)
    yb = jnp.dot(ob_ref[...], wob_ref[...], preferred_element_type=F32)
    ga = jax.nn.sigmoid(jnp.dot(xb, wg_ref[:, :d], preferred_element_type=F32))
    gb = jax.nn.sigmoid(jnp.dot(xb, wg_ref[:, d:], preferred_element_type=F32))
    merged = ga * ya + gb * yb
    h = jnp.dot(merged.astype(BF16), wout_ref[...], preferred_element_type=F32)
    y_ref[...] = _layer_norm(alpha * x + h, g_ref[...], b_ref[...])


def _pad_head_rows(w, nh, d):
    n = w.shape[1]
    return jnp.pad(w.reshape(nh, d, n), ((0, 0), (0, LANES - d), (0, 0))).reshape(nh * LANES, n)


def _merge(x2, oa, ob, woa, wob, wg, wout, g, b, alpha):
    T, D = x2.shape
    tm = 256
    row = lambda w: pl.BlockSpec((tm, w), lambda i: (i, 0))
    return pl.pallas_call(
        functools.partial(_merge_kernel, alpha=alpha),
        out_shape=jax.ShapeDtypeStruct((T, D), F32),
        grid=(T // tm,),
        in_specs=[row(D), row(oa.shape[1]), row(ob.shape[1]),
                  _const_spec(woa.shape), _const_spec(wob.shape), _const_spec(wg.shape), _const_spec(wout.shape),
                  _const_spec((1, D)), _const_spec((1, D))],
        out_specs=row(D),
        compiler_params=_cparams(("parallel",)),
        name="merge_outproj_ln",
    )(x2, oa, ob, woa, wob, wg, wout, g.reshape(1, D), b.reshape(1, D))


def _ffn_kernel(x_ref, w1_ref, w3_ref, w2_ref, g_ref, b_ref, y_ref, xb_sc, acc_sc, *, alpha):
    f = pl.program_id(1)

    @pl.when(f == 0)
    def _():
        xb_sc[...] = x_ref[...].astype(BF16)
        acc_sc[...] = jnp.zeros(acc_sc.shape, F32)

    xb = xb_sc[...]
    a = jax.nn.silu(jnp.dot(xb, w1_ref[...], preferred_element_type=F32))
    a = a * jnp.dot(xb, w3_ref[...], preferred_element_type=F32)
    acc_sc[...] += jnp.dot(a.astype(BF16), w2_ref[...], preferred_element_type=F32)

    @pl.when(f == pl.num_programs(1) - 1)
    def _():
        y_ref[...] = _layer_norm(alpha * x_ref[...] + acc_sc[...], g_ref[...], b_ref[...])


def _dense_ffn(x2, w1, w3, w2, g, b, alpha):
    T, D = x2.shape
    FF = w1.shape[1]
    tm, tf = 1024, 256
    assert FF % tf == 0
    return pl.pallas_call(
        functools.partial(_ffn_kernel, alpha=alpha),
        out_shape=jax.ShapeDtypeStruct((T, D), F32),
        grid=(T // tm, FF // tf),
        in_specs=[pl.BlockSpec((tm, D), lambda i, f: (i, 0)),
                  pl.BlockSpec((D, tf), lambda i, f: (0, f)),
                  pl.BlockSpec((D, tf), lambda i, f: (0, f)),
                  pl.BlockSpec((tf, D), lambda i, f: (f, 0)),
                  _const_spec((1, D)), _const_spec((1, D))],
        out_specs=pl.BlockSpec((tm, D), lambda i, f: (i, 0)),
        scratch_shapes=[pltpu.VMEM((tm, D), BF16), pltpu.VMEM((tm, D), F32)],
        compiler_params=_cparams(("parallel", "arbitrary")),
        name="dense_ffn_ln",
    )(x2, w1.astype(BF16), w3.astype(BF16), w2.astype(BF16), g.reshape(1, D), b.reshape(1, D))


def _router_kernel(x_ref, wr_ref, route_ref):
    logits = jnp.dot(x_ref[...], wr_ref[...], preferred_element_type=F32, precision=lax.Precision.HIGHEST)
    lane = lax.broadcasted_iota(jnp.int32, logits.shape, 1).astype(F32)
    logits = jnp.where(lane < N_EXPERTS, logits, -jnp.inf)
    v1 = jnp.max(logits, axis=1, keepdims=True)
    i1 = jnp.min(jnp.where(logits == v1, lane, float(LANES)), axis=1, keepdims=True)
    rest = jnp.where(lane == i1, -jnp.inf, logits)
    v2 = jnp.max(rest, axis=1, keepdims=True)
    i2 = jnp.min(jnp.where(rest == v2, lane, float(LANES)), axis=1, keepdims=True)
    e2 = jnp.exp(v2 - v1)
    p1 = 1.0 / (1.0 + e2)
    route_ref[...] = jnp.where(lane == 0.0, i1, jnp.where(lane == 1.0, i2, jnp.where(lane == 2.0, p1, e2 * p1)))


def _router(x2, w_router):
    T, D = x2.shape
    tm = 512
    wr = jnp.pad(w_router, ((0, 0), (0, LANES - N_EXPERTS)))
    return pl.pallas_call(
        _router_kernel,
        out_shape=jax.ShapeDtypeStruct((T, LANES), F32),
        grid=(T // tm,),
        in_specs=[pl.BlockSpec((tm, D), lambda i: (i, 0)), _const_spec(wr.shape)],
        out_specs=pl.BlockSpec((tm, LANES), lambda i: (i, 0)),
        compiler_params=_cparams(("parallel",)),
        name="router",
    )(x2, wr)


def _row_gather(idx_ref, base, n_rows, src_hbm, dst_ref, sem):
    def issue(r, c):
        row = idx_ref[base + r]
        pltpu.make_async_copy(src_hbm.at[pl.ds(row, 1), :], dst_ref.at[pl.ds(r, 1), :], sem).start()
        return c

    lax.fori_loop(0, n_rows, issue, 0, unroll=8)
    pltpu.make_async_copy(src_hbm.at[pl.ds(0, n_rows), :], dst_ref, sem).wait()


def _dispatch_kernel(idx_ref, src_hbm, out_ref, sem, *, rows):
    _row_gather(idx_ref, pl.program_id(0) * rows, rows, src_hbm, out_ref, sem)


def _dispatch(tok_of_slot, x2, rows):
    n, D = tok_of_slot.shape[0], x2.shape[1]
    return pl.pallas_call(
        functools.partial(_dispatch_kernel, rows=rows),
        out_shape=jax.ShapeDtypeStruct((n, D), x2.dtype),
        grid_spec=pltpu.PrefetchScalarGridSpec(
            num_scalar_prefetch=1, grid=(n // rows,),
            in_specs=[pl.BlockSpec(memory_space=pl.ANY)],
            out_specs=pl.BlockSpec((rows, D), lambda t, idx: (t, 0)),
            scratch_shapes=[pltpu.SemaphoreType.DMA(())]),
        compiler_params=_cparams(("arbitrary",)),
        name="moe_dispatch",
    )(tok_of_slot, x2)


def _experts_kernel(te_ref, nu_ref, xs_ref, w1_ref, w3_ref, w2_ref, ys_ref, *, tf):
    t = pl.program_id(0)

    @pl.when(t < nu_ref[0])
    def _():
        xb = xs_ref[...].astype(BF16)
        acc = jnp.zeros(ys_ref.shape, F32)
        for f in range(w1_ref.shape[1] // tf):
            cols = slice(f * tf, (f + 1) * tf)
            a = jax.nn.silu(jnp.dot(xb, w1_ref[:, cols], preferred_element_type=F32))
            a = a * jnp.dot(xb, w3_ref[:, cols], preferred_element_type=F32)
            acc = acc + jnp.dot(a.astype(BF16), w2_ref[cols, :], preferred_element_type=F32)
        ys_ref[...] = acc

    @pl.when(t >= nu_ref[0])
    def _():
        ys_ref[...] = jnp.zeros(ys_ref.shape, F32)


def _experts(tile_expert, n_used, xs, w1, w3, w2, rows):
    P, D = xs.shape
    E, _, FF = w1.shape
    resident = lambda shape: pl.BlockSpec((None,) + shape, lambda t, te, nu: (te[t], 0, 0),
                                          pipeline_mode=pl.Buffered(1))
    return pl.pallas_call(
        functools.partial(_experts_kernel, tf=512),
        out_shape=jax.ShapeDtypeStruct((P, D), F32),
        grid_spec=pltpu.PrefetchScalarGridSpec(
            num_scalar_prefetch=2, grid=(P // rows,),
            in_specs=[pl.BlockSpec((rows, D), lambda t, te, nu: (t, 0)),
                      resident((D, FF)), resident((D, FF)), resident((FF, D))],
            out_specs=pl.BlockSpec((rows, D), lambda t, te, nu: (t, 0))),
        compiler_params=_cparams(("arbitrary",)),
        name="moe_experts",
    )(tile_expert, n_used, xs, w1, w3, w2)


def _combine_kernel(slot_ref, x_ref, route_ref, g_ref, b_ref, ys_hbm, y_ref, buf, sem, *, tm, alpha):
    _row_gather(slot_ref, pl.program_id(0) * 2 * tm, 2 * tm, ys_hbm, buf, sem)
    p1, p2 = route_ref[:, 2:3], route_ref[:, 3:4]
    f = p1 * buf[:tm, :] + p2 * buf[tm:, :]
    y_ref[...] = _layer_norm(alpha * x_ref[...] + f, g_ref[...], b_ref[...])


def _combine(slots, ys, x2, route, g, b, alpha, tm):
    T, D = x2.shape
    return pl.pallas_call(
        functools.partial(_combine_kernel, tm=tm, alpha=alpha),
        out_shape=jax.ShapeDtypeStruct((T, D), F32),
        grid_spec=pltpu.PrefetchScalarGridSpec(
            num_scalar_prefetch=1, grid=(T // tm,),
            in_specs=[pl.BlockSpec((tm, D), lambda i, s: (i, 0)),
                      pl.BlockSpec((tm, LANES), lambda i, s: (i, 0)),
                      pl.BlockSpec((1, D), lambda i, s: (0, 0)), pl.BlockSpec((1, D), lambda i, s: (0, 0)),
                      pl.BlockSpec(memory_space=pl.ANY)],
            out_specs=pl.BlockSpec((tm, D), lambda i, s: (i, 0)),
            scratch_shapes=[pltpu.VMEM((2 * tm, D), F32), pltpu.SemaphoreType.DMA(())]),
        compiler_params=_cparams(("arbitrary",)),
        name="moe_combine_ln",
    )(slots, x2, route, g.reshape(1, D), b.reshape(1, D), ys)


def _moe_ffn(x2, w_router, w1, w3, w2, g, b, alpha):
    T, D = x2.shape
    E = w1.shape[0]
    rows, tm = 256, 512
    gather_rows = 4 * rows
    route = _router(x2, w_router)

    e_flat = jnp.concatenate([route[:, 0], route[:, 1]]).astype(jnp.int32)
    onehot = (e_flat[:, None] == jnp.arange(E, dtype=jnp.int32)[None, :]).astype(jnp.int32)
    csum = jnp.cumsum(onehot, axis=0)
    counts = csum[-1]
    padded = (counts + rows - 1) // rows * rows
    ends = jnp.cumsum(padded)
    slot = jnp.sum(onehot * (csum - 1 + (ends - padded)[None, :]), axis=1)
    n_slots = 2 * T + E * rows
    tok_of_slot = jnp.zeros((n_slots,), jnp.int32).at[slot].set(jnp.arange(2 * T, dtype=jnp.int32) % T)
    tile_start = jnp.arange(n_slots // rows, dtype=jnp.int32) * rows
    tile_expert = jnp.minimum(jnp.sum((tile_start[:, None] >= ends[None, :]).astype(jnp.int32), axis=1), E - 1)
    n_used = (ends[-1:] // rows).astype(jnp.int32)

    assert n_slots % gather_rows == 0
    xs = _dispatch(tok_of_slot, x2, gather_rows)
    ys = _experts(tile_expert, n_used, xs, w1.astype(BF16), w3.astype(BF16), w2.astype(BF16), rows)
    slots = jnp.concatenate([slot[:T].reshape(T // tm, tm), slot[T:].reshape(T // tm, tm)], axis=1).reshape(-1)
    return _combine(slots, ys, x2, route, g, b, alpha, tm)


def kernel(x, positions, w_in, mla_q_norm, w_uq, mla_kv_norm, w_ukv, w_o_mla, w_o_dsa, w_out,
           ln1_g, ln1_b, ln2_g, ln2_b, dense_w1, dense_w3, dense_w2,
           moe_router, moe_w1, moe_w3, moe_w2):
    B, S, D = x.shape
    depth = w_in.shape[0]
    alpha = (2 * depth) ** 0.25
    T = B * S
    cos, sin = _rope_tables(positions)
    x2 = x.reshape(T, D)
    for l in range(depth):
        w_in_p, w_uq_p, w_ukv_p, w_gates = _prep_in_weights(w_in[l], w_uq[l], w_ukv[l])
        qm, km, vm, dq, dk, dv, iq, ik, iw = _projections(
            x2, cos, sin, w_in_p, w_uq_p, w_ukv_p, mla_q_norm[l], mla_kv_norm[l])
        b3 = lambda a: a.reshape(B, S, a.shape[-1])
        o_a = _mla_attention(b3(qm), b3(km), b3(vm)).reshape(T, -1)
        o_b = _sparse_attention(b3(dq), b3(dk), b3(dv), b3(iq), b3(ik), b3(iw)).reshape(T, -1)
        x2 = _merge(x2, o_a, o_b,
                    _pad_head_rows(w_o_mla[l], MLA_HEADS, MLA_V).astype(BF16),
                    _pad_head_rows(w_o_dsa[l], DSA_HEADS, DSA_HEAD_DIM).astype(BF16),
                    w_gates, w_out[l].astype(BF16), ln1_g[l], ln1_b[l], alpha)
        if l % 2 == 0:
            x2 = _dense_ffn(x2, dense_w1[l // 2], dense_w3[l // 2], dense_w2[l // 2], ln2_g[l], ln2_b[l], alpha)
        else:
            x2 = _moe_ffn(x2, moe_router[l // 2], moe_w1[l // 2], moe_w3[l // 2], moe_w2[l // 2],
                          ln2_g[l], ln2_b[l], alpha)
    return x2.reshape(B, S, D)
```

```python
import functools

import numpy as np
import jax
import jax.numpy as jnp
from jax import lax
from jax.experimental import pallas as pl
from jax.experimental.pallas import tpu as pltpu

F32 = jnp.float32
BF16 = jnp.bfloat16
LANES = 128
VMEM_LIMIT = 56 * 1024 * 1024

MLA_HEADS = 8
MLA_NOPE = 64
MLA_ROPE = 32
MLA_V = 64
Q_LORA = 384
KV_LORA = 256
DSA_HEADS = 8
DSA_HEAD_DIM = 64
IDX_HEADS = 8
IDX_DIM = 32
TOPK_MAX = 256
N_EXPERTS = 8
ROPE_THETA = 10000.0
NORM_EPS = 1e-5
NEG = -1e30
LOG2E = 1.4426950408889634
F32_MIN_NORMAL = 2.0 ** -126
INT_MIN = np.int32(-2 ** 31)
ONE_LANE = 64
assert ONE_LANE >= MLA_V and ONE_LANE >= DSA_HEAD_DIM

C_CQ = 0
C_CKV = C_CQ + Q_LORA
C_KR = C_CKV + KV_LORA
C_DQ = C_KR + LANES
C_DK = C_DQ + DSA_HEADS * LANES
C_DV = C_DK + DSA_HEADS * LANES
C_IQ = C_DV + DSA_HEADS * LANES
C_IK = C_IQ + IDX_HEADS * IDX_DIM
C_IW = C_IK + LANES
C_END = C_IW + LANES


def _cparams(sem):
    return pltpu.CompilerParams(dimension_semantics=sem, vmem_limit_bytes=VMEM_LIMIT)


def _const_spec(shape):
    nd = len(shape)
    return pl.BlockSpec(shape, lambda *_: (0,) * nd)


def _layer_norm(z, g, b):
    mu = jnp.mean(z, axis=-1, keepdims=True)
    zc = z - mu
    var = jnp.mean(zc * zc, axis=-1, keepdims=True)
    return zc * lax.rsqrt(var + NORM_EPS) * g + b


def _rope_tables_kernel(pos_ref, inv_ref, sgn_ref, cos_ref, sin_ref):
    pos = pos_ref[...].astype(F32)
    for p in range(2):
        ang = pos * inv_ref[p:p + 1, :]
        cos_ref[p] = jnp.cos(ang)
        sin_ref[p] = jnp.sin(ang) * sgn_ref[p:p + 1, :]


def _rope_tables(positions):
    T = positions.size
    tm = 1024
    inv32 = jnp.power(ROPE_THETA, -jnp.arange(0, IDX_DIM, 2, dtype=F32) / IDX_DIM)
    inv64 = jnp.power(ROPE_THETA, -jnp.arange(0, DSA_HEAD_DIM, 2, dtype=F32) / DSA_HEAD_DIM)
    inv = jnp.stack([jnp.tile(inv32, LANES // 16),
                     jnp.concatenate([inv64, inv64, jnp.zeros((LANES - 64,), F32)])])
    lane = np.arange(LANES)
    sgn = jnp.asarray(np.stack([np.where(lane % 32 < 16, -1.0, 1.0),
                                np.where(lane < 32, -1.0, np.where(lane < 64, 1.0, 0.0))]), F32)
    out = jax.ShapeDtypeStruct((2, T, LANES), F32)
    return pl.pallas_call(
        _rope_tables_kernel,
        out_shape=(out, out),
        grid=(T // tm,),
        in_specs=[pl.BlockSpec((tm, 1), lambda i: (i, 0)), _const_spec((2, LANES)), _const_spec((2, LANES))],
        out_specs=(pl.BlockSpec((2, tm, LANES), lambda i: (0, i, 0)),) * 2,
        compiler_params=_cparams(("parallel",)),
        name="rope_tables",
    )(positions.reshape(T, 1), inv, sgn)


def _proj_kernel(x_ref, cos_ref, sin_ref, w_in_ref, w_uq_ref, w_ukv_ref, gq_ref, gkv_ref,
                 qm_ref, km_ref, vm_ref, dq_ref, dk_ref, dv_ref, iq_ref, ik_ref, iw_ref, *, mla_scale):
    xb = x_ref[...].astype(BF16)
    lane = lax.broadcasted_iota(jnp.int32, (1, LANES), 1)
    cos32, sin32 = cos_ref[0], sin_ref[0]
    cos64, sin64 = cos_ref[1], sin_ref[1]
    in_rope = (lane >> 5) == (MLA_NOPE >> 5)
    cos_m = jnp.where(in_rope, cos32, 1.0)
    sin_m = jnp.where(in_rope, sin32, 0.0)
    first32 = (lane & (IDX_DIM - 1)) < IDX_DIM // 2
    first64 = lane < DSA_HEAD_DIM // 2
    wide_lane = lax.broadcasted_iota(jnp.int32, (1, MLA_HEADS * LANES), 1)
    one_col = jnp.where((wide_lane & (LANES - 1)) == ONE_LANE, 1.0, 0.0)

    def proj(lo, hi):
        return jnp.dot(xb, w_in_ref[:, lo:hi], preferred_element_type=F32)

    def rope(xs, cos, sin, half, first):
        rot = jnp.where(first, pltpu.roll(xs, LANES - half, 1), pltpu.roll(xs, half, 1))
        return xs * cos + rot * sin

    def rms(c, g):
        ms = jnp.mean(c * c, axis=-1, keepdims=True)
        return c * lax.rsqrt(ms + NORM_EPS) * g

    q = jnp.dot(rms(proj(C_CQ, C_CKV), gq_ref[...]).astype(BF16), w_uq_ref[...], preferred_element_type=F32)
    kv = jnp.dot(rms(proj(C_CKV, C_KR), gkv_ref[...]).astype(BF16), w_ukv_ref[...], preferred_element_type=F32)
    kr = rope(proj(C_KR, C_DQ), cos_m, sin_m, MLA_ROPE // 2, first32)
    for h in range(MLA_HEADS):
        sl = slice(h * LANES, (h + 1) * LANES)
        qm_ref[:, sl] = (rope(q[:, sl], cos_m, sin_m, MLA_ROPE // 2, first32) * mla_scale).astype(BF16)
        km_ref[:, sl] = (kv[:, sl] + kr).astype(BF16)
    vm_ref[...] = (kv[:, MLA_HEADS * LANES:] + one_col).astype(BF16)

    dq = proj(C_DQ, C_DK)
    dk = proj(C_DK, C_DV)
    for h in range(DSA_HEADS):
        sl = slice(h * LANES, (h + 1) * LANES)
        dq_ref[:, sl] = (rope(dq[:, sl], cos64, sin64, DSA_HEAD_DIM // 2, first64) * LOG2E).astype(BF16)
        dk_ref[:, sl] = rope(dk[:, sl], cos64, sin64, DSA_HEAD_DIM // 2, first64).astype(BF16)
    dv_ref[...] = (proj(C_DV, C_IQ) + one_col).astype(BF16)

    iq = proj(C_IQ, C_IK)
    for g in range(IDX_HEADS * IDX_DIM // LANES):
        sl = slice(g * LANES, (g + 1) * LANES)
        iq_ref[:, sl] = rope(iq[:, sl], cos32, sin32, IDX_DIM // 2, first32).astype(BF16)
    ik_ref[...] = rope(proj(C_IK, C_IW), cos32, sin32, IDX_DIM // 2, first32).astype(BF16)
    iw_ref[...] = proj(C_IW, C_END)


def _pad_heads(w, nh, d):
    k = w.shape[0]
    return jnp.pad(w.reshape(k, nh, d), ((0, 0), (0, 0), (0, LANES - d))).reshape(k, nh * LANES)


def _prep_in_weights(w_in, w_uq, w_ukv):
    d = w_in.shape[0]
    o = np.cumsum([0, Q_LORA, KV_LORA, MLA_ROPE, 512, 512, 512, IDX_HEADS * IDX_DIM, IDX_DIM, IDX_HEADS])
    c_q, c_kv, k_r, dq, dk, dv, iq, ik, iw = [w_in[:, o[i]:o[i + 1]] for i in range(9)]
    gates = w_in[:, o[9]:]
    kr_tile = jnp.pad(k_r, ((0, 0), (MLA_NOPE, LANES - MLA_NOPE - MLA_ROPE)))
    dsa_scale = DSA_HEAD_DIM ** -0.5
    idx_scale = (IDX_HEADS * IDX_DIM) ** -0.5
    w_in_p = jnp.concatenate([
        c_q, c_kv, kr_tile,
        _pad_heads(dq * dsa_scale, DSA_HEADS, DSA_HEAD_DIM),
        _pad_heads(dk, DSA_HEADS, DSA_HEAD_DIM),
        _pad_heads(dv, DSA_HEADS, DSA_HEAD_DIM),
        iq, jnp.tile(ik, (1, LANES // IDX_DIM)),
        jnp.pad(iw * idx_scale, ((0, 0), (0, LANES - IDX_HEADS))),
    ], axis=1).astype(BF16)
    assert w_in_p.shape == (d, C_END)
    w_uq_p = _pad_heads(w_uq, MLA_HEADS, MLA_NOPE + MLA_ROPE).astype(BF16)
    ukv = w_ukv.reshape(KV_LORA, MLA_HEADS, MLA_NOPE + MLA_V)
    w_ukv_p = jnp.concatenate([
        _pad_heads(ukv[:, :, :MLA_NOPE].reshape(KV_LORA, -1), MLA_HEADS, MLA_NOPE),
        _pad_heads(ukv[:, :, MLA_NOPE:].reshape(KV_LORA, -1), MLA_HEADS, MLA_V),
    ], axis=1).astype(BF16)
    return w_in_p, w_uq_p, w_ukv_p, gates.astype(BF16)


def _projections(x2, cos, sin, w_in_p, w_uq_p, w_ukv_p, g_q, g_kv):
    T, D = x2.shape
    tm = 256
    wide = MLA_HEADS * LANES
    row = lambda w: pl.BlockSpec((tm, w), lambda i: (i, 0))
    shapes = [(wide, BF16)] * 6 + [(IDX_HEADS * IDX_DIM, BF16), (LANES, BF16), (LANES, F32)]
    return pl.pallas_call(
        functools.partial(_proj_kernel, mla_scale=(MLA_NOPE + MLA_ROPE) ** -0.5 * LOG2E),
        out_shape=tuple(jax.ShapeDtypeStruct((T, w), dt) for w, dt in shapes),
        grid=(T // tm,),
        in_specs=[row(D),
                  pl.BlockSpec((2, tm, LANES), lambda i: (0, i, 0)),
                  pl.BlockSpec((2, tm, LANES), lambda i: (0, i, 0)),
                  _const_spec(w_in_p.shape), _const_spec(w_uq_p.shape), _const_spec(w_ukv_p.shape),
                  _const_spec((1, Q_LORA)), _const_spec((1, KV_LORA))],
        out_specs=tuple(row(w) for w, _ in shapes),
        compiler_params=_cparams(("parallel",)),
        name="projections",
    )(x2, cos, sin, w_in_p, w_uq_p, w_ukv_p, g_q.reshape(1, -1), g_kv.reshape(1, -1))


SUBLANES = 8


def _sublane_tile_max(mv, s):
    parts = [mv] + [s[r * SUBLANES:(r + 1) * SUBLANES, :] for r in range(s.shape[0] // SUBLANES)]
    while len(parts) > 1:
        parts = [jnp.maximum(a, b) for a, b in zip(parts[::2], parts[1::2])] + ([parts[-1]] if len(parts) % 2 else [])
    return parts[0]


def _transpose_bf16(x):
    return x.astype(F32).T.astype(BF16)


def _online_softmax_step(s, v, m_ref, acc_ref):
    m_old = m_ref[...]
    mx = _sublane_tile_max(s[:SUBLANES, :], s[SUBLANES:, :])
    m_new = jnp.maximum(m_old, jnp.max(mx, axis=0, keepdims=True))
    p = jnp.exp2(s - m_new).astype(BF16)
    acc_ref[...] = jnp.exp2(m_old - m_new) * acc_ref[...] + jnp.dot(_transpose_bf16(v), p, preferred_element_type=F32)
    m_ref[...] = m_new


def _normalise_t(acc_t):
    return acc_t / acc_t[ONE_LANE:ONE_LANE + 1, :]


def _mla_attn_kernel(q_ref, k_ref, v_ref, o_ref, s_sc, qt_sc, mv_sc, m_sc, acc_sc, *, tq):
    qi = pl.program_id(2)
    hg = s_sc.shape[0]
    slabs = [slice(j * LANES, (j + 1) * LANES) for j in range(hg)]
    for j, sl in enumerate(slabs):
        qt_sc[j] = _transpose_bf16(q_ref[:, sl])
    mv_sc[...] = jnp.full(mv_sc.shape, NEG, F32)

    def scores(c, masked):
        off = pl.multiple_of(c * tq, tq)
        for j, sl in enumerate(slabs):
            s = jnp.dot(k_ref[pl.ds(off, tq), sl], qt_sc[j], preferred_element_type=F32)
            if masked:
                krow = lax.broadcasted_iota(jnp.int32, (tq, tq), 0)
                qcol = lax.broadcasted_iota(jnp.int32, (tq, tq), 1)
                s = jnp.where(krow <= qcol, s, NEG)
            s_sc[j, c] = s
            mv_sc[j] = _sublane_tile_max(mv_sc[j], s)

    def score_body(c, carry):
        scores(c, False)
        return carry

    lax.fori_loop(0, qi, score_body, 0)
    scores(qi, True)
    for j in range(hg):
        m_sc[j] = jnp.max(mv_sc[j], axis=0, keepdims=True)
    acc_sc[...] = jnp.zeros(acc_sc.shape, F32)

    def pv_body(c, carry):
        off = pl.multiple_of(c * tq, tq)
        for j, sl in enumerate(slabs):
            p = jnp.exp2(s_sc[j, c] - m_sc[j]).astype(BF16)
            acc_sc[j] += jnp.dot(_transpose_bf16(v_ref[pl.ds(off, tq), sl]), p, preferred_element_type=F32)
        return carry

    lax.fori_loop(0, qi + 1, pv_body, 0)
    for j, sl in enumerate(slabs):
        o_ref[:, sl] = _normalise_t(acc_sc[j]).T.astype(o_ref.dtype)


def _mla_attention(q, k, v):
    B, S, W = q.shape
    tq, hg = 512, 2
    blk_q = pl.BlockSpec((None, tq, hg * LANES), lambda b, h, i: (b, i, h))
    blk_kv = pl.BlockSpec((None, S, hg * LANES), lambda b, h, i: (b, 0, h))
    return pl.pallas_call(
        functools.partial(_mla_attn_kernel, tq=tq),
        out_shape=jax.ShapeDtypeStruct((B, S, W), BF16),
        grid=(B, W // (hg * LANES), S // tq),
        in_specs=[blk_q, blk_kv, blk_kv],
        out_specs=blk_q,
        scratch_shapes=[pltpu.VMEM((hg, S // tq, tq, tq), F32), pltpu.VMEM((hg, LANES, tq), BF16),
                        pltpu.VMEM((hg, SUBLANES, tq), F32), pltpu.VMEM((hg, 1, tq), F32),
                        pltpu.VMEM((hg, LANES, tq), F32)],
        compiler_params=_cparams(("parallel", "parallel", "arbitrary")),
        name="mla_attention",
    )(q, k, v)


def _dsa_kernel(dq_ref, dk_ref, dv_ref, iq_ref, ik_ref, iw_ref, o_ref,
                key_sc, hi_sc, bias_sc, wt_sc, qmt_sc, qt_sc, m_sc, acc_sc, p_sc, al_sc,
                *, tq, ck, n_sel):
    qi = pl.program_id(1)
    nch = (qi * tq + tq + ck - 1) // ck
    lane = lax.broadcasted_iota(jnp.int32, (1, LANES), 1)
    k_sel = float(n_sel)

    wt_sc[...] = iw_ref[...].T
    for h in range(IDX_HEADS):
        g, lo = divmod(h * IDX_DIM, LANES)
        qg = iq_ref[:, g * LANES:(g + 1) * LANES].astype(F32)
        in_head = (lane >> 5) == (lo >> 5)
        qmt_sc[:, h * tq:(h + 1) * tq] = jnp.where(in_head, qg, 0.0).T.astype(BF16)

    kw = 2 * LANES
    qpos_t = qi * tq + lax.broadcasted_iota(jnp.int32, (kw, tq), 1)
    krow = lax.broadcasted_iota(jnp.int32, (kw, tq), 0)

    def score_body(c, carry):
        for part in range(ck // kw):
            ks = ik_ref[pl.ds(pl.multiple_of(c * ck + part * kw, kw), kw), :]
            d = jnp.dot(ks, qmt_sc[...], preferred_element_type=F32)
            score = jnp.zeros((kw, tq), F32)
            for h in range(IDX_HEADS):
                score = score + wt_sc[h:h + 1, :] * jnp.maximum(d[:, h * tq:(h + 1) * tq], 0.0)
            score = jnp.where(jnp.abs(score) < F32_MIN_NORMAL, 0.0, score)
            bits = lax.bitcast_convert_type(score, jnp.int32)
            key = jnp.where(bits >= 0, bits, bits ^ np.int32(0x7FFFFFFF))
            causal = c * ck + part * kw + krow <= qpos_t
            key_sc[c, part * kw:(part + 1) * kw, :] = jnp.where(causal, key, INT_MIN)
            top = lax.bitcast_convert_type(bits & np.int32(-65536), F32)
            hi_sc[c, part * kw:(part + 1) * kw, :] = jnp.where(causal, top, jnp.nan).astype(BF16)
        return carry

    lax.fori_loop(0, nch, score_body, 0)

    sub = 8

    n_acc = 4

    def count_ge(thr):
        thr_t = jnp.broadcast_to(thr, (sub, tq))

        def body(c, accs):
            accs = list(accs)
            for r in range(ck // sub):
                ind = jnp.where(key_sc[c, r * sub:(r + 1) * sub, :] >= thr_t, 1.0, 0.0)
                accs[r % n_acc] = accs[r % n_acc] + ind
            return tuple(accs)

        accs = lax.fori_loop(0, nch, body, (jnp.zeros((sub, tq), F32),) * n_acc)
        return jnp.sum(sum(accs[1:], accs[0]), axis=0, keepdims=True)

    pack = 16

    def count_top_ge(top16):
        top16 = jnp.where(top16 > 0, jnp.maximum(top16, np.int32(0x0080)), top16)
        fbits = jnp.where(top16 >= 0, top16, top16 ^ np.int32(0x7FFF)) << 16
        thr_t = jnp.broadcast_to(lax.bitcast_convert_type(fbits, F32).astype(BF16), (pack, tq))
        one, zero = jnp.ones((pack, tq), BF16), jnp.zeros((pack, tq), BF16)

        def body(c, accs):
            accs = list(accs)
            for r in range(ck // pack):
                ind = jnp.where(hi_sc[c, r * pack:(r + 1) * pack, :] >= thr_t, one, zero)
                accs[r % n_acc] = accs[r % n_acc] + ind
            return tuple(accs)

        accs = lax.fori_loop(0, nch, body, (zero,) * n_acc)
        return jnp.sum(sum((a.astype(F32) for a in accs[1:]), accs[0].astype(F32)), axis=0, keepdims=True)

    n_all = (nch * ck).astype(F32)
    n_nonneg = count_top_ge(jnp.zeros((1, tq), jnp.int32))
    top0 = jnp.where(n_nonneg >= k_sel, np.int32(0), np.int32(-32768))
    n_ge0 = jnp.where(n_nonneg >= k_sel, n_nonneg, n_all)

    def search(count, first_bit, n_bits, start):
        def step(i, carry):
            tau, n_ge = carry
            cand = tau | (jnp.int32(1) << (first_bit - i))
            cnt = count(cand)
            take = cnt >= k_sel
            return jnp.where(take, cand, tau), jnp.where(take, cnt, n_ge)
        return lax.fori_loop(0, n_bits, step, start)

    top, n_ge = search(count_top_ge, 14, 15, (top0, n_ge0))
    tau, n_ge = search(count_ge, 15, 16, (top << 16, n_ge))

    excess = n_ge - k_sel
    qpos_c = qi * tq + lax.broadcasted_iota(jnp.int32, (ck, tq), 1)
    krow_c = lax.broadcasted_iota(jnp.int32, (ck, tq), 0)
    tri = jnp.where(lax.broadcasted_iota(jnp.int32, (ck, ck), 0) <= lax.broadcasted_iota(jnp.int32, (ck, ck), 1),
                    1.0, 0.0).astype(BF16)

    def bias_body(i, ties_after):
        c = nch - 1 - i
        blk = key_sc[c]
        is_tie = blk == tau
        suffix = jnp.dot(tri, jnp.where(is_tie, 1.0, 0.0).astype(BF16), preferred_element_type=F32)
        tie = jnp.where(is_tie, jnp.where(ties_after + suffix > excess, 0.0, NEG), NEG)
        sel = jnp.where(blk > tau, 0.0, tie)
        bias_sc[c] = jnp.where(c * ck + krow_c <= qpos_c, sel, NEG)
        return ties_after + suffix[0:1, :]

    lax.fori_loop(0, nch, bias_body, jnp.zeros((1, tq), F32))

    for h in range(DSA_HEADS):
        qt_sc[h] = _transpose_bf16(dq_ref[:, h * LANES:(h + 1) * LANES])
    m_sc[...] = jnp.full(m_sc.shape, NEG, F32)
    acc_sc[...] = jnp.zeros(acc_sc.shape, F32)

    def attn_body(c, carry):
        off = pl.multiple_of(c * ck, ck)
        bias = bias_sc[c]
        for h in range(DSA_HEADS):
            sl = slice(h * LANES, (h + 1) * LANES)
            s = jnp.dot(dk_ref[pl.ds(off, ck), sl], qt_sc[h], preferred_element_type=F32) + bias
            m_old = m_sc[h]
            mx = _sublane_tile_max(s[:SUBLANES, :], s[SUBLANES:, :])
            m_new = jnp.maximum(m_old, jnp.max(mx, axis=0, keepdims=True))
            p_sc[h] = jnp.exp2(s - m_new).astype(BF16)
            al_sc[h] = jnp.exp2(m_old - m_new)
            m_sc[h] = m_new
        for h in range(DSA_HEADS):
            sl = slice(h * LANES, (h + 1) * LANES)
            pv = jnp.dot(_transpose_bf16(dv_ref[pl.ds(off, ck), sl]), p_sc[h], preferred_element_type=F32)
            acc_sc[h] = al_sc[h] * acc_sc[h] + pv
        return carry

    lax.fori_loop(0, nch, attn_body, 0)
    for h in range(DSA_HEADS):
        o_ref[:, h * LANES:(h + 1) * LANES] = _normalise_t(acc_sc[h]).T.astype(o_ref.dtype)


def _sparse_attention(dq, dk, dv, iq, ik, iw):
    B, S, W = dq.shape
    tq, ck = 256, 512
    n_sel = min(TOPK_MAX, S // 4)
    assert S & (S - 1) == 0 and S % ck == 0
    blk_q = lambda w: pl.BlockSpec((None, tq, w), lambda b, i: (b, i, 0))
    blk_k = lambda w: pl.BlockSpec((None, S, w), lambda b, i: (b, 0, 0), pipeline_mode=pl.Buffered(1))
    return pl.pallas_call(
        functools.partial(_dsa_kernel, tq=tq, ck=ck, n_sel=n_sel),
        out_shape=jax.ShapeDtypeStruct((B, S, W), BF16),
        grid=(B, S // tq),
        in_specs=[blk_q(W), blk_k(W), blk_k(W), blk_q(iq.shape[-1]), blk_k(LANES), blk_q(LANES)],
        out_specs=blk_q(W),
        scratch_shapes=[
            pltpu.VMEM((S // ck, ck, tq), jnp.int32),
            pltpu.VMEM((S // ck, ck, tq), BF16),
            pltpu.VMEM((S // ck, ck, tq), F32),
            pltpu.VMEM((LANES, tq), F32),
            pltpu.VMEM((LANES, IDX_HEADS * tq), BF16),
            pltpu.VMEM((DSA_HEADS, LANES, tq), BF16),
            pltpu.VMEM((DSA_HEADS, 1, tq), F32),
            pltpu.VMEM((DSA_HEADS, LANES, tq), F32),
            pltpu.VMEM((DSA_HEADS, ck, tq), BF16),
            pltpu.VMEM((DSA_HEADS, 1, tq), F32)],
        compiler_params=_cparams(("parallel", "arbitrary")),
        name="sparse_attention",
    )(dq, dk, dv, iq, ik, iw)


def _merge_kernel(x_ref, oa_ref, ob_ref, woa_ref, wob_ref, wg_ref, wout_ref, g_ref, b_ref, y_ref, *, alpha):
    x = x_ref[...]
    xb = x.astype(BF16)
    d = x.shape[-1]
    ya = jnp.dot(oa_ref[...], woa_ref[...], preferred_element_type=F32)
    yb = jnp.dot(ob_ref[...], wob_ref[...], preferred_element_type=F32)
    ga = jax.nn.sigmoid(jnp.dot(xb, wg_ref[:, :d], preferred_element_type=F32))
    gb = jax.nn.sigmoid(jnp.dot(xb, wg_ref[:, d:], preferred_element_type=F32))
    merged = ga * ya + gb * yb
    h = jnp.dot(merged.astype(BF16), wout_ref[...], preferred_element_type=F32)
    y_ref[...] = _layer_norm(alpha * x + h, g_ref[...], b_ref[...])


def _pad_head_rows(w, nh, d):
    n = w.shape[1]
    return jnp.pad(w.reshape(nh, d, n), ((0, 0), (0, LANES - d), (0, 0))).reshape(nh * LANES, n)


def _merge(x2, oa, ob, woa, wob, wg, wout, g, b, alpha):
    T, D = x2.shape
    tm = 256
    row = lambda w: pl.BlockSpec((tm, w), lambda i: (i, 0))
    return pl.pallas_call(
        functools.partial(_merge_kernel, alpha=alpha),
        out_shape=jax.ShapeDtypeStruct((T, D), F32),
        grid=(T // tm,),
        in_specs=[row(D), row(oa.shape[1]), row(ob.shape[1]),
                  _const_spec(woa.shape), _const_spec(wob.shape), _const_spec(wg.shape), _const_spec(wout.shape),
                  _const_spec((1, D)), _const_spec((1, D))],
        out_specs=row(D),
        compiler_params=_cparams(("parallel",)),
        name="merge_outproj_ln",
    )(x2, oa, ob, woa, wob, wg, wout, g.reshape(1, D), b.reshape(1, D))


def _ffn_kernel(x_ref, w1_ref, w3_ref, w2_ref, g_ref, b_ref, y_ref, xb_sc, acc_sc, *, alpha):
    f = pl.program_id(1)

    @pl.when(f == 0)
    def _():
        xb_sc[...] = x_ref[...].astype(BF16)
        acc_sc[...] = jnp.zeros(acc_sc.shape, F32)

    xb = xb_sc[...]
    a = jax.nn.silu(jnp.dot(xb, w1_ref[...], preferred_element_type=F32))
    a = a * jnp.dot(xb, w3_ref[...], preferred_element_type=F32)
    acc_sc[...] += jnp.dot(a.astype(BF16), w2_ref[...], preferred_element_type=F32)

    @pl.when(f == pl.num_programs(1) - 1)
    def _():
        y_ref[...] = _layer_norm(alpha * x_ref[...] + acc_sc[...], g_ref[...], b_ref[...])


def _dense_ffn(x2, w1, w3, w2, g, b, alpha):
    T, D = x2.shape
    FF = w1.shape[1]
    tm, tf = 1024, 256
    assert FF % tf == 0
    return pl.pallas_call(
        functools.partial(_ffn_kernel, alpha=alpha),
        out_shape=jax.ShapeDtypeStruct((T, D), F32),
        grid=(T // tm, FF // tf),
        in_specs=[pl.BlockSpec((tm, D), lambda i, f: (i, 0)),
                  pl.BlockSpec((D, tf), lambda i, f: (0, f)),
                  pl.BlockSpec((D, tf), lambda i, f: (0, f)),
                  pl.BlockSpec((tf, D), lambda i, f: (f, 0)),
                  _const_spec((1, D)), _const_spec((1, D))],
        out_specs=pl.BlockSpec((tm, D), lambda i, f: (i, 0)),
        scratch_shapes=[pltpu.VMEM((tm, D), BF16), pltpu.VMEM((tm, D), F32)],
        compiler_params=_cparams(("parallel", "arbitrary")),
        name="dense_ffn_ln",
    )(x2, w1.astype(BF16), w3.astype(BF16), w2.astype(BF16), g.reshape(1, D), b.reshape(1, D))


def _router_kernel(x_ref, wr_ref, route_ref):
    logits = jnp.dot(x_ref[...], wr_ref[...], preferred_element_type=F32, precision=lax.Precision.HIGHEST)
    lane = lax.broadcasted_iota(jnp.int32, logits.shape, 1).astype(F32)
    logits = jnp.where(lane < N_EXPERTS, logits, -jnp.inf)
    v1 = jnp.max(logits, axis=1, keepdims=True)
    i1 = jnp.min(jnp.where(logits == v1, lane, float(LANES)), axis=1, keepdims=True)
    rest = jnp.where(lane == i1, -jnp.inf, logits)
    v2 = jnp.max(rest, axis=1, keepdims=True)
    i2 = jnp.min(jnp.where(rest == v2, lane, float(LANES)), axis=1, keepdims=True)
    e2 = jnp.exp(v2 - v1)
    p1 = 1.0 / (1.0 + e2)
    route_ref[...] = jnp.where(lane == 0.0, i1, jnp.where(lane == 1.0, i2, jnp.where(lane == 2.0, p1, e2 * p1)))


def _router(x2, w_router):
    T, D = x2.shape
    tm = 512
    wr = jnp.pad(w_router, ((0, 0), (0, LANES - N_EXPERTS)))
    return pl.pallas_call(
        _router_kernel,
        out_shape=jax.ShapeDtypeStruct((T, LANES), F32),
        grid=(T // tm,),
        in_specs=[pl.BlockSpec((tm, D), lambda i: (i, 0)), _const_spec(wr.shape)],
        out_specs=pl.BlockSpec((tm, LANES), lambda i: (i, 0)),
        compiler_params=_cparams(("parallel",)),
        name="router",
    )(x2, wr)


def _row_gather(idx_ref, base, n_rows, src_hbm, dst_ref, sem):
    def issue(r, c):
        row = idx_ref[base + r]
        pltpu.make_async_copy(src_hbm.at[pl.ds(row, 1), :], dst_ref.at[pl.ds(r, 1), :], sem).start()
        return c

    lax.fori_loop(0, n_rows, issue, 0, unroll=8)
    pltpu.make_async_copy(src_hbm.at[pl.ds(0, n_rows), :], dst_ref, sem).wait()


def _dispatch_kernel(idx_ref, src_hbm, out_ref, sem, *, rows):
    _row_gather(idx_ref, pl.program_id(0) * rows, rows, src_hbm, out_ref, sem)


def _dispatch(tok_of_slot, x2, rows):
    n, D = tok_of_slot.shape[0], x2.shape[1]
    return pl.pallas_call(
        functools.partial(_dispatch_kernel, rows=rows),
        out_shape=jax.ShapeDtypeStruct((n, D), x2.dtype),
        grid_spec=pltpu.PrefetchScalarGridSpec(
            num_scalar_prefetch=1, grid=(n // rows,),
            in_specs=[pl.BlockSpec(memory_space=pl.ANY)],
            out_specs=pl.BlockSpec((rows, D), lambda t, idx: (t, 0)),
            scratch_shapes=[pltpu.SemaphoreType.DMA(())]),
        compiler_params=_cparams(("arbitrary",)),
        name="moe_dispatch",
    )(tok_of_slot, x2)


def _experts_kernel(te_ref, nu_ref, xs_ref, w1_ref, w3_ref, w2_ref, ys_ref, *, tf):
    t = pl.program_id(0)

    @pl.when(t < nu_ref[0])
    def _():
        xb = xs_ref[...].astype(BF16)
        acc = jnp.zeros(ys_ref.shape, F32)
        for f in range(w1_ref.shape[1] // tf):
            cols = slice(f * tf, (f + 1) * tf)
            a = jax.nn.silu(jnp.dot(xb, w1_ref[:, cols], preferred_element_type=F32))
            a = a * jnp.dot(xb, w3_ref[:, cols], preferred_element_type=F32)
            acc = acc + jnp.dot(a.astype(BF16), w2_ref[cols, :], preferred_element_type=F32)
        ys_ref[...] = acc

    @pl.when(t >= nu_ref[0])
    def _():
        ys_ref[...] = jnp.zeros(ys_ref.shape, F32)


def _experts(tile_expert, n_used, xs, w1, w3, w2, rows):
    P, D = xs.shape
    E, _, FF = w1.shape
    resident = lambda shape: pl.BlockSpec((None,) + shape, lambda t, te, nu: (te[t], 0, 0),
                                          pipeline_mode=pl.Buffered(1))
    return pl.pallas_call(
        functools.partial(_experts_kernel, tf=512),
        out_shape=jax.ShapeDtypeStruct((P, D), F32),
        grid_spec=pltpu.PrefetchScalarGridSpec(
            num_scalar_prefetch=2, grid=(P // rows,),
            in_specs=[pl.BlockSpec((rows, D), lambda t, te, nu: (t, 0)),
                      resident((D, FF)), resident((D, FF)), resident((FF, D))],
            out_specs=pl.BlockSpec((rows, D), lambda t, te, nu: (t, 0))),
        compiler_params=_cparams(("arbitrary",)),
        name="moe_experts",
    )(tile_expert, n_used, xs, w1, w3, w2)


def _combine_kernel(slot_ref, x_ref, route_ref, g_ref, b_ref, ys_hbm, y_ref, buf, sem, *, tm, alpha):
    _row_gather(slot_ref, pl.program_id(0) * 2 * tm, 2 * tm, ys_hbm, buf, sem)
    p1, p2 = route_ref[:, 2:3], route_ref[:, 3:4]
    f = p1 * buf[:tm, :] + p2 * buf[tm:, :]
    y_ref[...] = _layer_norm(alpha * x_ref[...] + f, g_ref[...], b_ref[...])


def _combine(slots, ys, x2, route, g, b, alpha, tm):
    T, D = x2.shape
    return pl.pallas_call(
        functools.partial(_combine_kernel, tm=tm, alpha=alpha),
        out_shape=jax.ShapeDtypeStruct((T, D), F32),
        grid_spec=pltpu.PrefetchScalarGridSpec(
            num_scalar_prefetch=1, grid=(T // tm,),
            in_specs=[pl.BlockSpec((tm, D), lambda i, s: (i, 0)),
                      pl.BlockSpec((tm, LANES), lambda i, s: (i, 0)),
                      pl.BlockSpec((1, D), lambda i, s: (0, 0)), pl.BlockSpec((1, D), lambda i, s: (0, 0)),
                      pl.BlockSpec(memory_space=pl.ANY)],
            out_specs=pl.BlockSpec((tm, D), lambda i, s: (i, 0)),
            scratch_shapes=[pltpu.VMEM((2 * tm, D), F32), pltpu.SemaphoreType.DMA(())]),
        compiler_params=_cparams(("arbitrary",)),
        name="moe_combine_ln",
    )(slots, x2, route, g.reshape(1, D), b.reshape(1, D), ys)


def _moe_ffn(x2, w_router, w1, w3, w2, g, b, alpha):
    T, D = x2.shape
    E = w1.shape[0]
    rows, tm = 256, 512
    gather_rows = 4 * rows
    route = _router(x2, w_router)

    e_flat = jnp.concatenate([route[:, 0], route[:, 1]]).astype(jnp.int32)
    onehot = (e_flat[:, None] == jnp.arange(E, dtype=jnp.int32)[None, :]).astype(jnp.int32)
    csum = jnp.cumsum(onehot, axis=0)
    counts = csum[-1]
    padded = (counts + rows - 1) // rows * rows
    ends = jnp.cumsum(padded)
    slot = jnp.sum(onehot * (csum - 1 + (ends - padded)[None, :]), axis=1)
    n_slots = 2 * T + E * rows
    tok_of_slot = jnp.zeros((n_slots,), jnp.int32).at[slot].set(jnp.arange(2 * T, dtype=jnp.int32) % T)
    tile_start = jnp.arange(n_slots // rows, dtype=jnp.int32) * rows
    tile_expert = jnp.minimum(jnp.sum((tile_start[:, None] >= ends[None, :]).astype(jnp.int32), axis=1), E - 1)
    n_used = (ends[-1:] // rows).astype(jnp.int32)

    assert n_slots % gather_rows == 0
    xs = _dispatch(tok_of_slot, x2, gather_rows)
    ys = _experts(tile_expert, n_used, xs, w1.astype(BF16), w3.astype(BF16), w2.astype(BF16), rows)
    slots = jnp.concatenate([slot[:T].reshape(T // tm, tm), slot[T:].reshape(T // tm, tm)], axis=1).reshape(-1)
    return _combine(slots, ys, x2, route, g, b, alpha, tm)


def kernel(x, positions, w_in, mla_q_norm, w_uq, mla_kv_norm, w_ukv, w_o_mla, w_o_dsa, w_out,
           ln1_g, ln1_b, ln2_g, ln2_b, dense_w1, dense_w3, dense_w2,
           moe_router, moe_w1, moe_w3, moe_w2):
    B, S, D = x.shape
    depth = w_in.shape[0]
    alpha = (2 * depth) ** 0.25
    T = B * S
    cos, sin = _rope_tables(positions)
    x2 = x.reshape(T, D)
    for l in range(depth):
        w_in_p, w_uq_p, w_ukv_p, w_gates = _prep_in_weights(w_in[l], w_uq[l], w_ukv[l])
        qm, km, vm, dq, dk, dv, iq, ik, iw = _projections(
            x2, cos, sin, w_in_p, w_uq_p, w_ukv_p, mla_q_norm[l], mla_kv_norm[l])
        b3 = lambda a: a.reshape(B, S, a.shape[-1])
        o_a = _mla_attention(b3(qm), b3(km), b3(vm)).reshape(T, -1)
        o_b = _sparse_attention(b3(dq), b3(dk), b3(dv), b3(iq), b3(ik), b3(iw)).reshape(T, -1)
        x2 = _merge(x2, o_a, o_b,
                    _pad_head_rows(w_o_mla[l], MLA_HEADS, MLA_V).astype(BF16),
                    _pad_head_rows(w_o_dsa[l], DSA_HEADS, DSA_HEAD_DIM).astype(BF16),
                    w_gates, w_out[l].astype(BF16), ln1_g[l], ln1_b[l], alpha)
        if l % 2 == 0:
            x2 = _dense_ffn(x2, dense_w1[l // 2], dense_w3[l // 2], dense_w2[l // 2], ln2_g[l], ln2_b[l], alpha)
        else:
            x2 = _moe_ffn(x2, moe_router[l // 2], moe_w1[l // 2], moe_w3[l // 2], moe_w2[l // 2],
                          ln2_g[l], ln2_b[l], alpha)
    return x2.reshape(B, S, D)
```

```python
import functools

import numpy as np
import jax
import jax.numpy as jnp
from jax import lax
from jax.experimental import pallas as pl
from jax.experimental.pallas import tpu as pltpu

F32 = jnp.float32
BF16 = jnp.bfloat16
LANES = 128
VMEM_LIMIT = 56 * 1024 * 1024

MLA_HEADS = 8
MLA_NOPE = 64
MLA_ROPE = 32
MLA_V = 64
Q_LORA = 384
KV_LORA = 256
DSA_HEADS = 8
DSA_HEAD_DIM = 64
IDX_HEADS = 8
IDX_DIM = 32
TOPK_MAX = 256
N_EXPERTS = 8
ROPE_THETA = 10000.0
NORM_EPS = 1e-5
NEG = -1e30
LOG2E = 1.4426950408889634
F32_MIN_NORMAL = 2.0 ** -126
INT_MIN = np.int32(-2 ** 31)
ONE_LANE = 64
assert ONE_LANE >= MLA_V and ONE_LANE >= DSA_HEAD_DIM

C_CQ = 0
C_CKV = C_CQ + Q_LORA
C_KR = C_CKV + KV_LORA
C_DQ = C_KR + LANES
C_DK = C_DQ + DSA_HEADS * DSA_HEAD_DIM
C_DV = C_DK + DSA_HEADS * DSA_HEAD_DIM
C_IQ = C_DV + DSA_HEADS * LANES
C_IK = C_IQ + IDX_HEADS * IDX_DIM
C_IW = C_IK + LANES
C_END = C_IW + LANES


def _cparams(sem):
    return pltpu.CompilerParams(dimension_semantics=sem, vmem_limit_bytes=VMEM_LIMIT)


def _const_spec(shape):
    nd = len(shape)
    return pl.BlockSpec(shape, lambda *_: (0,) * nd)


def _layer_norm(z, g, b):
    mu = jnp.mean(z, axis=-1, keepdims=True)
    zc = z - mu
    var = jnp.mean(zc * zc, axis=-1, keepdims=True)
    return zc * lax.rsqrt(var + NORM_EPS) * g + b


def _rope_tables_kernel(pos_ref, inv_ref, sgn_ref, cos_ref, sin_ref):
    pos = pos_ref[...].astype(F32)
    for p in range(2):
        ang = pos * inv_ref[p:p + 1, :]
        cos_ref[p] = jnp.cos(ang)
        sin_ref[p] = jnp.sin(ang) * sgn_ref[p:p + 1, :]


def _rope_tables(positions):
    T = positions.size
    tm = 1024
    inv32 = jnp.power(ROPE_THETA, -jnp.arange(0, IDX_DIM, 2, dtype=F32) / IDX_DIM)
    inv64 = jnp.power(ROPE_THETA, -jnp.arange(0, DSA_HEAD_DIM, 2, dtype=F32) / DSA_HEAD_DIM)
    inv = jnp.stack([jnp.tile(inv32, LANES // 16), jnp.tile(inv64, LANES // 32)])
    lane = np.arange(LANES)
    sgn = jnp.asarray(np.stack([np.where(lane % 32 < 16, -1.0, 1.0), np.where(lane % 64 < 32, -1.0, 1.0)]), F32)
    out = jax.ShapeDtypeStruct((2, T, LANES), F32)
    return pl.pallas_call(
        _rope_tables_kernel,
        out_shape=(out, out),
        grid=(T // tm,),
        in_specs=[pl.BlockSpec((tm, 1), lambda i: (i, 0)), _const_spec((2, LANES)), _const_spec((2, LANES))],
        out_specs=(pl.BlockSpec((2, tm, LANES), lambda i: (0, i, 0)),) * 2,
        compiler_params=_cparams(("parallel",)),
        name="rope_tables",
    )(positions.reshape(T, 1), inv, sgn)


def _proj_kernel(x_ref, cos_ref, sin_ref, w_in_ref, w_uq_ref, w_ukv_ref, gq_ref, gkv_ref,
                 qm_ref, km_ref, vm_ref, dq_ref, dk_ref, dv_ref, iq_ref, ik_ref, iw_ref, *, mla_scale):
    xb = x_ref[...].astype(BF16)
    lane = lax.broadcasted_iota(jnp.int32, (1, LANES), 1)
    cos32, sin32 = cos_ref[0], sin_ref[0]
    cos64, sin64 = cos_ref[1], sin_ref[1]
    in_rope = (lane >> 5) == (MLA_NOPE >> 5)
    cos_m = jnp.where(in_rope, cos32, 1.0)
    sin_m = jnp.where(in_rope, sin32, 0.0)
    first32 = (lane & (IDX_DIM - 1)) < IDX_DIM // 2
    first64 = (lane & (DSA_HEAD_DIM - 1)) < DSA_HEAD_DIM // 2
    wide_lane = lax.broadcasted_iota(jnp.int32, (1, MLA_HEADS * LANES), 1)
    one_col = jnp.where((wide_lane & (LANES - 1)) == ONE_LANE, 1.0, 0.0)

    def proj(lo, hi):
        return jnp.dot(xb, w_in_ref[:, lo:hi], preferred_element_type=F32)

    def rope(xs, cos, sin, half, first):
        rot = jnp.where(first, pltpu.roll(xs, LANES - half, 1), pltpu.roll(xs, half, 1))
        return xs * cos + rot * sin

    def rms(c, g):
        ms = jnp.mean(c * c, axis=-1, keepdims=True)
        return c * lax.rsqrt(ms + NORM_EPS) * g

    q = jnp.dot(rms(proj(C_CQ, C_CKV), gq_ref[...]).astype(BF16), w_uq_ref[...], preferred_element_type=F32)
    kv = jnp.dot(rms(proj(C_CKV, C_KR), gkv_ref[...]).astype(BF16), w_ukv_ref[...], preferred_element_type=F32)
    kr = rope(proj(C_KR, C_DQ), cos_m, sin_m, MLA_ROPE // 2, first32)
    for h in range(MLA_HEADS):
        sl = slice(h * LANES, (h + 1) * LANES)
        qm_ref[:, sl] = (rope(q[:, sl], cos_m, sin_m, MLA_ROPE // 2, first32) * mla_scale).astype(BF16)
        km_ref[:, sl] = (kv[:, sl] + kr).astype(BF16)
    vm_ref[...] = (kv[:, MLA_HEADS * LANES:] + one_col).astype(BF16)

    dq = proj(C_DQ, C_DK)
    dk = proj(C_DK, C_DV)
    for g in range(DSA_HEADS * DSA_HEAD_DIM // LANES):
        sl = slice(g * LANES, (g + 1) * LANES)
        dq_ref[:, sl] = (rope(dq[:, sl], cos64, sin64, DSA_HEAD_DIM // 2, first64) * LOG2E).astype(BF16)
        dk_ref[:, sl] = rope(dk[:, sl], cos64, sin64, DSA_HEAD_DIM // 2, first64).astype(BF16)
    dv_ref[...] = (proj(C_DV, C_IQ) + one_col).astype(BF16)

    iq = proj(C_IQ, C_IK)
    for g in range(IDX_HEADS * IDX_DIM // LANES):
        sl = slice(g * LANES, (g + 1) * LANES)
        iq_ref[:, sl] = rope(iq[:, sl], cos32, sin32, IDX_DIM // 2, first32).astype(BF16)
    ik_ref[...] = rope(proj(C_IK, C_IW), cos32, sin32, IDX_DIM // 2, first32).astype(BF16)
    iw_ref[...] = proj(C_IW, C_END)


def _pad_heads(w, nh, d):
    k = w.shape[0]
    return jnp.pad(w.reshape(k, nh, d), ((0, 0), (0, 0), (0, LANES - d))).reshape(k, nh * LANES)


def _prep_in_weights(w_in, w_uq, w_ukv):
    d = w_in.shape[0]
    o = np.cumsum([0, Q_LORA, KV_LORA, MLA_ROPE, 512, 512, 512, IDX_HEADS * IDX_DIM, IDX_DIM, IDX_HEADS])
    c_q, c_kv, k_r, dq, dk, dv, iq, ik, iw = [w_in[:, o[i]:o[i + 1]] for i in range(9)]
    gates = w_in[:, o[9]:]
    kr_tile = jnp.pad(k_r, ((0, 0), (MLA_NOPE, LANES - MLA_NOPE - MLA_ROPE)))
    dsa_scale = DSA_HEAD_DIM ** -0.5
    idx_scale = (IDX_HEADS * IDX_DIM) ** -0.5
    w_in_p = jnp.concatenate([
        c_q, c_kv, kr_tile,
        dq * dsa_scale, dk,
        _pad_heads(dv, DSA_HEADS, DSA_HEAD_DIM),
        iq, jnp.tile(ik, (1, LANES // IDX_DIM)),
        jnp.pad(iw * idx_scale, ((0, 0), (0, LANES - IDX_HEADS))),
    ], axis=1).astype(BF16)
    assert w_in_p.shape == (d, C_END)
    w_uq_p = _pad_heads(w_uq, MLA_HEADS, MLA_NOPE + MLA_ROPE).astype(BF16)
    ukv = w_ukv.reshape(KV_LORA, MLA_HEADS, MLA_NOPE + MLA_V)
    w_ukv_p = jnp.concatenate([
        _pad_heads(ukv[:, :, :MLA_NOPE].reshape(KV_LORA, -1), MLA_HEADS, MLA_NOPE),
        _pad_heads(ukv[:, :, MLA_NOPE:].reshape(KV_LORA, -1), MLA_HEADS, MLA_V),
    ], axis=1).astype(BF16)
    return w_in_p, w_uq_p, w_ukv_p, gates.astype(BF16)


def _projections(x2, cos, sin, w_in_p, w_uq_p, w_ukv_p, g_q, g_kv):
    T, D = x2.shape
    tm = 256
    wide = MLA_HEADS * LANES
    row = lambda w: pl.BlockSpec((tm, w), lambda i: (i, 0))
    packed = DSA_HEADS * DSA_HEAD_DIM
    shapes = [(wide, BF16)] * 3 + [(packed, BF16), (packed, BF16), (wide, BF16),
                                   (IDX_HEADS * IDX_DIM, BF16), (LANES, BF16), (LANES, F32)]
    return pl.pallas_call(
        functools.partial(_proj_kernel, mla_scale=(MLA_NOPE + MLA_ROPE) ** -0.5 * LOG2E),
        out_shape=tuple(jax.ShapeDtypeStruct((T, w), dt) for w, dt in shapes),
        grid=(T // tm,),
        in_specs=[row(D),
                  pl.BlockSpec((2, tm, LANES), lambda i: (0, i, 0)),
                  pl.BlockSpec((2, tm, LANES), lambda i: (0, i, 0)),
                  _const_spec(w_in_p.shape), _const_spec(w_uq_p.shape), _const_spec(w_ukv_p.shape),
                  _const_spec((1, Q_LORA)), _const_spec((1, KV_LORA))],
        out_specs=tuple(row(w) for w, _ in shapes),
        compiler_params=_cparams(("parallel",)),
        name="projections",
    )(x2, cos, sin, w_in_p, w_uq_p, w_ukv_p, g_q.reshape(1, -1), g_kv.reshape(1, -1))


SUBLANES = 8


def _sublane_tile_max(mv, s):
    parts = [mv] + [s[r * SUBLANES:(r + 1) * SUBLANES, :] for r in range(s.shape[0] // SUBLANES)]
    while len(parts) > 1:
        parts = [jnp.maximum(a, b) for a, b in zip(parts[::2], parts[1::2])] + ([parts[-1]] if len(parts) % 2 else [])
    return parts[0]


def _transpose_bf16(x):
    return x.astype(F32).T.astype(BF16)


def _online_softmax_step(s, v, m_ref, acc_ref):
    m_old = m_ref[...]
    mx = _sublane_tile_max(s[:SUBLANES, :], s[SUBLANES:, :])
    m_new = jnp.maximum(m_old, jnp.max(mx, axis=0, keepdims=True))
    p = jnp.exp2(s - m_new).astype(BF16)
    acc_ref[...] = jnp.exp2(m_old - m_new) * acc_ref[...] + jnp.dot(_transpose_bf16(v), p, preferred_element_type=F32)
    m_ref[...] = m_new


def _normalise_t(acc_t):
    return acc_t / acc_t[ONE_LANE:ONE_LANE + 1, :]


def _mla_attn_kernel(q_ref, k_ref, v_ref, o_ref, s_sc, qt_sc, mv_sc, m_sc, acc_sc, *, tq):
    qi = pl.program_id(2)
    hg = s_sc.shape[0]
    slabs = [slice(j * LANES, (j + 1) * LANES) for j in range(hg)]
    for j, sl in enumerate(slabs):
        qt_sc[j] = _transpose_bf16(q_ref[:, sl])
    mv_sc[...] = jnp.full(mv_sc.shape, NEG, F32)

    def scores(c, masked):
        off = pl.multiple_of(c * tq, tq)
        for j, sl in enumerate(slabs):
            s = jnp.dot(k_ref[pl.ds(off, tq), sl], qt_sc[j], preferred_element_type=F32)
            if masked:
                krow = lax.broadcasted_iota(jnp.int32, (tq, tq), 0)
                qcol = lax.broadcasted_iota(jnp.int32, (tq, tq), 1)
                s = jnp.where(krow <= qcol, s, NEG)
            s_sc[j, c] = s
            mv_sc[j] = _sublane_tile_max(mv_sc[j], s)

    def score_body(c, carry):
        scores(c, False)
        return carry

    lax.fori_loop(0, qi, score_body, 0)
    scores(qi, True)
    for j in range(hg):
        m_sc[j] = jnp.max(mv_sc[j], axis=0, keepdims=True)
    acc_sc[...] = jnp.zeros(acc_sc.shape, F32)

    def pv_body(c, carry):
        off = pl.multiple_of(c * tq, tq)
        for j, sl in enumerate(slabs):
            p = jnp.exp2(s_sc[j, c] - m_sc[j]).astype(BF16)
            acc_sc[j] += jnp.dot(_transpose_bf16(v_ref[pl.ds(off, tq), sl]), p, preferred_element_type=F32)
        return carry

    lax.fori_loop(0, qi + 1, pv_body, 0)
    for j, sl in enumerate(slabs):
        o_ref[:, sl] = _normalise_t(acc_sc[j]).T.astype(o_ref.dtype)


def _mla_attention(q, k, v):
    B, S, W = q.shape
    tq, hg = 512, 2
    blk_q = pl.BlockSpec((None, tq, hg * LANES), lambda b, h, i: (b, i, h))
    blk_kv = pl.BlockSpec((None, S, hg * LANES), lambda b, h, i: (b, 0, h))
    return pl.pallas_call(
        functools.partial(_mla_attn_kernel, tq=tq),
        out_shape=jax.ShapeDtypeStruct((B, S, W), BF16),
        grid=(B, W // (hg * LANES), S // tq),
        in_specs=[blk_q, blk_kv, blk_kv],
        out_specs=blk_q,
        scratch_shapes=[pltpu.VMEM((hg, S // tq, tq, tq), F32), pltpu.VMEM((hg, LANES, tq), BF16),
                        pltpu.VMEM((hg, SUBLANES, tq), F32), pltpu.VMEM((hg, 1, tq), F32),
                        pltpu.VMEM((hg, LANES, tq), F32)],
        compiler_params=_cparams(("parallel", "parallel", "arbitrary")),
        name="mla_attention",
    )(q, k, v)


def _dsa_kernel(dq_ref, dk_ref, dv_ref, iq_ref, ik_ref, iw_ref, o_ref,
                key_sc, hi_sc, bias_sc, wt_sc, qmt_sc, qt_sc, m_sc, acc_sc, p_sc, al_sc,
                *, tq, ck, n_sel):
    qi = pl.program_id(1)
    nch = (qi * tq + tq + ck - 1) // ck
    lane = lax.broadcasted_iota(jnp.int32, (1, LANES), 1)
    k_sel = float(n_sel)

    wt_sc[...] = iw_ref[...].T
    for h in range(IDX_HEADS):
        g, lo = divmod(h * IDX_DIM, LANES)
        qg = iq_ref[:, g * LANES:(g + 1) * LANES].astype(F32)
        in_head = (lane >> 5) == (lo >> 5)
        qmt_sc[:, h * tq:(h + 1) * tq] = jnp.where(in_head, qg, 0.0).T.astype(BF16)

    kw = 2 * LANES
    qpos_t = qi * tq + lax.broadcasted_iota(jnp.int32, (kw, tq), 1)
    krow = lax.broadcasted_iota(jnp.int32, (kw, tq), 0)

    def score_body(c, carry):
        for part in range(ck // kw):
            ks = ik_ref[pl.ds(pl.multiple_of(c * ck + part * kw, kw), kw), :]
            d = jnp.dot(ks, qmt_sc[...], preferred_element_type=F32)
            score = jnp.zeros((kw, tq), F32)
            for h in range(IDX_HEADS):
                score = score + wt_sc[h:h + 1, :] * jnp.maximum(d[:, h * tq:(h + 1) * tq], 0.0)
            score = jnp.where(jnp.abs(score) < F32_MIN_NORMAL, 0.0, score)
            bits = lax.bitcast_convert_type(score, jnp.int32)
            key = jnp.where(bits >= 0, bits, bits ^ np.int32(0x7FFFFFFF))
            causal = c * ck + part * kw + krow <= qpos_t
            key_sc[c, part * kw:(part + 1) * kw, :] = jnp.where(causal, key, INT_MIN)
            top = lax.bitcast_convert_type(bits & np.int32(-65536), F32)
            hi_sc[c, part * kw:(part + 1) * kw, :] = jnp.where(causal, top, jnp.nan).astype(BF16)
        return carry

    lax.fori_loop(0, nch, score_body, 0)

    sub = 8

    n_acc = 4

    def count_ge(thr):
        thr_t = jnp.broadcast_to(thr, (sub, tq))

        def body(c, accs):
            accs = list(accs)
            for r in range(ck // sub):
                ind = jnp.where(key_sc[c, r * sub:(r + 1) * sub, :] >= thr_t, 1.0, 0.0)
                accs[r % n_acc] = accs[r % n_acc] + ind
            return tuple(accs)

        accs = lax.fori_loop(0, nch, body, (jnp.zeros((sub, tq), F32),) * n_acc)
        return jnp.sum(sum(accs[1:], accs[0]), axis=0, keepdims=True)

    pack = 16

    def count_top_ge(top16):
        top16 = jnp.where(top16 > 0, jnp.maximum(top16, np.int32(0x0080)), top16)
        fbits = jnp.where(top16 >= 0, top16, top16 ^ np.int32(0x7FFF)) << 16
        thr_t = jnp.broadcast_to(lax.bitcast_convert_type(fbits, F32).astype(BF16), (pack, tq))
        one, zero = jnp.ones((pack, tq), BF16), jnp.zeros((pack, tq), BF16)

        def body(c, accs):
            accs = list(accs)
            for r in range(ck // pack):
                ind = jnp.where(hi_sc[c, r * pack:(r + 1) * pack, :] >= thr_t, one, zero)
                accs[r % n_acc] = accs[r % n_acc] + ind
            return tuple(accs)

        accs = lax.fori_loop(0, nch, body, (zero,) * n_acc)
        return jnp.sum(sum((a.astype(F32) for a in accs[1:]), accs[0].astype(F32)), axis=0, keepdims=True)

    n_all = (nch * ck).astype(F32)
    n_nonneg = count_top_ge(jnp.zeros((1, tq), jnp.int32))
    top0 = jnp.where(n_nonneg >= k_sel, np.int32(0), np.int32(-32768))
    n_ge0 = jnp.where(n_nonneg >= k_sel, n_nonneg, n_all)

    def search(count, first_bit, n_bits, start):
        def step(i, carry):
            tau, n_ge = carry
            cand = tau | (jnp.int32(1) << (first_bit - i))
            cnt = count(cand)
            take = cnt >= k_sel
            return jnp.where(take, cand, tau), jnp.where(take, cnt, n_ge)
        return lax.fori_loop(0, n_bits, step, start)

    top, n_ge = search(count_top_ge, 14, 15, (top0, n_ge0))
    tau, n_ge = search(count_ge, 15, 16, (top << 16, n_ge))

    excess = n_ge - k_sel
    qpos_c = qi * tq + lax.broadcasted_iota(jnp.int32, (ck, tq), 1)
    krow_c = lax.broadcasted_iota(jnp.int32, (ck, tq), 0)
    tri = jnp.where(lax.broadcasted_iota(jnp.int32, (ck, ck), 0) <= lax.broadcasted_iota(jnp.int32, (ck, ck), 1),
                    1.0, 0.0).astype(BF16)

    def bias_body(i, ties_after):
        c = nch - 1 - i
        blk = key_sc[c]
        is_tie = blk == tau
        suffix = jnp.dot(tri, jnp.where(is_tie, 1.0, 0.0).astype(BF16), preferred_element_type=F32)
        tie = jnp.where(is_tie, jnp.where(ties_after + suffix > excess, 0.0, NEG), NEG)
        sel = jnp.where(blk > tau, 0.0, tie)
        bias_sc[c] = jnp.where(c * ck + krow_c <= qpos_c, sel, NEG)
        return ties_after + suffix[0:1, :]

    lax.fori_loop(0, nch, bias_body, jnp.zeros((1, tq), F32))

    for h in range(DSA_HEADS):
        g, lo = divmod(h * DSA_HEAD_DIM, LANES)
        qt = dq_ref[:, g * LANES:(g + 1) * LANES].astype(F32).T
        dim = lax.broadcasted_iota(jnp.int32, (LANES, tq), 0)
        qt_sc[h] = jnp.where((dim >> 6) == (lo >> 6), qt, 0.0).astype(BF16)
    m_sc[...] = jnp.full(m_sc.shape, NEG, F32)
    acc_sc[...] = jnp.zeros(acc_sc.shape, F32)

    def attn_body(c, carry):
        off = pl.multiple_of(c * ck, ck)
        bias = bias_sc[c]
        for h in range(DSA_HEADS):
            g = h * DSA_HEAD_DIM // LANES
            k_slab = dk_ref[pl.ds(off, ck), g * LANES:(g + 1) * LANES]
            s = jnp.dot(k_slab, qt_sc[h], preferred_element_type=F32) + bias
            m_old = m_sc[h]
            mx = _sublane_tile_max(s[:SUBLANES, :], s[SUBLANES:, :])
            m_new = jnp.maximum(m_old, jnp.max(mx, axis=0, keepdims=True))
            p_sc[h] = jnp.exp2(s - m_new).astype(BF16)
            al_sc[h] = jnp.exp2(m_old - m_new)
            m_sc[h] = m_new
        for h in range(DSA_HEADS):
            sl = slice(h * LANES, (h + 1) * LANES)
            pv = jnp.dot(_transpose_bf16(dv_ref[pl.ds(off, ck), sl]), p_sc[h], preferred_element_type=F32)
            acc_sc[h] = al_sc[h] * acc_sc[h] + pv
        return carry

    lax.fori_loop(0, nch, attn_body, 0)
    for h in range(DSA_HEADS):
        o_ref[:, h * LANES:(h + 1) * LANES] = _normalise_t(acc_sc[h]).T.astype(o_ref.dtype)


def _sparse_attention(dq, dk, dv, iq, ik, iw):
    B, S, W = dv.shape
    wqk = dq.shape[-1]
    tq, ck = 256, 512
    n_sel = min(TOPK_MAX, S // 4)
    assert S & (S - 1) == 0 and S % ck == 0
    blk_q = lambda w: pl.BlockSpec((None, tq, w), lambda b, i: (b, i, 0))
    blk_k = lambda w: pl.BlockSpec((None, S, w), lambda b, i: (b, 0, 0), pipeline_mode=pl.Buffered(1))
    return pl.pallas_call(
        functools.partial(_dsa_kernel, tq=tq, ck=ck, n_sel=n_sel),
        out_shape=jax.ShapeDtypeStruct((B, S, W), BF16),
        grid=(B, S // tq),
        in_specs=[blk_q(wqk), blk_k(wqk), blk_k(W), blk_q(iq.shape[-1]), blk_k(LANES), blk_q(LANES)],
        out_specs=blk_q(W),
        scratch_shapes=[
            pltpu.VMEM((S // ck, ck, tq), jnp.int32),
            pltpu.VMEM((S // ck, ck, tq), BF16),
            pltpu.VMEM((S // ck, ck, tq), F32),
            pltpu.VMEM((LANES, tq), F32),
            pltpu.VMEM((LANES, IDX_HEADS * tq), BF16),
            pltpu.VMEM((DSA_HEADS, LANES, tq), BF16),
            pltpu.VMEM((DSA_HEADS, 1, tq), F32),
            pltpu.VMEM((DSA_HEADS, LANES, tq), F32),
            pltpu.VMEM((DSA_HEADS, ck, tq), BF16),
            pltpu.VMEM((DSA_HEADS, 1, tq), F32)],
        compiler_params=_cparams(("parallel", "arbitrary")),
        name="sparse_attention",
    )(dq, dk, dv, iq, ik, iw)


def _merge_kernel(x_ref, oa_ref, ob_ref, woa_ref, wob_ref, wg_ref, wout_ref, g_ref, b_ref, y_ref, *, alpha):
    x = x_ref[...]
    xb = x.astype(BF16)
    d = x.shape[-1]
    ya = jnp.dot(oa_ref[...], woa_ref[...], preferred_element_type=F32)
    yb = jnp.dot(ob_ref[...], wob_ref[...], preferred_element_type=F32)
    ga = jax.nn.sigmoid(jnp.dot(xb, wg_ref[:, :d], preferred_element_type=F32))
    gb = jax.nn.sigmoid(jnp.dot(xb, wg_ref[:, d:], preferred_element_type=F32))
    merged = ga * ya + gb * yb
    h = jnp.dot(merged.astype(BF16), wout_ref[...], preferred_element_type=F32)
    y_ref[...] = _layer_norm(alpha * x + h, g_ref[...], b_ref[...])


def _pad_head_rows(w, nh, d):
    n = w.shape[1]
    return jnp.pad(w.reshape(nh, d, n), ((0, 0), (0, LANES - d), (0, 0))).reshape(nh * LANES, n)


def _merge(x2, oa, ob, woa, wob, wg, wout, g, b, alpha):
    T, D = x2.shape
    tm = 256
    row = lambda w: pl.BlockSpec((tm, w), lambda i: (i, 0))
    return pl.pallas_call(
        functools.partial(_merge_kernel, alpha=alpha),
        out_shape=jax.ShapeDtypeStruct((T, D), F32),
        grid=(T // tm,),
        in_specs=[row(D), row(oa.shape[1]), row(ob.shape[1]),
                  _const_spec(woa.shape), _const_spec(wob.shape), _const_spec(wg.shape), _const_spec(wout.shape),
                  _const_spec((1, D)), _const_spec((1, D))],
        out_specs=row(D),
        compiler_params=_cparams(("parallel",)),
        name="merge_outproj_ln",
    )(x2, oa, ob, woa, wob, wg, wout, g.reshape(1, D), b.reshape(1, D))


def _ffn_kernel(x_ref, w1_ref, w3_ref, w2_ref, g_ref, b_ref, y_ref, xb_sc, acc_sc, *, alpha):
    f = pl.program_id(1)

    @pl.when(f == 0)
    def _():
        xb_sc[...] = x_ref[...].astype(BF16)
        acc_sc[...] = jnp.zeros(acc_sc.shape, F32)

    xb = xb_sc[...]
    a = jax.nn.silu(jnp.dot(xb, w1_ref[...], preferred_element_type=F32))
    a = a * jnp.dot(xb, w3_ref[...], preferred_element_type=F32)
    acc_sc[...] += jnp.dot(a.astype(BF16), w2_ref[...], preferred_element_type=F32)

    @pl.when(f == pl.num_programs(1) - 1)
    def _():
        y_ref[...] = _layer_norm(alpha * x_ref[...] + acc_sc[...], g_ref[...], b_ref[...])


def _dense_ffn(x2, w1, w3, w2, g, b, alpha):
    T, D = x2.shape
    FF = w1.shape[1]
    tm, tf = 1024, 256
    assert FF % tf == 0
    return pl.pallas_call(
        functools.partial(_ffn_kernel, alpha=alpha),
        out_shape=jax.ShapeDtypeStruct((T, D), F32),
        grid=(T // tm, FF // tf),
        in_specs=[pl.BlockSpec((tm, D), lambda i, f: (i, 0)),
                  pl.BlockSpec((D, tf), lambda i, f: (0, f)),
                  pl.BlockSpec((D, tf), lambda i, f: (0, f)),
                  pl.BlockSpec((tf, D), lambda i, f: (f, 0)),
                  _const_spec((1, D)), _const_spec((1, D))],
        out_specs=pl.BlockSpec((tm, D), lambda i, f: (i, 0)),
        scratch_shapes=[pltpu.VMEM((tm, D), BF16), pltpu.VMEM((tm, D), F32)],
        compiler_params=_cparams(("parallel", "arbitrary")),
        name="dense_ffn_ln",
    )(x2, w1.astype(BF16), w3.astype(BF16), w2.astype(BF16), g.reshape(1, D), b.reshape(1, D))


def _router_kernel(x_ref, wr_ref, route_ref):
    logits = jnp.dot(x_ref[...], wr_ref[...], preferred_element_type=F32, precision=lax.Precision.HIGHEST)
    lane = lax.broadcasted_iota(jnp.int32, logits.shape, 1).astype(F32)
    logits = jnp.where(lane < N_EXPERTS, logits, -jnp.inf)
    v1 = jnp.max(logits, axis=1, keepdims=True)
    i1 = jnp.min(jnp.where(logits == v1, lane, float(LANES)), axis=1, keepdims=True)
    rest = jnp.where(lane == i1, -jnp.inf, logits)
    v2 = jnp.max(rest, axis=1, keepdims=True)
    i2 = jnp.min(jnp.where(rest == v2, lane, float(LANES)), axis=1, keepdims=True)
    e2 = jnp.exp(v2 - v1)
    p1 = 1.0 / (1.0 + e2)
    route_ref[...] = jnp.where(lane == 0.0, i1, jnp.where(lane == 1.0, i2, jnp.where(lane == 2.0, p1, e2 * p1)))


def _router(x2, w_router):
    T, D = x2.shape
    tm = 512
    wr = jnp.pad(w_router, ((0, 0), (0, LANES - N_EXPERTS)))
    return pl.pallas_call(
        _router_kernel,
        out_shape=jax.ShapeDtypeStruct((T, LANES), F32),
        grid=(T // tm,),
        in_specs=[pl.BlockSpec((tm, D), lambda i: (i, 0)), _const_spec(wr.shape)],
        out_specs=pl.BlockSpec((tm, LANES), lambda i: (i, 0)),
        compiler_params=_cparams(("parallel",)),
        name="router",
    )(x2, wr)


def _row_gather(idx_ref, base, n_rows, src_hbm, dst_ref, sem):
    def issue(r, c):
        row = idx_ref[base + r]
        pltpu.make_async_copy(src_hbm.at[pl.ds(row, 1), :], dst_ref.at[pl.ds(r, 1), :], sem).start()
        return c

    lax.fori_loop(0, n_rows, issue, 0, unroll=8)
    pltpu.make_async_copy(src_hbm.at[pl.ds(0, n_rows), :], dst_ref, sem).wait()


def _dispatch_kernel(idx_ref, src_hbm, out_ref, sem, *, rows):
    _row_gather(idx_ref, pl.program_id(0) * rows, rows, src_hbm, out_ref, sem)


def _dispatch(tok_of_slot, x2, rows):
    n, D = tok_of_slot.shape[0], x2.shape[1]
    return pl.pallas_call(
        functools.partial(_dispatch_kernel, rows=rows),
        out_shape=jax.ShapeDtypeStruct((n, D), x2.dtype),
        grid_spec=pltpu.PrefetchScalarGridSpec(
            num_scalar_prefetch=1, grid=(n // rows,),
            in_specs=[pl.BlockSpec(memory_space=pl.ANY)],
            out_specs=pl.BlockSpec((rows, D), lambda t, idx: (t, 0)),
            scratch_shapes=[pltpu.SemaphoreType.DMA(())]),
        compiler_params=_cparams(("arbitrary",)),
        name="moe_dispatch",
    )(tok_of_slot, x2)


def _experts_kernel(te_ref, nu_ref, xs_ref, w1_ref, w3_ref, w2_ref, ys_ref, *, tf):
    t = pl.program_id(0)

    @pl.when(t < nu_ref[0])
    def _():
        xb = xs_ref[...].astype(BF16)
        acc = jnp.zeros(ys_ref.shape, F32)
        for f in range(w1_ref.shape[1] // tf):
            cols = slice(f * tf, (f + 1) * tf)
            a = jax.nn.silu(jnp.dot(xb, w1_ref[:, cols], preferred_element_type=F32))
            a = a * jnp.dot(xb, w3_ref[:, cols], preferred_element_type=F32)
            acc = acc + jnp.dot(a.astype(BF16), w2_ref[cols, :], preferred_element_type=F32)
        ys_ref[...] = acc

    @pl.when(t >= nu_ref[0])
    def _():
        ys_ref[...] = jnp.zeros(ys_ref.shape, F32)


def _experts(tile_expert, n_used, xs, w1, w3, w2, rows):
    P, D = xs.shape
    E, _, FF = w1.shape
    resident = lambda shape: pl.BlockSpec((None,) + shape, lambda t, te, nu: (te[t], 0, 0),
                                          pipeline_mode=pl.Buffered(1))
    return pl.pallas_call(
        functools.partial(_experts_kernel, tf=512),
        out_shape=jax.ShapeDtypeStruct((P, D), F32),
        grid_spec=pltpu.PrefetchScalarGridSpec(
            num_scalar_prefetch=2, grid=(P // rows,),
            in_specs=[pl.BlockSpec((rows, D), lambda t, te, nu: (t, 0)),
                      resident((D, FF)), resident((D, FF)), resident((FF, D))],
            out_specs=pl.BlockSpec((rows, D), lambda t, te, nu: (t, 0))),
        compiler_params=_cparams(("arbitrary",)),
        name="moe_experts",
    )(tile_expert, n_used, xs, w1, w3, w2)


def _combine_kernel(slot_ref, x_ref, route_ref, g_ref, b_ref, ys_hbm, y_ref, buf, sem, *, tm, alpha):
    _row_gather(slot_ref, pl.program_id(0) * 2 * tm, 2 * tm, ys_hbm, buf, sem)
    p1, p2 = route_ref[:, 2:3], route_ref[:, 3:4]
    f = p1 * buf[:tm, :] + p2 * buf[tm:, :]
    y_ref[...] = _layer_norm(alpha * x_ref[...] + f, g_ref[...], b_ref[...])


def _combine(slots, ys, x2, route, g, b, alpha, tm):
    T, D = x2.shape
    return pl.pallas_call(
        functools.partial(_combine_kernel, tm=tm, alpha=alpha),
        out_shape=jax.ShapeDtypeStruct((T, D), F32),
        grid_spec=pltpu.PrefetchScalarGridSpec(
            num_scalar_prefetch=1, grid=(T // tm,),
            in_specs=[pl.BlockSpec((tm, D), lambda i, s: (i, 0)),
                      pl.BlockSpec((tm, LANES), lambda i, s: (i, 0)),
                      pl.BlockSpec((1, D), lambda i, s: (0, 0)), pl.BlockSpec((1, D), lambda i, s: (0, 0)),
                      pl.BlockSpec(memory_space=pl.ANY)],
            out_specs=pl.BlockSpec((tm, D), lambda i, s: (i, 0)),
            scratch_shapes=[pltpu.VMEM((2 * tm, D), F32), pltpu.SemaphoreType.DMA(())]),
        compiler_params=_cparams(("arbitrary",)),
        name="moe_combine_ln",
    )(slots, x2, route, g.reshape(1, D), b.reshape(1, D), ys)


def _moe_ffn(x2, w_router, w1, w3, w2, g, b, alpha):
    T, D = x2.shape
    E = w1.shape[0]
    rows, tm = 256, 512
    gather_rows = 4 * rows
    route = _router(x2, w_router)

    e_flat = jnp.concatenate([route[:, 0], route[:, 1]]).astype(jnp.int32)
    onehot = (e_flat[:, None] == jnp.arange(E, dtype=jnp.int32)[None, :]).astype(jnp.int32)
    csum = jnp.cumsum(onehot, axis=0)
    counts = csum[-1]
    padded = (counts + rows - 1) // rows * rows
    ends = jnp.cumsum(padded)
    slot = jnp.sum(onehot * (csum - 1 + (ends - padded)[None, :]), axis=1)
    n_slots = 2 * T + E * rows
    tok_of_slot = jnp.zeros((n_slots,), jnp.int32).at[slot].set(jnp.arange(2 * T, dtype=jnp.int32) % T)
    tile_start = jnp.arange(n_slots // rows, dtype=jnp.int32) * rows
    tile_expert = jnp.minimum(jnp.sum((tile_start[:, None] >= ends[None, :]).astype(jnp.int32), axis=1), E - 1)
    n_used = (ends[-1:] // rows).astype(jnp.int32)

    assert n_slots % gather_rows == 0
    xs = _dispatch(tok_of_slot, x2, gather_rows)
    ys = _experts(tile_expert, n_used, xs, w1.astype(BF16), w3.astype(BF16), w2.astype(BF16), rows)
    slots = jnp.concatenate([slot[:T].reshape(T // tm, tm), slot[T:].reshape(T // tm, tm)], axis=1).reshape(-1)
    return _combine(slots, ys, x2, route, g, b, alpha, tm)


def kernel(x, positions, w_in, mla_q_norm, w_uq, mla_kv_norm, w_ukv, w_o_mla, w_o_dsa, w_out,
           ln1_g, ln1_b, ln2_g, ln2_b, dense_w1, dense_w3, dense_w2,
           moe_router, moe_w1, moe_w3, moe_w2):
    B, S, D = x.shape
    depth = w_in.shape[0]
    alpha = (2 * depth) ** 0.25
    T = B * S
    cos, sin = _rope_tables(positions)
    x2 = x.reshape(T, D)
    for l in range(depth):
        w_in_p, w_uq_p, w_ukv_p, w_gates = _prep_in_weights(w_in[l], w_uq[l], w_ukv[l])
        qm, km, vm, dq, dk, dv, iq, ik, iw = _projections(
            x2, cos, sin, w_in_p, w_uq_p, w_ukv_p, mla_q_norm[l], mla_kv_norm[l])
        b3 = lambda a: a.reshape(B, S, a.shape[-1])
        o_a = _mla_attention(b3(qm), b3(km), b3(vm)).reshape(T, -1)
        o_b = _sparse_attention(b3(dq), b3(dk), b3(dv), b3(iq), b3(ik), b3(iw)).reshape(T, -1)
        x2 = _merge(x2, o_a, o_b,
                    _pad_head_rows(w_o_mla[l], MLA_HEADS, MLA_V).astype(BF16),
                    _pad_head_rows(w_o_dsa[l], DSA_HEADS, DSA_HEAD_DIM).astype(BF16),
                    w_gates, w_out[l].astype(BF16), ln1_g[l], ln1_b[l], alpha)
        if l % 2 == 0:
            x2 = _dense_ffn(x2, dense_w1[l // 2], dense_w3[l // 2], dense_w2[l // 2], ln2_g[l], ln2_b[l], alpha)
        else:
            x2 = _moe_ffn(x2, moe_router[l // 2], moe_w1[l // 2], moe_w3[l // 2], moe_w2[l // 2],
                          ln2_g[l], ln2_b[l], alpha)
    return x2.reshape(B, S, D)
```

```python
import functools

import numpy as np
import jax
import jax.numpy as jnp
from jax import lax
from jax.experimental import pallas as pl
from jax.experimental.pallas import tpu as pltpu

F32 = jnp.float32
BF16 = jnp.bfloat16
LANES = 128
VMEM_LIMIT = 56 * 1024 * 1024

MLA_HEADS = 8
MLA_NOPE = 64
MLA_ROPE = 32
MLA_V = 64
Q_LORA = 384
KV_LORA = 256
DSA_HEADS = 8
DSA_HEAD_DIM = 64
IDX_HEADS = 8
IDX_DIM = 32
TOPK_MAX = 256
N_EXPERTS = 8
ROPE_THETA = 10000.0
NORM_EPS = 1e-5
NEG = -1e30
LOG2E = 1.4426950408889634
F32_MIN_NORMAL = 2.0 ** -126
INT_MIN = np.int32(-2 ** 31)
ONE_LANE = 64
assert ONE_LANE >= MLA_V and ONE_LANE >= DSA_HEAD_DIM

C_CQ = 0
C_CKV = C_CQ + Q_LORA
C_KR = C_CKV + KV_LORA
C_DQ = C_KR + LANES
C_DK = C_DQ + DSA_HEADS * DSA_HEAD_DIM
C_DV = C_DK + DSA_HEADS * DSA_HEAD_DIM
C_IQ = C_DV + DSA_HEADS * LANES
C_IK = C_IQ + IDX_HEADS * IDX_DIM
C_IW = C_IK + LANES
C_END = C_IW + LANES


def _cparams(sem):
    return pltpu.CompilerParams(dimension_semantics=sem, vmem_limit_bytes=VMEM_LIMIT)


def _const_spec(shape):
    nd = len(shape)
    return pl.BlockSpec(shape, lambda *_: (0,) * nd)


def _layer_norm(z, g, b):
    mu = jnp.mean(z, axis=-1, keepdims=True)
    zc = z - mu
    var = jnp.mean(zc * zc, axis=-1, keepdims=True)
    return zc * lax.rsqrt(var + NORM_EPS) * g + b


def _rope_tables_kernel(pos_ref, inv_ref, sgn_ref, cos_ref, sin_ref):
    pos = pos_ref[...].astype(F32)
    for p in range(2):
        ang = pos * inv_ref[p:p + 1, :]
        cos_ref[p] = jnp.cos(ang)
        sin_ref[p] = jnp.sin(ang) * sgn_ref[p:p + 1, :]


def _rope_tables(positions):
    T = positions.size
    tm = 1024
    inv32 = jnp.power(ROPE_THETA, -jnp.arange(0, IDX_DIM, 2, dtype=F32) / IDX_DIM)
    inv64 = jnp.power(ROPE_THETA, -jnp.arange(0, DSA_HEAD_DIM, 2, dtype=F32) / DSA_HEAD_DIM)
    inv = jnp.stack([jnp.tile(inv32, LANES // 16), jnp.tile(inv64, LANES // 32)])
    lane = np.arange(LANES)
    sgn = jnp.asarray(np.stack([np.where(lane % 32 < 16, -1.0, 1.0), np.where(lane % 64 < 32, -1.0, 1.0)]), F32)
    out = jax.ShapeDtypeStruct((2, T, LANES), F32)
    return pl.pallas_call(
        _rope_tables_kernel,
        out_shape=(out, out),
        grid=(T // tm,),
        in_specs=[pl.BlockSpec((tm, 1), lambda i: (i, 0)), _const_spec((2, LANES)), _const_spec((2, LANES))],
        out_specs=(pl.BlockSpec((2, tm, LANES), lambda i: (0, i, 0)),) * 2,
        compiler_params=_cparams(("parallel",)),
        name="rope_tables",
    )(positions.reshape(T, 1), inv, sgn)


def _proj_kernel(x_ref, cos_ref, sin_ref, w_in_ref, w_uq_ref, w_ukv_ref, gq_ref, gkv_ref,
                 qm_ref, km_ref, vm_ref, dq_ref, dk_ref, dv_ref, iq_ref, ik_ref, iw_ref, *, mla_scale):
    xb = x_ref[...].astype(BF16)
    lane = lax.broadcasted_iota(jnp.int32, (1, LANES), 1)
    cos32, sin32 = cos_ref[0], sin_ref[0]
    cos64, sin64 = cos_ref[1], sin_ref[1]
    in_rope = (lane >> 5) == (MLA_NOPE >> 5)
    cos_m = jnp.where(in_rope, cos32, 1.0)
    sin_m = jnp.where(in_rope, sin32, 0.0)
    first32 = (lane & (IDX_DIM - 1)) < IDX_DIM // 2
    first64 = (lane & (DSA_HEAD_DIM - 1)) < DSA_HEAD_DIM // 2
    wide_lane = lax.broadcasted_iota(jnp.int32, (1, MLA_HEADS * LANES), 1)
    one_col = jnp.where((wide_lane & (LANES - 1)) == ONE_LANE, 1.0, 0.0)

    def proj(lo, hi):
        return jnp.dot(xb, w_in_ref[:, lo:hi], preferred_element_type=F32)

    def rope(xs, cos, sin, half, first):
        rot = jnp.where(first, pltpu.roll(xs, LANES - half, 1), pltpu.roll(xs, half, 1))
        return xs * cos + rot * sin

    def rms(c, g):
        ms = jnp.mean(c * c, axis=-1, keepdims=True)
        return c * lax.rsqrt(ms + NORM_EPS) * g

    q = jnp.dot(rms(proj(C_CQ, C_CKV), gq_ref[...]).astype(BF16), w_uq_ref[...], preferred_element_type=F32)
    kv = jnp.dot(rms(proj(C_CKV, C_KR), gkv_ref[...]).astype(BF16), w_ukv_ref[...], preferred_element_type=F32)
    kr = rope(proj(C_KR, C_DQ), cos_m, sin_m, MLA_ROPE // 2, first32)
    for h in range(MLA_HEADS):
        sl = slice(h * LANES, (h + 1) * LANES)
        qm_ref[:, sl] = (rope(q[:, sl], cos_m, sin_m, MLA_ROPE // 2, first32) * mla_scale).astype(BF16)
        km_ref[:, sl] = (kv[:, sl] + kr).astype(BF16)
    vm_ref[...] = (kv[:, MLA_HEADS * LANES:] + one_col).astype(BF16)

    dq = proj(C_DQ, C_DK)
    dk = proj(C_DK, C_DV)
    for g in range(DSA_HEADS * DSA_HEAD_DIM // LANES):
        sl = slice(g * LANES, (g + 1) * LANES)
        dq_ref[:, sl] = (rope(dq[:, sl], cos64, sin64, DSA_HEAD_DIM // 2, first64) * LOG2E).astype(BF16)
        dk_ref[:, sl] = rope(dk[:, sl], cos64, sin64, DSA_HEAD_DIM // 2, first64).astype(BF16)
    dv_ref[...] = (proj(C_DV, C_IQ) + one_col).astype(BF16)

    iq = proj(C_IQ, C_IK)
    for g in range(IDX_HEADS * IDX_DIM // LANES):
        sl = slice(g * LANES, (g + 1) * LANES)
        iq_ref[:, sl] = rope(iq[:, sl], cos32, sin32, IDX_DIM // 2, first32).astype(BF16)
    ik_ref[...] = rope(proj(C_IK, C_IW), cos32, sin32, IDX_DIM // 2, first32).astype(BF16)
    iw_ref[...] = proj(C_IW, C_END)


def _pad_heads(w, nh, d):
    k = w.shape[0]
    return jnp.pad(w.reshape(k, nh, d), ((0, 0), (0, 0), (0, LANES - d))).reshape(k, nh * LANES)


def _prep_in_weights(w_in, w_uq, w_ukv):
    d = w_in.shape[0]
    o = np.cumsum([0, Q_LORA, KV_LORA, MLA_ROPE, 512, 512, 512, IDX_HEADS * IDX_DIM, IDX_DIM, IDX_HEADS])
    c_q, c_kv, k_r, dq, dk, dv, iq, ik, iw = [w_in[:, o[i]:o[i + 1]] for i in range(9)]
    gates = w_in[:, o[9]:]
    kr_tile = jnp.pad(k_r, ((0, 0), (MLA_NOPE, LANES - MLA_NOPE - MLA_ROPE)))
    dsa_scale = DSA_HEAD_DIM ** -0.5
    idx_scale = (IDX_HEADS * IDX_DIM) ** -0.5
    w_in_p = jnp.concatenate([
        c_q, c_kv, kr_tile,
        dq * dsa_scale, dk,
        _pad_heads(dv, DSA_HEADS, DSA_HEAD_DIM),
        iq, jnp.tile(ik, (1, LANES // IDX_DIM)),
        jnp.pad(iw * idx_scale, ((0, 0), (0, LANES - IDX_HEADS))),
    ], axis=1).astype(BF16)
    assert w_in_p.shape == (d, C_END)
    w_uq_p = _pad_heads(w_uq, MLA_HEADS, MLA_NOPE + MLA_ROPE).astype(BF16)
    ukv = w_ukv.reshape(KV_LORA, MLA_HEADS, MLA_NOPE + MLA_V)
    w_ukv_p = jnp.concatenate([
        _pad_heads(ukv[:, :, :MLA_NOPE].reshape(KV_LORA, -1), MLA_HEADS, MLA_NOPE),
        _pad_heads(ukv[:, :, MLA_NOPE:].reshape(KV_LORA, -1), MLA_HEADS, MLA_V),
    ], axis=1).astype(BF16)
    return w_in_p, w_uq_p, w_ukv_p, gates.astype(BF16)


def _projections(x2, cos, sin, w_in_p, w_uq_p, w_ukv_p, g_q, g_kv):
    T, D = x2.shape
    tm = 256
    wide = MLA_HEADS * LANES
    row = lambda w: pl.BlockSpec((tm, w), lambda i: (i, 0))
    packed = DSA_HEADS * DSA_HEAD_DIM
    shapes = [(wide, BF16)] * 3 + [(packed, BF16), (packed, BF16), (wide, BF16),
                                   (IDX_HEADS * IDX_DIM, BF16), (LANES, BF16), (LANES, F32)]
    return pl.pallas_call(
        functools.partial(_proj_kernel, mla_scale=(MLA_NOPE + MLA_ROPE) ** -0.5 * LOG2E),
        out_shape=tuple(jax.ShapeDtypeStruct((T, w), dt) for w, dt in shapes),
        grid=(T // tm,),
        in_specs=[row(D),
                  pl.BlockSpec((2, tm, LANES), lambda i: (0, i, 0)),
                  pl.BlockSpec((2, tm, LANES), lambda i: (0, i, 0)),
                  _const_spec(w_in_p.shape), _const_spec(w_uq_p.shape), _const_spec(w_ukv_p.shape),
                  _const_spec((1, Q_LORA)), _const_spec((1, KV_LORA))],
        out_specs=tuple(row(w) for w, _ in shapes),
        compiler_params=_cparams(("parallel",)),
        name="projections",
    )(x2, cos, sin, w_in_p, w_uq_p, w_ukv_p, g_q.reshape(1, -1), g_kv.reshape(1, -1))


SUBLANES = 8


def _sublane_tile_max(mv, s):
    parts = [mv] + [s[r * SUBLANES:(r + 1) * SUBLANES, :] for r in range(s.shape[0] // SUBLANES)]
    while len(parts) > 1:
        parts = [jnp.maximum(a, b) for a, b in zip(parts[::2], parts[1::2])] + ([parts[-1]] if len(parts) % 2 else [])
    return parts[0]


def _transpose_bf16(x):
    return x.astype(F32).T.astype(BF16)


def _online_softmax_chunk(logits, values, m_sc, acc_sc, p_sc, al_sc):
    for h, logit in enumerate(logits):
        s = logit()
        m_old = m_sc[h]
        mx = _sublane_tile_max(s[:SUBLANES, :], s[SUBLANES:, :])
        m_new = jnp.maximum(m_old, jnp.max(mx, axis=0, keepdims=True))
        p_sc[h] = jnp.exp2(s - m_new).astype(BF16)
        al_sc[h] = jnp.exp2(m_old - m_new)
        m_sc[h] = m_new
    for h, value in enumerate(values):
        pv = jnp.dot(_transpose_bf16(value()), p_sc[h], preferred_element_type=F32)
        acc_sc[h] = al_sc[h] * acc_sc[h] + pv


def _normalise_t(acc_t):
    return acc_t / acc_t[ONE_LANE:ONE_LANE + 1, :]


def _mla_attn_kernel(q_ref, k_ref, v_ref, o_ref, qt_sc, m_sc, acc_sc, p_sc, al_sc, *, tq):
    qi = pl.program_id(2)
    hg = qt_sc.shape[0]
    slabs = [slice(j * LANES, (j + 1) * LANES) for j in range(hg)]
    for j, sl in enumerate(slabs):
        qt_sc[j] = _transpose_bf16(q_ref[:, sl])
    m_sc[...] = jnp.full(m_sc.shape, NEG, F32)
    acc_sc[...] = jnp.zeros(acc_sc.shape, F32)

    def chunk(c, masked):
        off = pl.multiple_of(c * tq, tq)

        def logit(j):
            s = jnp.dot(k_ref[pl.ds(off, tq), slabs[j]], qt_sc[j], preferred_element_type=F32)
            if masked:
                krow = lax.broadcasted_iota(jnp.int32, (tq, tq), 0)
                qcol = lax.broadcasted_iota(jnp.int32, (tq, tq), 1)
                s = jnp.where(krow <= qcol, s, NEG)
            return s

        _online_softmax_chunk(
            [functools.partial(logit, j) for j in range(hg)],
            [functools.partial(lambda j: v_ref[pl.ds(off, tq), slabs[j]], j) for j in range(hg)],
            m_sc, acc_sc, p_sc, al_sc)

    def body(c, carry):
        chunk(c, False)
        return carry

    lax.fori_loop(0, qi, body, 0)
    chunk(qi, True)
    for j, sl in enumerate(slabs):
        o_ref[:, sl] = _normalise_t(acc_sc[j]).T.astype(o_ref.dtype)


def _mla_attention(q, k, v):
    B, S, W = q.shape
    tq, hg = 512, 4
    blk_q = pl.BlockSpec((None, tq, hg * LANES), lambda b, h, i: (b, i, h))
    blk_kv = pl.BlockSpec((None, S, hg * LANES), lambda b, h, i: (b, 0, h))
    return pl.pallas_call(
        functools.partial(_mla_attn_kernel, tq=tq),
        out_shape=jax.ShapeDtypeStruct((B, S, W), BF16),
        grid=(B, W // (hg * LANES), S // tq),
        in_specs=[blk_q, blk_kv, blk_kv],
        out_specs=blk_q,
        scratch_shapes=[pltpu.VMEM((hg, LANES, tq), BF16),
                        pltpu.VMEM((hg, 1, tq), F32),
                        pltpu.VMEM((hg, LANES, tq), F32),
                        pltpu.VMEM((hg, tq, tq), BF16),
                        pltpu.VMEM((hg, 1, tq), F32)],
        compiler_params=_cparams(("parallel", "parallel", "arbitrary")),
        name="mla_attention",
    )(q, k, v)


def _dsa_kernel(dq_ref, dk_ref, dv_ref, iq_ref, ik_ref, iw_ref, o_ref,
                key_sc, hi_sc, bias_sc, wt_sc, qmt_sc, qt_sc, m_sc, acc_sc, p_sc, al_sc,
                *, tq, ck, n_sel):
    qi = pl.program_id(1)
    nch = (qi * tq + tq + ck - 1) // ck
    lane = lax.broadcasted_iota(jnp.int32, (1, LANES), 1)
    k_sel = float(n_sel)

    wt_sc[...] = iw_ref[...].T
    for h in range(IDX_HEADS):
        g, lo = divmod(h * IDX_DIM, LANES)
        qg = iq_ref[:, g * LANES:(g + 1) * LANES].astype(F32)
        in_head = (lane >> 5) == (lo >> 5)
        qmt_sc[:, h * tq:(h + 1) * tq] = jnp.where(in_head, qg, 0.0).T.astype(BF16)

    kw = 2 * LANES
    qpos_t = qi * tq + lax.broadcasted_iota(jnp.int32, (kw, tq), 1)
    krow = lax.broadcasted_iota(jnp.int32, (kw, tq), 0)

    def score_body(c, carry):
        for part in range(ck // kw):
            ks = ik_ref[pl.ds(pl.multiple_of(c * ck + part * kw, kw), kw), :]
            d = jnp.dot(ks, qmt_sc[...], preferred_element_type=F32)
            score = jnp.zeros((kw, tq), F32)
            for h in range(IDX_HEADS):
                score = score + wt_sc[h:h + 1, :] * jnp.maximum(d[:, h * tq:(h + 1) * tq], 0.0)
            score = jnp.where(jnp.abs(score) < F32_MIN_NORMAL, 0.0, score)
            bits = lax.bitcast_convert_type(score, jnp.int32)
            key = jnp.where(bits >= 0, bits, bits ^ np.int32(0x7FFFFFFF))
            causal = c * ck + part * kw + krow <= qpos_t
            key_sc[c, part * kw:(part + 1) * kw, :] = jnp.where(causal, key, INT_MIN)
            top = lax.bitcast_convert_type(bits & np.int32(-65536), F32)
            hi_sc[c, part * kw:(part + 1) * kw, :] = jnp.where(causal, top, jnp.nan).astype(BF16)
        return carry

    lax.fori_loop(0, nch, score_body, 0)

    sub = 8

    n_acc = 4

    def count_ge(thr):
        thr_t = jnp.broadcast_to(thr, (sub, tq))

        def body(c, accs):
            accs = list(accs)
            for r in range(ck // sub):
                ind = jnp.where(key_sc[c, r * sub:(r + 1) * sub, :] >= thr_t, 1.0, 0.0)
                accs[r % n_acc] = accs[r % n_acc] + ind
            return tuple(accs)

        accs = lax.fori_loop(0, nch, body, (jnp.zeros((sub, tq), F32),) * n_acc)
        return jnp.sum(sum(accs[1:], accs[0]), axis=0, keepdims=True)

    pack = 16

    def count_top_ge(top16):
        top16 = jnp.where(top16 > 0, jnp.maximum(top16, np.int32(0x0080)), top16)
        fbits = jnp.where(top16 >= 0, top16, top16 ^ np.int32(0x7FFF)) << 16
        thr_t = jnp.broadcast_to(lax.bitcast_convert_type(fbits, F32).astype(BF16), (pack, tq))
        one, zero = jnp.ones((pack, tq), BF16), jnp.zeros((pack, tq), BF16)

        def body(c, accs):
            accs = list(accs)
            for r in range(ck // pack):
                ind = jnp.where(hi_sc[c, r * pack:(r + 1) * pack, :] >= thr_t, one, zero)
                accs[r % n_acc] = accs[r % n_acc] + ind
            return tuple(accs)

        accs = lax.fori_loop(0, nch, body, (zero,) * n_acc)
        return jnp.sum(sum((a.astype(F32) for a in accs[1:]), accs[0].astype(F32)), axis=0, keepdims=True)

    n_all = (nch * ck).astype(F32)
    n_nonneg = count_top_ge(jnp.zeros((1, tq), jnp.int32))
    top0 = jnp.where(n_nonneg >= k_sel, np.int32(0), np.int32(-32768))
    n_ge0 = jnp.where(n_nonneg >= k_sel, n_nonneg, n_all)

    def search(count, first_bit, n_bits, start):
        def step(i, carry):
            tau, n_ge = carry
            cand = tau | (jnp.int32(1) << (first_bit - i))
            cnt = count(cand)
            take = cnt >= k_sel
            return jnp.where(take, cand, tau), jnp.where(take, cnt, n_ge)
        return lax.fori_loop(0, n_bits, step, start)

    top, n_ge = search(count_top_ge, 14, 15, (top0, n_ge0))
    tau, n_ge = search(count_ge, 15, 16, (top << 16, n_ge))

    excess = n_ge - k_sel
    qpos_c = qi * tq + lax.broadcasted_iota(jnp.int32, (ck, tq), 1)
    krow_c = lax.broadcasted_iota(jnp.int32, (ck, tq), 0)
    tri = jnp.where(lax.broadcasted_iota(jnp.int32, (ck, ck), 0) <= lax.broadcasted_iota(jnp.int32, (ck, ck), 1),
                    1.0, 0.0).astype(BF16)

    def bias_body(i, ties_after):
        c = nch - 1 - i
        blk = key_sc[c]
        is_tie = blk == tau
        suffix = jnp.dot(tri, jnp.where(is_tie, 1.0, 0.0).astype(BF16), preferred_element_type=F32)
        tie = jnp.where(is_tie, jnp.where(ties_after + suffix > excess, 0.0, NEG), NEG)
        sel = jnp.where(blk > tau, 0.0, tie)
        bias_sc[c] = jnp.where(c * ck + krow_c <= qpos_c, sel, NEG)
        return ties_after + suffix[0:1, :]

    lax.fori_loop(0, nch, bias_body, jnp.zeros((1, tq), F32))

    for h in range(DSA_HEADS):
        g, lo = divmod(h * DSA_HEAD_DIM, LANES)
        qt = dq_ref[:, g * LANES:(g + 1) * LANES].astype(F32).T
        dim = lax.broadcasted_iota(jnp.int32, (LANES, tq), 0)
        qt_sc[h] = jnp.where((dim >> 6) == (lo >> 6), qt, 0.0).astype(BF16)
    m_sc[...] = jnp.full(m_sc.shape, NEG, F32)
    acc_sc[...] = jnp.zeros(acc_sc.shape, F32)

    def attn_body(c, carry):
        off = pl.multiple_of(c * ck, ck)
        bias = bias_sc[c]

        def logit(h):
            g = h * DSA_HEAD_DIM // LANES
            k_slab = dk_ref[pl.ds(off, ck), g * LANES:(g + 1) * LANES]
            return jnp.dot(k_slab, qt_sc[h], preferred_element_type=F32) + bias

        _online_softmax_chunk(
            [functools.partial(logit, h) for h in range(DSA_HEADS)],
            [functools.partial(lambda h: dv_ref[pl.ds(off, ck), h * LANES:(h + 1) * LANES], h) for h in range(DSA_HEADS)],
            m_sc, acc_sc, p_sc, al_sc)
        return carry

    lax.fori_loop(0, nch, attn_body, 0)
    for h in range(DSA_HEADS):
        o_ref[:, h * LANES:(h + 1) * LANES] = _normalise_t(acc_sc[h]).T.astype(o_ref.dtype)


def _sparse_attention(dq, dk, dv, iq, ik, iw):
    B, S, W = dv.shape
    wqk = dq.shape[-1]
    tq, ck = 256, 512
    n_sel = min(TOPK_MAX, S // 4)
    assert S & (S - 1) == 0 and S % ck == 0
    blk_q = lambda w: pl.BlockSpec((None, tq, w), lambda b, i: (b, i, 0))
    blk_k = lambda w: pl.BlockSpec((None, S, w), lambda b, i: (b, 0, 0), pipeline_mode=pl.Buffered(1))
    return pl.pallas_call(
        functools.partial(_dsa_kernel, tq=tq, ck=ck, n_sel=n_sel),
        out_shape=jax.ShapeDtypeStruct((B, S, W), BF16),
        grid=(B, S // tq),
        in_specs=[blk_q(wqk), blk_k(wqk), blk_k(W), blk_q(iq.shape[-1]), blk_k(LANES), blk_q(LANES)],
        out_specs=blk_q(W),
        scratch_shapes=[
            pltpu.VMEM((S // ck, ck, tq), jnp.int32),
            pltpu.VMEM((S // ck, ck, tq), BF16),
            pltpu.VMEM((S // ck, ck, tq), F32),
            pltpu.VMEM((LANES, tq), F32),
            pltpu.VMEM((LANES, IDX_HEADS * tq), BF16),
            pltpu.VMEM((DSA_HEADS, LANES, tq), BF16),
            pltpu.VMEM((DSA_HEADS, 1, tq), F32),
            pltpu.VMEM((DSA_HEADS, LANES, tq), F32),
            pltpu.VMEM((DSA_HEADS, ck, tq), BF16),
            pltpu.VMEM((DSA_HEADS, 1, tq), F32)],
        compiler_params=_cparams(("parallel", "arbitrary")),
        name="sparse_attention",
    )(dq, dk, dv, iq, ik, iw)


def _merge_kernel(x_ref, oa_ref, ob_ref, woa_ref, wob_ref, wg_ref, wout_ref, g_ref, b_ref, y_ref, *, alpha):
    x = x_ref[...]
    xb = x.astype(BF16)
    d = x.shape[-1]
    ya = jnp.dot(oa_ref[...], woa_ref[...], preferred_element_type=F32)
    yb = jnp.dot(ob_ref[...], wob_ref[...], preferred_element_type=F32)
    ga = jax.nn.sigmoid(jnp.dot(xb, wg_ref[:, :d], preferred_element_type=F32))
    gb = jax.nn.sigmoid(jnp.dot(xb, wg_ref[:, d:], preferred_element_type=F32))
    merged = ga * ya + gb * yb
    h = jnp.dot(merged.astype(BF16), wout_ref[...], preferred_element_type=F32)
    y_ref[...] = _layer_norm(alpha * x + h, g_ref[...], b_ref[...])


def _pad_head_rows(w, nh, d):
    n = w.shape[1]
    return jnp.pad(w.reshape(nh, d, n), ((0, 0), (0, LANES - d), (0, 0))).reshape(nh * LANES, n)


def _merge(x2, oa, ob, woa, wob, wg, wout, g, b, alpha):
    T, D = x2.shape
    tm = 256
    row = lambda w: pl.BlockSpec((tm, w), lambda i: (i, 0))
    return pl.pallas_call(
        functools.partial(_merge_kernel, alpha=alpha),
        out_shape=jax.ShapeDtypeStruct((T, D), F32),
        grid=(T // tm,),
        in_specs=[row(D), row(oa.shape[1]), row(ob.shape[1]),
                  _const_spec(woa.shape), _const_spec(wob.shape), _const_spec(wg.shape), _const_spec(wout.shape),
                  _const_spec((1, D)), _const_spec((1, D))],
        out_specs=row(D),
        compiler_params=_cparams(("parallel",)),
        name="merge_outproj_ln",
    )(x2, oa, ob, woa, wob, wg, wout, g.reshape(1, D), b.reshape(1, D))


def _ffn_kernel(x_ref, w1_ref, w3_ref, w2_ref, g_ref, b_ref, y_ref, xb_sc, acc_sc, *, alpha):
    f = pl.program_id(1)

    @pl.when(f == 0)
    def _():
        xb_sc[...] = x_ref[...].astype(BF16)
        acc_sc[...] = jnp.zeros(acc_sc.shape, F32)

    xb = xb_sc[...]
    a = jax.nn.silu(jnp.dot(xb, w1_ref[...], preferred_element_type=F32))
    a = a * jnp.dot(xb, w3_ref[...], preferred_element_type=F32)
    acc_sc[...] += jnp.dot(a.astype(BF16), w2_ref[...], preferred_element_type=F32)

    @pl.when(f == pl.num_programs(1) - 1)
    def _():
        y_ref[...] = _layer_norm(alpha * x_ref[...] + acc_sc[...], g_ref[...], b_ref[...])


def _dense_ffn(x2, w1, w3, w2, g, b, alpha):
    T, D = x2.shape
    FF = w1.shape[1]
    tm, tf = 1024, 256
    assert FF % tf == 0
    return pl.pallas_call(
        functools.partial(_ffn_kernel, alpha=alpha),
        out_shape=jax.ShapeDtypeStruct((T, D), F32),
        grid=(T // tm, FF // tf),
        in_specs=[pl.BlockSpec((tm, D), lambda i, f: (i, 0)),
                  pl.BlockSpec((D, tf), lambda i, f: (0, f)),
                  pl.BlockSpec((D, tf), lambda i, f: (0, f)),
                  pl.BlockSpec((tf, D), lambda i, f: (f, 0)),
                  _const_spec((1, D)), _const_spec((1, D))],
        out_specs=pl.BlockSpec((tm, D), lambda i, f: (i, 0)),
        scratch_shapes=[pltpu.VMEM((tm, D), BF16), pltpu.VMEM((tm, D), F32)],
        compiler_params=_cparams(("parallel", "arbitrary")),
        name="dense_ffn_ln",
    )(x2, w1.astype(BF16), w3.astype(BF16), w2.astype(BF16), g.reshape(1, D), b.reshape(1, D))


def _router_kernel(x_ref, wr_ref, route_ref):
    logits = jnp.dot(x_ref[...], wr_ref[...], preferred_element_type=F32, precision=lax.Precision.HIGHEST)
    lane = lax.broadcasted_iota(jnp.int32, logits.shape, 1).astype(F32)
    logits = jnp.where(lane < N_EXPERTS, logits, -jnp.inf)
    v1 = jnp.max(logits, axis=1, keepdims=True)
    i1 = jnp.min(jnp.where(logits == v1, lane, float(LANES)), axis=1, keepdims=True)
    rest = jnp.where(lane == i1, -jnp.inf, logits)
    v2 = jnp.max(rest, axis=1, keepdims=True)
    i2 = jnp.min(jnp.where(rest == v2, lane, float(LANES)), axis=1, keepdims=True)
    e2 = jnp.exp(v2 - v1)
    p1 = 1.0 / (1.0 + e2)
    route_ref[...] = jnp.where(lane == 0.0, i1, jnp.where(lane == 1.0, i2, jnp.where(lane == 2.0, p1, e2 * p1)))


def _router(x2, w_router):
    T, D = x2.shape
    tm = 512
    wr = jnp.pad(w_router, ((0, 0), (0, LANES - N_EXPERTS)))
    return pl.pallas_call(
        _router_kernel,
        out_shape=jax.ShapeDtypeStruct((T, LANES), F32),
        grid=(T // tm,),
        in_specs=[pl.BlockSpec((tm, D), lambda i: (i, 0)), _const_spec(wr.shape)],
        out_specs=pl.BlockSpec((tm, LANES), lambda i: (i, 0)),
        compiler_params=_cparams(("parallel",)),
        name="router",
    )(x2, wr)


def _row_gather(idx_ref, base, n_rows, src_hbm, dst_ref, sem):
    unroll = 8

    def issue(i, c):
        for j in range(unroll):
            r = i * unroll + j
            row = idx_ref[base + r]
            pltpu.make_async_copy(src_hbm.at[pl.ds(row, 1), :], dst_ref.at[pl.ds(r, 1), :], sem).start(priority=j % 2)
        return c

    lax.fori_loop(0, n_rows // unroll, issue, 0)
    pltpu.make_async_copy(src_hbm.at[pl.ds(0, n_rows), :], dst_ref, sem).wait()


def _dispatch_kernel(idx_ref, src_hbm, out_ref, sem, *, rows):
    _row_gather(idx_ref, pl.program_id(0) * rows, rows, src_hbm, out_ref, sem)


def _dispatch(tok_of_slot, x2, rows):
    n, D = tok_of_slot.shape[0], x2.shape[1]
    return pl.pallas_call(
        functools.partial(_dispatch_kernel, rows=rows),
        out_shape=jax.ShapeDtypeStruct((n, D), x2.dtype),
        grid_spec=pltpu.PrefetchScalarGridSpec(
            num_scalar_prefetch=1, grid=(n // rows,),
            in_specs=[pl.BlockSpec(memory_space=pl.ANY)],
            out_specs=pl.BlockSpec((rows, D), lambda t, idx: (t, 0)),
            scratch_shapes=[pltpu.SemaphoreType.DMA(())]),
        compiler_params=_cparams(("arbitrary",)),
        name="moe_dispatch",
    )(tok_of_slot, x2)


def _experts_kernel(te_ref, nu_ref, xs_ref, w1_ref, w3_ref, w2_ref, ys_ref, *, tf):
    t = pl.program_id(0)

    @pl.when(t < nu_ref[0])
    def _():
        xb = xs_ref[...].astype(BF16)
        acc = jnp.zeros(ys_ref.shape, F32)
        for f in range(w1_ref.shape[1] // tf):
            cols = slice(f * tf, (f + 1) * tf)
            a = jax.nn.silu(jnp.dot(xb, w1_ref[:, cols], preferred_element_type=F32))
            a = a * jnp.dot(xb, w3_ref[:, cols], preferred_element_type=F32)
            acc = acc + jnp.dot(a.astype(BF16), w2_ref[cols, :], preferred_element_type=F32)
        ys_ref[...] = acc

    @pl.when(t >= nu_ref[0])
    def _():
        ys_ref[...] = jnp.zeros(ys_ref.shape, F32)


def _experts(tile_expert, n_used, xs, w1, w3, w2, rows):
    P, D = xs.shape
    E, _, FF = w1.shape
    resident = lambda shape: pl.BlockSpec((None,) + shape, lambda t, te, nu: (te[t], 0, 0),
                                          pipeline_mode=pl.Buffered(1))
    return pl.pallas_call(
        functools.partial(_experts_kernel, tf=512),
        out_shape=jax.ShapeDtypeStruct((P, D), F32),
        grid_spec=pltpu.PrefetchScalarGridSpec(
            num_scalar_prefetch=2, grid=(P // rows,),
            in_specs=[pl.BlockSpec((rows, D), lambda t, te, nu: (t, 0)),
                      resident((D, FF)), resident((D, FF)), resident((FF, D))],
            out_specs=pl.BlockSpec((rows, D), lambda t, te, nu: (t, 0))),
        compiler_params=_cparams(("arbitrary",)),
        name="moe_experts",
    )(tile_expert, n_used, xs, w1, w3, w2)


def _combine_kernel(slot_ref, x_ref, route_ref, g_ref, b_ref, ys_hbm, y_ref, buf, sem, *, tm, alpha):
    _row_gather(slot_ref, pl.program_id(0) * 2 * tm, 2 * tm, ys_hbm, buf, sem)
    p1, p2 = route_ref[:, 2:3], route_ref[:, 3:4]
    f = p1 * buf[:tm, :] + p2 * buf[tm:, :]
    y_ref[...] = _layer_norm(alpha * x_ref[...] + f, g_ref[...], b_ref[...])


def _combine(slots, ys, x2, route, g, b, alpha, tm):
    T, D = x2.shape
    return pl.pallas_call(
        functools.partial(_combine_kernel, tm=tm, alpha=alpha),
        out_shape=jax.ShapeDtypeStruct((T, D), F32),
        grid_spec=pltpu.PrefetchScalarGridSpec(
            num_scalar_prefetch=1, grid=(T // tm,),
            in_specs=[pl.BlockSpec((tm, D), lambda i, s: (i, 0)),
                      pl.BlockSpec((tm, LANES), lambda i, s: (i, 0)),
                      pl.BlockSpec((1, D), lambda i, s: (0, 0)), pl.BlockSpec((1, D), lambda i, s: (0, 0)),
                      pl.BlockSpec(memory_space=pl.ANY)],
            out_specs=pl.BlockSpec((tm, D), lambda i, s: (i, 0)),
            scratch_shapes=[pltpu.VMEM((2 * tm, D), F32), pltpu.SemaphoreType.DMA(())]),
        compiler_params=_cparams(("arbitrary",)),
        name="moe_combine_ln",
    )(slots, x2, route, g.reshape(1, D), b.reshape(1, D), ys)


def _moe_ffn(x2, w_router, w1, w3, w2, g, b, alpha):
    T, D = x2.shape
    E = w1.shape[0]
    rows, tm = 256, 512
    gather_rows = 4 * rows
    route = _router(x2, w_router)

    e_flat = jnp.concatenate([route[:, 0], route[:, 1]]).astype(jnp.int32)
    onehot = (e_flat[:, None] == jnp.arange(E, dtype=jnp.int32)[None, :]).astype(jnp.int32)
    csum = jnp.cumsum(onehot, axis=0)
    counts = csum[-1]
    padded = (counts + rows - 1) // rows * rows
    ends = jnp.cumsum(padded)
    slot = jnp.sum(onehot * (csum - 1 + (ends - padded)[None, :]), axis=1)
    n_slots = 2 * T + E * rows
    tok_of_slot = jnp.zeros((n_slots,), jnp.int32).at[slot].set(jnp.arange(2 * T, dtype=jnp.int32) % T)
    tile_start = jnp.arange(n_slots // rows, dtype=jnp.int32) * rows
    tile_expert = jnp.minimum(jnp.sum((tile_start[:, None] >= ends[None, :]).astype(jnp.int32), axis=1), E - 1)
    n_used = (ends[-1:] // rows).astype(jnp.int32)

    assert n_slots % gather_rows == 0
    xs = _dispatch(tok_of_slot, x2, gather_rows)
    ys = _experts(tile_expert, n_used, xs, w1.astype(BF16), w3.astype(BF16), w2.astype(BF16), rows)
    slots = jnp.concatenate([slot[:T].reshape(T // tm, tm), slot[T:].reshape(T // tm, tm)], axis=1).reshape(-1)
    return _combine(slots, ys, x2, route, g, b, alpha, tm)


def kernel(x, positions, w_in, mla_q_norm, w_uq, mla_kv_norm, w_ukv, w_o_mla, w_o_dsa, w_out,
           ln1_g, ln1_b, ln2_g, ln2_b, dense_w1, dense_w3, dense_w2,
           moe_router, moe_w1, moe_w3, moe_w2):
    B, S, D = x.shape
    depth = w_in.shape[0]
    alpha = (2 * depth) ** 0.25
    T = B * S
    cos, sin = _rope_tables(positions)
    x2 = x.reshape(T, D)
    for l in range(depth):
        w_in_p, w_uq_p, w_ukv_p, w_gates = _prep_in_weights(w_in[l], w_uq[l], w_ukv[l])
        qm, km, vm, dq, dk, dv, iq, ik, iw = _projections(
            x2, cos, sin, w_in_p, w_uq_p, w_ukv_p, mla_q_norm[l], mla_kv_norm[l])
        b3 = lambda a: a.reshape(B, S, a.shape[-1])
        o_a = _mla_attention(b3(qm), b3(km), b3(vm)).reshape(T, -1)
        o_b = _sparse_attention(b3(dq), b3(dk), b3(dv), b3(iq), b3(ik), b3(iw)).reshape(T, -1)
        x2 = _merge(x2, o_a, o_b,
                    _pad_head_rows(w_o_mla[l], MLA_HEADS, MLA_V).astype(BF16),
                    _pad_head_rows(w_o_dsa[l], DSA_HEADS, DSA_HEAD_DIM).astype(BF16),
                    w_gates, w_out[l].astype(BF16), ln1_g[l], ln1_b[l], alpha)
        if l % 2 == 0:
            x2 = _dense_ffn(x2, dense_w1[l // 2], dense_w3[l // 2], dense_w2[l // 2], ln2_g[l], ln2_b[l], alpha)
        else:
            x2 = _moe_ffn(x2, moe_router[l // 2], moe_w1[l // 2], moe_w3[l // 2], moe_w2[l // 2],
                          ln2_g[l], ln2_b[l], alpha)
    return x2.reshape(B, S, D)
```

```python
import functools

import numpy as np
import jax
import jax.numpy as jnp
from jax import lax
from jax.experimental import pallas as pl
from jax.experimental.pallas import tpu as pltpu

F32 = jnp.float32
BF16 = jnp.bfloat16
LANES = 128
VMEM_LIMIT = 56 * 1024 * 1024

MLA_HEADS = 8
MLA_NOPE = 64
MLA_ROPE = 32
MLA_V = 64
Q_LORA = 384
KV_LORA = 256
DSA_HEADS = 8
DSA_HEAD_DIM = 64
IDX_HEADS = 8
IDX_DIM = 32
TOPK_MAX = 256
N_EXPERTS = 8
ROPE_THETA = 10000.0
NORM_EPS = 1e-5
NEG = -1e30
LOG2E = 1.4426950408889634
F32_MIN_NORMAL = 2.0 ** -126
INT_MIN = np.int32(-2 ** 31)
ONE_LANE = 64
assert ONE_LANE >= MLA_V and ONE_LANE >= DSA_HEAD_DIM

C_CQ = 0
C_CKV = C_CQ + Q_LORA
C_KR = C_CKV + KV_LORA
C_DQ = C_KR + LANES
C_DK = C_DQ + DSA_HEADS * DSA_HEAD_DIM
C_DV = C_DK + DSA_HEADS * DSA_HEAD_DIM
C_IQ = C_DV + DSA_HEADS * LANES
C_IK = C_IQ + IDX_HEADS * IDX_DIM
C_IW = C_IK + LANES
C_END = C_IW + LANES


def _cparams(sem):
    return pltpu.CompilerParams(dimension_semantics=sem, vmem_limit_bytes=VMEM_LIMIT)


def _const_spec(shape):
    nd = len(shape)
    return pl.BlockSpec(shape, lambda *_: (0,) * nd)


def _layer_norm(z, g, b):
    mu = jnp.mean(z, axis=-1, keepdims=True)
    zc = z - mu
    var = jnp.mean(zc * zc, axis=-1, keepdims=True)
    return zc * lax.rsqrt(var + NORM_EPS) * g + b


def _rope_tables_kernel(pos_ref, inv_ref, sgn_ref, cos_ref, sin_ref):
    pos = pos_ref[...].astype(F32)
    for p in range(2):
        ang = pos * inv_ref[p:p + 1, :]
        cos_ref[p] = jnp.cos(ang)
        sin_ref[p] = jnp.sin(ang) * sgn_ref[p:p + 1, :]


def _rope_tables(positions):
    T = positions.size
    tm = 1024
    inv32 = jnp.power(ROPE_THETA, -jnp.arange(0, IDX_DIM, 2, dtype=F32) / IDX_DIM)
    inv64 = jnp.power(ROPE_THETA, -jnp.arange(0, DSA_HEAD_DIM, 2, dtype=F32) / DSA_HEAD_DIM)
    inv = jnp.stack([jnp.tile(inv32, LANES // 16), jnp.tile(inv64, LANES // 32)])
    lane = np.arange(LANES)
    sgn = jnp.asarray(np.stack([np.where(lane % 32 < 16, -1.0, 1.0), np.where(lane % 64 < 32, -1.0, 1.0)]), F32)
    out = jax.ShapeDtypeStruct((2, T, LANES), F32)
    return pl.pallas_call(
        _rope_tables_kernel,
        out_shape=(out, out),
        grid=(T // tm,),
        in_specs=[pl.BlockSpec((tm, 1), lambda i: (i, 0)), _const_spec((2, LANES)), _const_spec((2, LANES))],
        out_specs=(pl.BlockSpec((2, tm, LANES), lambda i: (0, i, 0)),) * 2,
        compiler_params=_cparams(("parallel",)),
        name="rope_tables",
    )(positions.reshape(T, 1), inv, sgn)


def _proj_kernel(x_ref, cos_ref, sin_ref, w_in_ref, w_uq_ref, w_ukv_ref, gq_ref, gkv_ref,
                 qm_ref, km_ref, vm_ref, dq_ref, dk_ref, dv_ref, iq_ref, ik_ref, iw_ref, *, mla_scale):
    xb = x_ref[...].astype(BF16)
    lane = lax.broadcasted_iota(jnp.int32, (1, LANES), 1)
    cos32, sin32 = cos_ref[0], sin_ref[0]
    cos64, sin64 = cos_ref[1], sin_ref[1]
    in_rope = (lane >> 5) == (MLA_NOPE >> 5)
    cos_m = jnp.where(in_rope, cos32, 1.0)
    sin_m = jnp.where(in_rope, sin32, 0.0)
    first32 = (lane & (IDX_DIM - 1)) < IDX_DIM // 2
    first64 = (lane & (DSA_HEAD_DIM - 1)) < DSA_HEAD_DIM // 2
    wide_lane = lax.broadcasted_iota(jnp.int32, (1, MLA_HEADS * LANES), 1)
    one_col = jnp.where((wide_lane & (LANES - 1)) == ONE_LANE, 1.0, 0.0)

    def proj(lo, hi):
        return jnp.dot(xb, w_in_ref[:, lo:hi], preferred_element_type=F32)

    def rope(xs, cos, sin, half, first):
        rot = jnp.where(first, pltpu.roll(xs, LANES - half, 1), pltpu.roll(xs, half, 1))
        return xs * cos + rot * sin

    def rms(c, g):
        ms = jnp.mean(c * c, axis=-1, keepdims=True)
        return c * lax.rsqrt(ms + NORM_EPS) * g

    q = jnp.dot(rms(proj(C_CQ, C_CKV), gq_ref[...]).astype(BF16), w_uq_ref[...], preferred_element_type=F32)
    kv = jnp.dot(rms(proj(C_CKV, C_KR), gkv_ref[...]).astype(BF16), w_ukv_ref[...], preferred_element_type=F32)
    kr = rope(proj(C_KR, C_DQ), cos_m, sin_m, MLA_ROPE // 2, first32)
    for h in range(MLA_HEADS):
        sl = slice(h * LANES, (h + 1) * LANES)
        qm_ref[:, sl] = (rope(q[:, sl], cos_m, sin_m, MLA_ROPE // 2, first32) * mla_scale).astype(BF16)
        km_ref[:, sl] = (kv[:, sl] + kr).astype(BF16)
    vm_ref[...] = (kv[:, MLA_HEADS * LANES:] + one_col).astype(BF16)

    dq = proj(C_DQ, C_DK)
    dk = proj(C_DK, C_DV)
    for g in range(DSA_HEADS * DSA_HEAD_DIM // LANES):
        sl = slice(g * LANES, (g + 1) * LANES)
        dq_ref[:, sl] = (rope(dq[:, sl], cos64, sin64, DSA_HEAD_DIM // 2, first64) * LOG2E).astype(BF16)
        dk_ref[:, sl] = rope(dk[:, sl], cos64, sin64, DSA_HEAD_DIM // 2, first64).astype(BF16)
    dv_ref[...] = (proj(C_DV, C_IQ) + one_col).astype(BF16)

    iq = proj(C_IQ, C_IK)
    for g in range(IDX_HEADS * IDX_DIM // LANES):
        sl = slice(g * LANES, (g + 1) * LANES)
        iq_ref[:, sl] = rope(iq[:, sl], cos32, sin32, IDX_DIM // 2, first32).astype(BF16)
    ik_ref[...] = rope(proj(C_IK, C_IW), cos32, sin32, IDX_DIM // 2, first32).astype(BF16)
    iw_ref[...] = proj(C_IW, C_END)


def _pad_heads(w, nh, d):
    k = w.shape[0]
    return jnp.pad(w.reshape(k, nh, d), ((0, 0), (0, 0), (0, LANES - d))).reshape(k, nh * LANES)


def _prep_in_weights(w_in, w_uq, w_ukv):
    d = w_in.shape[0]
    o = np.cumsum([0, Q_LORA, KV_LORA, MLA_ROPE, 512, 512, 512, IDX_HEADS * IDX_DIM, IDX_DIM, IDX_HEADS])
    c_q, c_kv, k_r, dq, dk, dv, iq, ik, iw = [w_in[:, o[i]:o[i + 1]] for i in range(9)]
    gates = w_in[:, o[9]:]
    kr_tile = jnp.pad(k_r, ((0, 0), (MLA_NOPE, LANES - MLA_NOPE - MLA_ROPE)))
    dsa_scale = DSA_HEAD_DIM ** -0.5
    idx_scale = (IDX_HEADS * IDX_DIM) ** -0.5
    w_in_p = jnp.concatenate([
        c_q, c_kv, kr_tile,
        dq * dsa_scale, dk,
        _pad_heads(dv, DSA_HEADS, DSA_HEAD_DIM),
        iq, jnp.tile(ik, (1, LANES // IDX_DIM)),
        jnp.pad(iw * idx_scale, ((0, 0), (0, LANES - IDX_HEADS))),
    ], axis=1).astype(BF16)
    assert w_in_p.shape == (d, C_END)
    w_uq_p = _pad_heads(w_uq, MLA_HEADS, MLA_NOPE + MLA_ROPE).astype(BF16)
    ukv = w_ukv.reshape(KV_LORA, MLA_HEADS, MLA_NOPE + MLA_V)
    w_ukv_p = jnp.concatenate([
        _pad_heads(ukv[:, :, :MLA_NOPE].reshape(KV_LORA, -1), MLA_HEADS, MLA_NOPE),
        _pad_heads(ukv[:, :, MLA_NOPE:].reshape(KV_LORA, -1), MLA_HEADS, MLA_V),
    ], axis=1).astype(BF16)
    return w_in_p, w_uq_p, w_ukv_p, gates.astype(BF16)


def _projections(x2, cos, sin, w_in_p, w_uq_p, w_ukv_p, g_q, g_kv):
    T, D = x2.shape
    tm = 256
    wide = MLA_HEADS * LANES
    row = lambda w: pl.BlockSpec((tm, w), lambda i: (i, 0))
    packed = DSA_HEADS * DSA_HEAD_DIM
    shapes = [(wide, BF16)] * 3 + [(packed, BF16), (packed, BF16), (wide, BF16),
                                   (IDX_HEADS * IDX_DIM, BF16), (LANES, BF16), (LANES, F32)]
    return pl.pallas_call(
        functools.partial(_proj_kernel, mla_scale=(MLA_NOPE + MLA_ROPE) ** -0.5 * LOG2E),
        out_shape=tuple(jax.ShapeDtypeStruct((T, w), dt) for w, dt in shapes),
        grid=(T // tm,),
        in_specs=[row(D),
                  pl.BlockSpec((2, tm, LANES), lambda i: (0, i, 0)),
                  pl.BlockSpec((2, tm, LANES), lambda i: (0, i, 0)),
                  _const_spec(w_in_p.shape), _const_spec(w_uq_p.shape), _const_spec(w_ukv_p.shape),
                  _const_spec((1, Q_LORA)), _const_spec((1, KV_LORA))],
        out_specs=tuple(row(w) for w, _ in shapes),
        compiler_params=_cparams(("parallel",)),
        name="projections",
    )(x2, cos, sin, w_in_p, w_uq_p, w_ukv_p, g_q.reshape(1, -1), g_kv.reshape(1, -1))


SUBLANES = 8


def _sublane_tile_max(mv, s):
    parts = [mv] + [s[r * SUBLANES:(r + 1) * SUBLANES, :] for r in range(s.shape[0] // SUBLANES)]
    while len(parts) > 1:
        parts = [jnp.maximum(a, b) for a, b in zip(parts[::2], parts[1::2])] + ([parts[-1]] if len(parts) % 2 else [])
    return parts[0]


def _transpose_bf16(x):
    return x.astype(F32).T.astype(BF16)


def _online_softmax_chunk(logits, values, m_sc, acc_sc, p_sc, al_sc):
    for h, logit in enumerate(logits):
        s = logit()
        m_old = m_sc[h]
        mx = _sublane_tile_max(s[:SUBLANES, :], s[SUBLANES:, :])
        m_new = jnp.maximum(m_old, jnp.max(mx, axis=0, keepdims=True))
        p_sc[h] = jnp.exp2(s - m_new).astype(BF16)
        al_sc[h] = jnp.exp2(m_old - m_new)
        m_sc[h] = m_new
    for h, value in enumerate(values):
        pv = jnp.dot(_transpose_bf16(value()), p_sc[h], preferred_element_type=F32)
        acc_sc[h] = al_sc[h] * acc_sc[h] + pv


def _normalise_t(acc_t):
    return acc_t / acc_t[ONE_LANE:ONE_LANE + 1, :]


def _mla_attn_kernel(q_ref, k_ref, v_ref, o_ref, qt_sc, m_sc, acc_sc, p_sc, al_sc, *, tq):
    qi = pl.program_id(2)
    hg = qt_sc.shape[0]
    slabs = [slice(j * LANES, (j + 1) * LANES) for j in range(hg)]
    for j, sl in enumerate(slabs):
        qt_sc[j] = _transpose_bf16(q_ref[:, sl])
    m_sc[...] = jnp.full(m_sc.shape, NEG, F32)
    acc_sc[...] = jnp.zeros(acc_sc.shape, F32)

    def chunk(c, masked):
        off = pl.multiple_of(c * tq, tq)

        def logit(j):
            s = jnp.dot(k_ref[pl.ds(off, tq), slabs[j]], qt_sc[j], preferred_element_type=F32)
            if masked:
                krow = lax.broadcasted_iota(jnp.int32, (tq, tq), 0)
                qcol = lax.broadcasted_iota(jnp.int32, (tq, tq), 1)
                s = jnp.where(krow <= qcol, s, NEG)
            return s

        _online_softmax_chunk(
            [functools.partial(logit, j) for j in range(hg)],
            [functools.partial(lambda j: v_ref[pl.ds(off, tq), slabs[j]], j) for j in range(hg)],
            m_sc, acc_sc, p_sc, al_sc)

    def body(c, carry):
        chunk(c, False)
        return carry

    lax.fori_loop(0, qi, body, 0)
    chunk(qi, True)
    for j, sl in enumerate(slabs):
        o_ref[:, sl] = _normalise_t(acc_sc[j]).T.astype(o_ref.dtype)


def _mla_attention(q, k, v):
    B, S, W = q.shape
    tq, hg = 512, 4
    blk_q = pl.BlockSpec((None, tq, hg * LANES), lambda b, h, i: (b, i, h))
    blk_kv = pl.BlockSpec((None, S, hg * LANES), lambda b, h, i: (b, 0, h))
    return pl.pallas_call(
        functools.partial(_mla_attn_kernel, tq=tq),
        out_shape=jax.ShapeDtypeStruct((B, S, W), BF16),
        grid=(B, W // (hg * LANES), S // tq),
        in_specs=[blk_q, blk_kv, blk_kv],
        out_specs=blk_q,
        scratch_shapes=[pltpu.VMEM((hg, LANES, tq), BF16),
                        pltpu.VMEM((hg, 1, tq), F32),
                        pltpu.VMEM((hg, LANES, tq), F32),
                        pltpu.VMEM((hg, tq, tq), BF16),
                        pltpu.VMEM((hg, 1, tq), F32)],
        compiler_params=_cparams(("parallel", "parallel", "arbitrary")),
        name="mla_attention",
    )(q, k, v)


def _dsa_kernel(dq_ref, dk_ref, dv_ref, iq_ref, ik_ref, iw_ref, o_ref,
                key_sc, bias_sc, wt_sc, qmt_sc, qt_sc, m_sc, acc_sc, p_sc, al_sc,
                *, tq, ck, n_sel):
    qi = pl.program_id(1)
    nch = (qi * tq + tq + ck - 1) // ck
    lane = lax.broadcasted_iota(jnp.int32, (1, LANES), 1)
    k_sel = float(n_sel)

    wt_sc[...] = iw_ref[...].T
    for h in range(IDX_HEADS):
        g, lo = divmod(h * IDX_DIM, LANES)
        qg = iq_ref[:, g * LANES:(g + 1) * LANES].astype(F32)
        in_head = (lane >> 5) == (lo >> 5)
        qmt_sc[:, h * tq:(h + 1) * tq] = jnp.where(in_head, qg, 0.0).T.astype(BF16)

    kw = 2 * LANES
    qpos_t = qi * tq + lax.broadcasted_iota(jnp.int32, (kw, tq), 1)
    krow = lax.broadcasted_iota(jnp.int32, (kw, tq), 0)

    def score_body(c, carry):
        for part in range(ck // kw):
            ks = ik_ref[pl.ds(pl.multiple_of(c * ck + part * kw, kw), kw), :]
            d = jnp.dot(ks, qmt_sc[...], preferred_element_type=F32)
            score = jnp.zeros((kw, tq), F32)
            for h in range(IDX_HEADS):
                score = score + wt_sc[h:h + 1, :] * jnp.maximum(d[:, h * tq:(h + 1) * tq], 0.0)
            score = jnp.where(jnp.abs(score) < F32_MIN_NORMAL, 0.0, score)
            causal = c * ck + part * kw + krow <= qpos_t
            key_sc[c, part * kw:(part + 1) * kw, :] = jnp.where(causal, score, jnp.nan)
        return carry

    lax.fori_loop(0, nch, score_body, 0)

    sub = 8
    n_acc = 4

    def as_float(image):
        image = jnp.where(image > 0, jnp.maximum(image, np.int32(0x00800000)), image)
        return lax.bitcast_convert_type(jnp.where(image >= 0, image, image ^ np.int32(0x7FFFFFFF)), F32)

    def count_ge(image):
        thr_t = jnp.broadcast_to(as_float(image), (sub, tq))

        def body(c, accs):
            accs = list(accs)
            for r in range(ck // sub):
                ind = jnp.where(key_sc[c, r * sub:(r + 1) * sub, :] >= thr_t, 1.0, 0.0)
                accs[r % n_acc] = accs[r % n_acc] + ind
            return tuple(accs)

        accs = lax.fori_loop(0, nch, body, (jnp.zeros((sub, tq), F32),) * n_acc)
        return jnp.sum(sum(accs[1:], accs[0]), axis=0, keepdims=True)

    n_all = (nch * ck).astype(F32)
    n_nonneg = count_ge(jnp.zeros((1, tq), jnp.int32))
    tau0 = jnp.where(n_nonneg >= k_sel, np.int32(0), INT_MIN)
    n_ge0 = jnp.where(n_nonneg >= k_sel, n_nonneg, n_all)

    def tau_step(i, carry):
        tau, n_ge = carry
        cand = tau | (jnp.int32(1) << (30 - i))
        cnt = count_ge(cand)
        take = cnt >= k_sel
        return jnp.where(take, cand, tau), jnp.where(take, cnt, n_ge)

    tau, n_ge = lax.fori_loop(0, 31, tau_step, (tau0, n_ge0))
    open_below = tau == INT_MIN
    tau_f = jnp.where(open_below, -jnp.inf, as_float(tau))
    next_f = jnp.where(open_below, -jnp.inf, as_float(tau + 1))

    excess = n_ge - k_sel
    qpos_c = qi * tq + lax.broadcasted_iota(jnp.int32, (ck, tq), 1)
    krow_c = lax.broadcasted_iota(jnp.int32, (ck, tq), 0)
    tri = jnp.where(lax.broadcasted_iota(jnp.int32, (ck, ck), 0) <= lax.broadcasted_iota(jnp.int32, (ck, ck), 1),
                    1.0, 0.0).astype(BF16)

    def bias_body(i, ties_after):
        c = nch - 1 - i
        blk = key_sc[c]
        above = blk >= next_f
        at_least = blk >= tau_f
        is_tie = jnp.where(above, 0.0, jnp.where(at_least, 1.0, 0.0))
        suffix = jnp.dot(tri, is_tie.astype(BF16), preferred_element_type=F32)
        tie = jnp.where(at_least, jnp.where(ties_after + suffix > excess, 0.0, NEG), NEG)
        sel = jnp.where(above, 0.0, tie)
        bias_sc[c] = jnp.where(c * ck + krow_c <= qpos_c, sel, NEG)
        return ties_after + suffix[0:1, :]

    lax.fori_loop(0, nch, bias_body, jnp.zeros((1, tq), F32))

    for h in range(DSA_HEADS):
        g, lo = divmod(h * DSA_HEAD_DIM, LANES)
        qt = dq_ref[:, g * LANES:(g + 1) * LANES].astype(F32).T
        dim = lax.broadcasted_iota(jnp.int32, (LANES, tq), 0)
        qt_sc[h] = jnp.where((dim >> 6) == (lo >> 6), qt, 0.0).astype(BF16)
    m_sc[...] = jnp.full(m_sc.shape, NEG, F32)
    acc_sc[...] = jnp.zeros(acc_sc.shape, F32)

    def attn_body(c, carry):
        off = pl.multiple_of(c * ck, ck)
        bias = bias_sc[c]

        def logit(h):
            g = h * DSA_HEAD_DIM // LANES
            k_slab = dk_ref[pl.ds(off, ck), g * LANES:(g + 1) * LANES]
            return jnp.dot(k_slab, qt_sc[h], preferred_element_type=F32) + bias

        _online_softmax_chunk(
            [functools.partial(logit, h) for h in range(DSA_HEADS)],
            [functools.partial(lambda h: dv_ref[pl.ds(off, ck), h * LANES:(h + 1) * LANES], h) for h in range(DSA_HEADS)],
            m_sc, acc_sc, p_sc, al_sc)
        return carry

    lax.fori_loop(0, nch, attn_body, 0)
    for h in range(DSA_HEADS):
        o_ref[:, h * LANES:(h + 1) * LANES] = _normalise_t(acc_sc[h]).T.astype(o_ref.dtype)


def _sparse_attention(dq, dk, dv, iq, ik, iw):
    B, S, W = dv.shape
    wqk = dq.shape[-1]
    tq, ck = 256, 512
    n_sel = min(TOPK_MAX, S // 4)
    assert S & (S - 1) == 0 and S % ck == 0
    blk_q = lambda w: pl.BlockSpec((None, tq, w), lambda b, i: (b, i, 0))
    blk_k = lambda w: pl.BlockSpec((None, S, w), lambda b, i: (b, 0, 0), pipeline_mode=pl.Buffered(1))
    return pl.pallas_call(
        functools.partial(_dsa_kernel, tq=tq, ck=ck, n_sel=n_sel),
        out_shape=jax.ShapeDtypeStruct((B, S, W), BF16),
        grid=(B, S // tq),
        in_specs=[blk_q(wqk), blk_k(wqk), blk_k(W), blk_q(iq.shape[-1]), blk_k(LANES), blk_q(LANES)],
        out_specs=blk_q(W),
        scratch_shapes=[
            pltpu.VMEM((S // ck, ck, tq), F32),
            pltpu.VMEM((S // ck, ck, tq), F32),
            pltpu.VMEM((LANES, tq), F32),
            pltpu.VMEM((LANES, IDX_HEADS * tq), BF16),
            pltpu.VMEM((DSA_HEADS, LANES, tq), BF16),
            pltpu.VMEM((DSA_HEADS, 1, tq), F32),
            pltpu.VMEM((DSA_HEADS, LANES, tq), F32),
            pltpu.VMEM((DSA_HEADS, ck, tq), BF16),
            pltpu.VMEM((DSA_HEADS, 1, tq), F32)],
        compiler_params=_cparams(("parallel", "arbitrary")),
        name="sparse_attention",
    )(dq, dk, dv, iq, ik, iw)


def _merge_kernel(x_ref, oa_ref, ob_ref, woa_ref, wob_ref, wg_ref, wout_ref, g_ref, b_ref, y_ref, *, alpha):
    x = x_ref[...]
    xb = x.astype(BF16)
    d = x.shape[-1]
    ya = jnp.dot(oa_ref[...], woa_ref[...], preferred_element_type=F32)
    yb = jnp.dot(ob_ref[...], wob_ref[...], preferred_element_type=F32)
    ga = jax.nn.sigmoid(jnp.dot(xb, wg_ref[:, :d], preferred_element_type=F32))
    gb = jax.nn.sigmoid(jnp.dot(xb, wg_ref[:, d:], preferred_element_type=F32))
    merged = ga * ya + gb * yb
    h = jnp.dot(merged.astype(BF16), wout_ref[...], preferred_element_type=F32)
    y_ref[...] = _layer_norm(alpha * x + h, g_ref[...], b_ref[...])


def _pad_head_rows(w, nh, d):
    n = w.shape[1]
    return jnp.pad(w.reshape(nh, d, n), ((0, 0), (0, LANES - d), (0, 0))).reshape(nh * LANES, n)


def _merge(x2, oa, ob, woa, wob, wg, wout, g, b, alpha):
    T, D = x2.shape
    tm = 256
    row = lambda w: pl.BlockSpec((tm, w), lambda i: (i, 0))
    return pl.pallas_call(
        functools.partial(_merge_kernel, alpha=alpha),
        out_shape=jax.ShapeDtypeStruct((T, D), F32),
        grid=(T // tm,),
        in_specs=[row(D), row(oa.shape[1]), row(ob.shape[1]),
                  _const_spec(woa.shape), _const_spec(wob.shape), _const_spec(wg.shape), _const_spec(wout.shape),
                  _const_spec((1, D)), _const_spec((1, D))],
        out_specs=row(D),
        compiler_params=_cparams(("parallel",)),
        name="merge_outproj_ln",
    )(x2, oa, ob, woa, wob, wg, wout, g.reshape(1, D), b.reshape(1, D))


def _ffn_kernel(x_ref, w1_ref, w3_ref, w2_ref, g_ref, b_ref, y_ref, xb_sc, acc_sc, *, alpha):
    f = pl.program_id(1)

    @pl.when(f == 0)
    def _():
        xb_sc[...] = x_ref[...].astype(BF16)
        acc_sc[...] = jnp.zeros(acc_sc.shape, F32)

    xb = xb_sc[...]
    a = jax.nn.silu(jnp.dot(xb, w1_ref[...], preferred_element_type=F32))
    a = a * jnp.dot(xb, w3_ref[...], preferred_element_type=F32)
    acc_sc[...] += jnp.dot(a.astype(BF16), w2_ref[...], preferred_element_type=F32)

    @pl.when(f == pl.num_programs(1) - 1)
    def _():
        y_ref[...] = _layer_norm(alpha * x_ref[...] + acc_sc[...], g_ref[...], b_ref[...])


def _dense_ffn(x2, w1, w3, w2, g, b, alpha):
    T, D = x2.shape
    FF = w1.shape[1]
    tm, tf = 1024, 256
    assert FF % tf == 0
    return pl.pallas_call(
        functools.partial(_ffn_kernel, alpha=alpha),
        out_shape=jax.ShapeDtypeStruct((T, D), F32),
        grid=(T // tm, FF // tf),
        in_specs=[pl.BlockSpec((tm, D), lambda i, f: (i, 0)),
                  pl.BlockSpec((D, tf), lambda i, f: (0, f)),
                  pl.BlockSpec((D, tf), lambda i, f: (0, f)),
                  pl.BlockSpec((tf, D), lambda i, f: (f, 0)),
                  _const_spec((1, D)), _const_spec((1, D))],
        out_specs=pl.BlockSpec((tm, D), lambda i, f: (i, 0)),
        scratch_shapes=[pltpu.VMEM((tm, D), BF16), pltpu.VMEM((tm, D), F32)],
        compiler_params=_cparams(("parallel", "arbitrary")),
        name="dense_ffn_ln",
    )(x2, w1.astype(BF16), w3.astype(BF16), w2.astype(BF16), g.reshape(1, D), b.reshape(1, D))


def _router_kernel(x_ref, wr_ref, route_ref):
    logits = jnp.dot(x_ref[...], wr_ref[...], preferred_element_type=F32, precision=lax.Precision.HIGHEST)
    lane = lax.broadcasted_iota(jnp.int32, logits.shape, 1).astype(F32)
    logits = jnp.where(lane < N_EXPERTS, logits, -jnp.inf)
    v1 = jnp.max(logits, axis=1, keepdims=True)
    i1 = jnp.min(jnp.where(logits == v1, lane, float(LANES)), axis=1, keepdims=True)
    rest = jnp.where(lane == i1, -jnp.inf, logits)
    v2 = jnp.max(rest, axis=1, keepdims=True)
    i2 = jnp.min(jnp.where(rest == v2, lane, float(LANES)), axis=1, keepdims=True)
    e2 = jnp.exp(v2 - v1)
    p1 = 1.0 / (1.0 + e2)
    route_ref[...] = jnp.where(lane == 0.0, i1, jnp.where(lane == 1.0, i2, jnp.where(lane == 2.0, p1, e2 * p1)))


def _router(x2, w_router):
    T, D = x2.shape
    tm = 512
    wr = jnp.pad(w_router, ((0, 0), (0, LANES - N_EXPERTS)))
    return pl.pallas_call(
        _router_kernel,
        out_shape=jax.ShapeDtypeStruct((T, LANES), F32),
        grid=(T // tm,),
        in_specs=[pl.BlockSpec((tm, D), lambda i: (i, 0)), _const_spec(wr.shape)],
        out_specs=pl.BlockSpec((tm, LANES), lambda i: (i, 0)),
        compiler_params=_cparams(("parallel",)),
        name="router",
    )(x2, wr)


def _row_gather(idx_ref, base, n_rows, src_hbm, dst_ref, sem):
    def issue(r, c):
        row = idx_ref[base + r]
        pltpu.make_async_copy(src_hbm.at[pl.ds(row, 1), :], dst_ref.at[pl.ds(r, 1), :], sem).start()
        return c

    lax.fori_loop(0, n_rows, issue, 0, unroll=8)
    pltpu.make_async_copy(src_hbm.at[pl.ds(0, n_rows), :], dst_ref, sem).wait()


def _dispatch_kernel(idx_ref, src_hbm, out_ref, sem, *, rows):
    _row_gather(idx_ref, pl.program_id(0) * rows, rows, src_hbm, out_ref, sem)


def _dispatch(tok_of_slot, x2, rows):
    n, D = tok_of_slot.shape[0], x2.shape[1]
    return pl.pallas_call(
        functools.partial(_dispatch_kernel, rows=rows),
        out_shape=jax.ShapeDtypeStruct((n, D), x2.dtype),
        grid_spec=pltpu.PrefetchScalarGridSpec(
            num_scalar_prefetch=1, grid=(n // rows,),
            in_specs=[pl.BlockSpec(memory_space=pl.ANY)],
            out_specs=pl.BlockSpec((rows, D), lambda t, idx: (t, 0)),
            scratch_shapes=[pltpu.SemaphoreType.DMA(())]),
        compiler_params=_cparams(("arbitrary",)),
        name="moe_dispatch",
    )(tok_of_slot, x2)


def _experts_kernel(te_ref, nu_ref, xs_ref, w1_ref, w3_ref, w2_ref, ys_ref, *, tf):
    t = pl.program_id(0)

    @pl.when(t < nu_ref[0])
    def _():
        xb = xs_ref[...].astype(BF16)
        acc = jnp.zeros(ys_ref.shape, F32)
        for f in range(w1_ref.shape[1] // tf):
            cols = slice(f * tf, (f + 1) * tf)
            a = jax.nn.silu(jnp.dot(xb, w1_ref[:, cols], preferred_element_type=F32))
            a = a * jnp.dot(xb, w3_ref[:, cols], preferred_element_type=F32)
            acc = acc + jnp.dot(a.astype(BF16), w2_ref[cols, :], preferred_element_type=F32)
        ys_ref[...] = acc

    @pl.when(t >= nu_ref[0])
    def _():
        ys_ref[...] = jnp.zeros(ys_ref.shape, F32)


def _experts(tile_expert, n_used, xs, w1, w3, w2, rows):
    P, D = xs.shape
    E, _, FF = w1.shape
    resident = lambda shape: pl.BlockSpec((None,) + shape, lambda t, te, nu: (te[t], 0, 0),
                                          pipeline_mode=pl.Buffered(1))
    return pl.pallas_call(
        functools.partial(_experts_kernel, tf=512),
        out_shape=jax.ShapeDtypeStruct((P, D), F32),
        grid_spec=pltpu.PrefetchScalarGridSpec(
            num_scalar_prefetch=2, grid=(P // rows,),
            in_specs=[pl.BlockSpec((rows, D), lambda t, te, nu: (t, 0)),
                      resident((D, FF)), resident((D, FF)), resident((FF, D))],
            out_specs=pl.BlockSpec((rows, D), lambda t, te, nu: (t, 0))),
        compiler_params=_cparams(("arbitrary",)),
        name="moe_experts",
    )(tile_expert, n_used, xs, w1, w3, w2)


def _combine_kernel(slot_ref, x_ref, route_ref, g_ref, b_ref, ys_hbm, y_ref, buf, sem, *, tm, alpha):
    _row_gather(slot_ref, pl.program_id(0) * 2 * tm, 2 * tm, ys_hbm, buf, sem)
    p1, p2 = route_ref[:, 2:3], route_ref[:, 3:4]
    f = p1 * buf[:tm, :] + p2 * buf[tm:, :]
    y_ref[...] = _layer_norm(alpha * x_ref[...] + f, g_ref[...], b_ref[...])


def _combine(slots, ys, x2, route, g, b, alpha, tm):
    T, D = x2.shape
    return pl.pallas_call(
        functools.partial(_combine_kernel, tm=tm, alpha=alpha),
        out_shape=jax.ShapeDtypeStruct((T, D), F32),
        grid_spec=pltpu.PrefetchScalarGridSpec(
            num_scalar_prefetch=1, grid=(T // tm,),
            in_specs=[pl.BlockSpec((tm, D), lambda i, s: (i, 0)),
                      pl.BlockSpec((tm, LANES), lambda i, s: (i, 0)),
                      pl.BlockSpec((1, D), lambda i, s: (0, 0)), pl.BlockSpec((1, D), lambda i, s: (0, 0)),
                      pl.BlockSpec(memory_space=pl.ANY)],
            out_specs=pl.BlockSpec((tm, D), lambda i, s: (i, 0)),
            scratch_shapes=[pltpu.VMEM((2 * tm, D), F32), pltpu.SemaphoreType.DMA(())]),
        compiler_params=_cparams(("arbitrary",)),
        name="moe_combine_ln",
    )(slots, x2, route, g.reshape(1, D), b.reshape(1, D), ys)


def _moe_ffn(x2, w_router, w1, w3, w2, g, b, alpha):
    T, D = x2.shape
    E = w1.shape[0]
    rows, tm = 256, 512
    gather_rows = 4 * rows
    route = _router(x2, w_router)

    e_flat = jnp.concatenate([route[:, 0], route[:, 1]]).astype(jnp.int32)
    onehot = (e_flat[:, None] == jnp.arange(E, dtype=jnp.int32)[None, :]).astype(jnp.int32)
    csum = jnp.cumsum(onehot, axis=0)
    counts = csum[-1]
    padded = (counts + rows - 1) // rows * rows
    ends = jnp.cumsum(padded)
    slot = jnp.sum(onehot * (csum - 1 + (ends - padded)[None, :]), axis=1)
    n_slots = 2 * T + E * rows
    tok_of_slot = jnp.zeros((n_slots,), jnp.int32).at[slot].set(jnp.arange(2 * T, dtype=jnp.int32) % T)
    tile_start = jnp.arange(n_slots // rows, dtype=jnp.int32) * rows
    tile_expert = jnp.minimum(jnp.sum((tile_start[:, None] >= ends[None, :]).astype(jnp.int32), axis=1), E - 1)
    n_used = (ends[-1:] // rows).astype(jnp.int32)

    assert n_slots % gather_rows == 0
    xs = _dispatch(tok_of_slot, x2, gather_rows)
    ys = _experts(tile_expert, n_used, xs, w1.astype(BF16), w3.astype(BF16), w2.astype(BF16), rows)
    slots = jnp.concatenate([slot[:T].reshape(T // tm, tm), slot[T:].reshape(T // tm, tm)], axis=1).reshape(-1)
    return _combine(slots, ys, x2, route, g, b, alpha, tm)


def kernel(x, positions, w_in, mla_q_norm, w_uq, mla_kv_norm, w_ukv, w_o_mla, w_o_dsa, w_out,
           ln1_g, ln1_b, ln2_g, ln2_b, dense_w1, dense_w3, dense_w2,
           moe_router, moe_w1, moe_w3, moe_w2):
    B, S, D = x.shape
    depth = w_in.shape[0]
    alpha = (2 * depth) ** 0.25
    T = B * S
    cos, sin = _rope_tables(positions)
    x2 = x.reshape(T, D)
    for l in range(depth):
        w_in_p, w_uq_p, w_ukv_p, w_gates = _prep_in_weights(w_in[l], w_uq[l], w_ukv[l])
        qm, km, vm, dq, dk, dv, iq, ik, iw = _projections(
            x2, cos, sin, w_in_p, w_uq_p, w_ukv_p, mla_q_norm[l], mla_kv_norm[l])
        b3 = lambda a: a.reshape(B, S, a.shape[-1])
        o_a = _mla_attention(b3(qm), b3(km), b3(vm)).reshape(T, -1)
        o_b = _sparse_attention(b3(dq), b3(dk), b3(dv), b3(iq), b3(ik), b3(iw)).reshape(T, -1)
        x2 = _merge(x2, o_a, o_b,
                    _pad_head_rows(w_o_mla[l], MLA_HEADS, MLA_V).astype(BF16),
                    _pad_head_rows(w_o_dsa[l], DSA_HEADS, DSA_HEAD_DIM).astype(BF16),
                    w_gates, w_out[l].astype(BF16), ln1_g[l], ln1_b[l], alpha)
        if l % 2 == 0:
            x2 = _dense_ffn(x2, dense_w1[l // 2], dense_w3[l // 2], dense_w2[l // 2], ln2_g[l], ln2_b[l], alpha)
        else:
            x2 = _moe_ffn(x2, moe_router[l // 2], moe_w1[l // 2], moe_w3[l // 2], moe_w2[l // 2],
                          ln2_g[l], ln2_b[l], alpha)
    return x2.reshape(B, S, D)
```

```python
import functools

import numpy as np
import jax
import jax.numpy as jnp
from jax import lax
from jax.experimental import pallas as pl
from jax.experimental.pallas import tpu as pltpu

F32 = jnp.float32
BF16 = jnp.bfloat16
LANES = 128
VMEM_LIMIT = 56 * 1024 * 1024

MLA_HEADS = 8
MLA_NOPE = 64
MLA_ROPE = 32
MLA_V = 64
Q_LORA = 384
KV_LORA = 256
DSA_HEADS = 8
DSA_HEAD_DIM = 64
IDX_HEADS = 8
IDX_DIM = 32
TOPK_MAX = 256
N_EXPERTS = 8
ROPE_THETA = 10000.0
NORM_EPS = 1e-5
NEG = -1e30
LOG2E = 1.4426950408889634
F32_MIN_NORMAL = 2.0 ** -126
INT_MIN = np.int32(-2 ** 31)
ONE_LANE = 64
assert ONE_LANE >= MLA_V and ONE_LANE >= DSA_HEAD_DIM

C_CQ = 0
C_CKV = C_CQ + Q_LORA
C_KR = C_CKV + KV_LORA
C_DQ = C_KR + LANES
C_DK = C_DQ + DSA_HEADS * DSA_HEAD_DIM
C_DV = C_DK + DSA_HEADS * DSA_HEAD_DIM
C_IQ = C_DV + DSA_HEADS * LANES
C_IK = C_IQ + IDX_HEADS * IDX_DIM
C_IW = C_IK + LANES
C_END = C_IW + LANES


def _cparams(sem):
    return pltpu.CompilerParams(dimension_semantics=sem, vmem_limit_bytes=VMEM_LIMIT)


def _const_spec(shape):
    nd = len(shape)
    return pl.BlockSpec(shape, lambda *_: (0,) * nd)


def _layer_norm(z, g, b):
    mu = jnp.mean(z, axis=-1, keepdims=True)
    zc = z - mu
    var = jnp.mean(zc * zc, axis=-1, keepdims=True)
    return zc * lax.rsqrt(var + NORM_EPS) * g + b


def _rope_tables_kernel(pos_ref, inv_ref, sgn_ref, cos_ref, sin_ref):
    pos = pos_ref[...].astype(F32)
    for p in range(2):
        ang = pos * inv_ref[p:p + 1, :]
        cos_ref[p] = jnp.cos(ang)
        sin_ref[p] = jnp.sin(ang) * sgn_ref[p:p + 1, :]


def _rope_tables(positions):
    T = positions.size
    tm = 1024
    inv32 = jnp.power(ROPE_THETA, -jnp.arange(0, IDX_DIM, 2, dtype=F32) / IDX_DIM)
    inv64 = jnp.power(ROPE_THETA, -jnp.arange(0, DSA_HEAD_DIM, 2, dtype=F32) / DSA_HEAD_DIM)
    inv = jnp.stack([jnp.tile(inv32, LANES // 16), jnp.tile(inv64, LANES // 32)])
    lane = np.arange(LANES)
    sgn = jnp.asarray(np.stack([np.where(lane % 32 < 16, -1.0, 1.0), np.where(lane % 64 < 32, -1.0, 1.0)]), F32)
    out = jax.ShapeDtypeStruct((2, T, LANES), F32)
    return pl.pallas_call(
        _rope_tables_kernel,
        out_shape=(out, out),
        grid=(T // tm,),
        in_specs=[pl.BlockSpec((tm, 1), lambda i: (i, 0)), _const_spec((2, LANES)), _const_spec((2, LANES))],
        out_specs=(pl.BlockSpec((2, tm, LANES), lambda i: (0, i, 0)),) * 2,
        compiler_params=_cparams(("parallel",)),
        name="rope_tables",
    )(positions.reshape(T, 1), inv, sgn)


def _proj_kernel(x_ref, cos_ref, sin_ref, w_in_ref, w_uq_ref, w_ukv_ref, gq_ref, gkv_ref,
                 qm_ref, km_ref, vm_ref, dq_ref, dk_ref, dv_ref, iq_ref, ik_ref, iw_ref, *, mla_scale):
    xb = x_ref[...].astype(BF16)
    lane = lax.broadcasted_iota(jnp.int32, (1, LANES), 1)
    cos32, sin32 = cos_ref[0], sin_ref[0]
    cos64, sin64 = cos_ref[1], sin_ref[1]
    in_rope = (lane >> 5) == (MLA_NOPE >> 5)
    cos_m = jnp.where(in_rope, cos32, 1.0)
    sin_m = jnp.where(in_rope, sin32, 0.0)
    first32 = (lane & (IDX_DIM - 1)) < IDX_DIM // 2
    first64 = (lane & (DSA_HEAD_DIM - 1)) < DSA_HEAD_DIM // 2
    wide_lane = lax.broadcasted_iota(jnp.int32, (1, MLA_HEADS * LANES), 1)
    one_col = jnp.where((wide_lane & (LANES - 1)) == ONE_LANE, 1.0, 0.0)

    def proj(lo, hi):
        return jnp.dot(xb, w_in_ref[:, lo:hi], preferred_element_type=F32)

    def rope(xs, cos, sin, half, first):
        rot = jnp.where(first, pltpu.roll(xs, LANES - half, 1), pltpu.roll(xs, half, 1))
        return xs * cos + rot * sin

    def rms(c, g):
        ms = jnp.mean(c * c, axis=-1, keepdims=True)
        return c * lax.rsqrt(ms + NORM_EPS) * g

    q = jnp.dot(rms(proj(C_CQ, C_CKV), gq_ref[...]).astype(BF16), w_uq_ref[...], preferred_element_type=F32)
    kv = jnp.dot(rms(proj(C_CKV, C_KR), gkv_ref[...]).astype(BF16), w_ukv_ref[...], preferred_element_type=F32)
    kr = rope(proj(C_KR, C_DQ), cos_m, sin_m, MLA_ROPE // 2, first32)
    for h in range(MLA_HEADS):
        sl = slice(h * LANES, (h + 1) * LANES)
        qm_ref[:, sl] = (rope(q[:, sl], cos_m, sin_m, MLA_ROPE // 2, first32) * mla_scale).astype(BF16)
        km_ref[:, sl] = (kv[:, sl] + kr).astype(BF16)
    vm_ref[...] = (kv[:, MLA_HEADS * LANES:] + one_col).astype(BF16)

    dq = proj(C_DQ, C_DK)
    dk = proj(C_DK, C_DV)
    for g in range(DSA_HEADS * DSA_HEAD_DIM // LANES):
        sl = slice(g * LANES, (g + 1) * LANES)
        dq_ref[:, sl] = (rope(dq[:, sl], cos64, sin64, DSA_HEAD_DIM // 2, first64) * LOG2E).astype(BF16)
        dk_ref[:, sl] = rope(dk[:, sl], cos64, sin64, DSA_HEAD_DIM // 2, first64).astype(BF16)
    dv_ref[...] = (proj(C_DV, C_IQ) + one_col).astype(BF16)

    iq = proj(C_IQ, C_IK)
    for g in range(IDX_HEADS * IDX_DIM // LANES):
        sl = slice(g * LANES, (g + 1) * LANES)
        iq_ref[:, sl] = rope(iq[:, sl], cos32, sin32, IDX_DIM // 2, first32).astype(BF16)
    ik_ref[...] = rope(proj(C_IK, C_IW), cos32, sin32, IDX_DIM // 2, first32).astype(BF16)
    iw_ref[...] = proj(C_IW, C_END)


def _pad_heads(w, nh, d):
    k = w.shape[0]
    return jnp.pad(w.reshape(k, nh, d), ((0, 0), (0, 0), (0, LANES - d))).reshape(k, nh * LANES)


def _prep_in_weights(w_in, w_uq, w_ukv):
    d = w_in.shape[0]
    o = np.cumsum([0, Q_LORA, KV_LORA, MLA_ROPE, 512, 512, 512, IDX_HEADS * IDX_DIM, IDX_DIM, IDX_HEADS])
    c_q, c_kv, k_r, dq, dk, dv, iq, ik, iw = [w_in[:, o[i]:o[i + 1]] for i in range(9)]
    gates = w_in[:, o[9]:]
    kr_tile = jnp.pad(k_r, ((0, 0), (MLA_NOPE, LANES - MLA_NOPE - MLA_ROPE)))
    dsa_scale = DSA_HEAD_DIM ** -0.5
    idx_scale = (IDX_HEADS * IDX_DIM) ** -0.5
    w_in_p = jnp.concatenate([
        c_q, c_kv, kr_tile,
        dq * dsa_scale, dk,
        _pad_heads(dv, DSA_HEADS, DSA_HEAD_DIM),
        iq, jnp.tile(ik, (1, LANES // IDX_DIM)),
        jnp.pad(iw * idx_scale, ((0, 0), (0, LANES - IDX_HEADS))),
    ], axis=1).astype(BF16)
    assert w_in_p.shape == (d, C_END)
    w_uq_p = _pad_heads(w_uq, MLA_HEADS, MLA_NOPE + MLA_ROPE).astype(BF16)
    ukv = w_ukv.reshape(KV_LORA, MLA_HEADS, MLA_NOPE + MLA_V)
    w_ukv_p = jnp.concatenate([
        _pad_heads(ukv[:, :, :MLA_NOPE].reshape(KV_LORA, -1), MLA_HEADS, MLA_NOPE),
        _pad_heads(ukv[:, :, MLA_NOPE:].reshape(KV_LORA, -1), MLA_HEADS, MLA_V),
    ], axis=1).astype(BF16)
    return w_in_p, w_uq_p, w_ukv_p, gates.astype(BF16)


def _projections(x2, cos, sin, w_in_p, w_uq_p, w_ukv_p, g_q, g_kv):
    T, D = x2.shape
    tm = 256
    wide = MLA_HEADS * LANES
    row = lambda w: pl.BlockSpec((tm, w), lambda i: (i, 0))
    packed = DSA_HEADS * DSA_HEAD_DIM
    shapes = [(wide, BF16)] * 3 + [(packed, BF16), (packed, BF16), (wide, BF16),
                                   (IDX_HEADS * IDX_DIM, BF16), (LANES, BF16), (LANES, F32)]
    return pl.pallas_call(
        functools.partial(_proj_kernel, mla_scale=(MLA_NOPE + MLA_ROPE) ** -0.5 * LOG2E),
        out_shape=tuple(jax.ShapeDtypeStruct((T, w), dt) for w, dt in shapes),
        grid=(T // tm,),
        in_specs=[row(D),
                  pl.BlockSpec((2, tm, LANES), lambda i: (0, i, 0)),
                  pl.BlockSpec((2, tm, LANES), lambda i: (0, i, 0)),
                  _const_spec(w_in_p.shape), _const_spec(w_uq_p.shape), _const_spec(w_ukv_p.shape),
                  _const_spec((1, Q_LORA)), _const_spec((1, KV_LORA))],
        out_specs=tuple(row(w) for w, _ in shapes),
        compiler_params=_cparams(("parallel",)),
        name="projections",
    )(x2, cos, sin, w_in_p, w_uq_p, w_ukv_p, g_q.reshape(1, -1), g_kv.reshape(1, -1))


SUBLANES = 8


def _sublane_tile_max(mv, s):
    parts = [mv] + [s[r * SUBLANES:(r + 1) * SUBLANES, :] for r in range(s.shape[0] // SUBLANES)]
    while len(parts) > 1:
        parts = [jnp.maximum(a, b) for a, b in zip(parts[::2], parts[1::2])] + ([parts[-1]] if len(parts) % 2 else [])
    return parts[0]


def _transpose_bf16(x):
    return x.astype(F32).T.astype(BF16)


def _online_softmax_chunk(logits, values, m_sc, acc_sc, p_sc, al_sc):
    for h, logit in enumerate(logits):
        s = logit()
        m_old = m_sc[h]
        mx = _sublane_tile_max(s[:SUBLANES, :], s[SUBLANES:, :])
        m_new = jnp.maximum(m_old, jnp.max(mx, axis=0, keepdims=True))
        p_sc[h] = jnp.exp2(s - m_new).astype(BF16)
        al_sc[h] = jnp.exp2(m_old - m_new)
        m_sc[h] = m_new
    for h, value in enumerate(values):
        pv = jnp.dot(_transpose_bf16(value()), p_sc[h], preferred_element_type=F32)
        acc_sc[h] = al_sc[h] * acc_sc[h] + pv


def _normalise_t(acc_t):
    return acc_t / acc_t[ONE_LANE:ONE_LANE + 1, :]


def _mla_attn_kernel(q_ref, k_ref, v_ref, o_ref, qt_sc, m_sc, acc_sc, p_sc, al_sc, *, tq):
    qi = pl.program_id(2)
    hg = qt_sc.shape[0]
    slabs = [slice(j * LANES, (j + 1) * LANES) for j in range(hg)]
    for j, sl in enumerate(slabs):
        qt_sc[j] = _transpose_bf16(q_ref[:, sl])
    m_sc[...] = jnp.full(m_sc.shape, NEG, F32)
    acc_sc[...] = jnp.zeros(acc_sc.shape, F32)

    def chunk(c, masked):
        off = pl.multiple_of(c * tq, tq)

        def logit(j):
            s = jnp.dot(k_ref[pl.ds(off, tq), slabs[j]], qt_sc[j], preferred_element_type=F32)
            if masked:
                krow = lax.broadcasted_iota(jnp.int32, (tq, tq), 0)
                qcol = lax.broadcasted_iota(jnp.int32, (tq, tq), 1)
                s = jnp.where(krow <= qcol, s, NEG)
            return s

        _online_softmax_chunk(
            [functools.partial(logit, j) for j in range(hg)],
            [functools.partial(lambda j: v_ref[pl.ds(off, tq), slabs[j]], j) for j in range(hg)],
            m_sc, acc_sc, p_sc, al_sc)

    def body(c, carry):
        chunk(c, False)
        return carry

    lax.fori_loop(0, qi, body, 0)
    chunk(qi, True)
    for j, sl in enumerate(slabs):
        o_ref[:, sl] = _normalise_t(acc_sc[j]).T.astype(o_ref.dtype)


def _mla_attention(q, k, v):
    B, S, W = q.shape
    tq, hg = 512, 4
    blk_q = pl.BlockSpec((None, tq, hg * LANES), lambda b, h, i: (b, i, h))
    blk_kv = pl.BlockSpec((None, S, hg * LANES), lambda b, h, i: (b, 0, h))
    return pl.pallas_call(
        functools.partial(_mla_attn_kernel, tq=tq),
        out_shape=jax.ShapeDtypeStruct((B, S, W), BF16),
        grid=(B, W // (hg * LANES), S // tq),
        in_specs=[blk_q, blk_kv, blk_kv],
        out_specs=blk_q,
        scratch_shapes=[pltpu.VMEM((hg, LANES, tq), BF16),
                        pltpu.VMEM((hg, 1, tq), F32),
                        pltpu.VMEM((hg, LANES, tq), F32),
                        pltpu.VMEM((hg, tq, tq), BF16),
                        pltpu.VMEM((hg, 1, tq), F32)],
        compiler_params=_cparams(("parallel", "parallel", "arbitrary")),
        name="mla_attention",
    )(q, k, v)


def _dsa_kernel(dq_ref, dk_ref, dv_ref, iq_ref, ik_ref, iw_ref, o_ref,
                key_sc, bias_sc, wt_sc, qmt_sc, qt_sc, m_sc, acc_sc, p_sc, al_sc,
                *, tq, ck, n_sel):
    qi = pl.program_id(1)
    nch = (qi * tq + tq + ck - 1) // ck
    lane = lax.broadcasted_iota(jnp.int32, (1, LANES), 1)
    k_sel = float(n_sel)

    wt_sc[...] = iw_ref[...].T
    for h in range(IDX_HEADS):
        g, lo = divmod(h * IDX_DIM, LANES)
        qg = iq_ref[:, g * LANES:(g + 1) * LANES].astype(F32)
        in_head = (lane >> 5) == (lo >> 5)
        qmt_sc[:, h * tq:(h + 1) * tq] = jnp.where(in_head, qg, 0.0).T.astype(BF16)

    kw = 2 * LANES
    qpos_t = qi * tq + lax.broadcasted_iota(jnp.int32, (kw, tq), 1)
    krow = lax.broadcasted_iota(jnp.int32, (kw, tq), 0)

    def score_body(c, carry):
        for part in range(ck // kw):
            ks = ik_ref[pl.ds(pl.multiple_of(c * ck + part * kw, kw), kw), :]
            d = jnp.dot(ks, qmt_sc[...], preferred_element_type=F32)
            score = jnp.zeros((kw, tq), F32)
            for h in range(IDX_HEADS):
                score = score + wt_sc[h:h + 1, :] * jnp.maximum(d[:, h * tq:(h + 1) * tq], 0.0)
            score = jnp.where(jnp.abs(score) < F32_MIN_NORMAL, 0.0, score)
            causal = c * ck + part * kw + krow <= qpos_t
            key_sc[c, part * kw:(part + 1) * kw, :] = jnp.where(causal, score, jnp.nan)
        return carry

    lax.fori_loop(0, nch, score_body, 0)

    sub = 8
    n_acc = 4

    def as_float(image):
        image = jnp.where(image > 0, jnp.maximum(image, np.int32(0x00800000)), image)
        return lax.bitcast_convert_type(jnp.where(image >= 0, image, image ^ np.int32(0x7FFFFFFF)), F32)

    def count_ge(image):
        thr_t = jnp.broadcast_to(as_float(image), (sub, tq))

        def body(c, accs):
            accs = list(accs)
            for r in range(ck // sub):
                ind = jnp.where(key_sc[c, r * sub:(r + 1) * sub, :] >= thr_t, 1.0, 0.0)
                accs[r % n_acc] = accs[r % n_acc] + ind
            return tuple(accs)

        accs = lax.fori_loop(0, nch, body, (jnp.zeros((sub, tq), F32),) * n_acc)
        return jnp.sum(sum(accs[1:], accs[0]), axis=0, keepdims=True)

    n_all = (nch * ck).astype(F32)
    n_nonneg = count_ge(jnp.zeros((1, tq), jnp.int32))
    tau0 = jnp.where(n_nonneg >= k_sel, np.int32(0), INT_MIN)
    n_ge0 = jnp.where(n_nonneg >= k_sel, n_nonneg, n_all)

    def tau_step(i, carry):
        tau, n_ge = carry
        cand = tau | (jnp.int32(1) << (30 - i))
        cnt = count_ge(cand)
        take = cnt >= k_sel
        return jnp.where(take, cand, tau), jnp.where(take, cnt, n_ge)

    tau, n_ge = lax.fori_loop(0, 31, tau_step, (tau0, n_ge0))
    open_below = tau == INT_MIN
    tau_f = jnp.where(open_below, -jnp.inf, as_float(tau))
    next_f = jnp.where(open_below, -jnp.inf, as_float(tau + 1))

    excess = n_ge - k_sel
    qpos_c = qi * tq + lax.broadcasted_iota(jnp.int32, (ck, tq), 1)
    krow_c = lax.broadcasted_iota(jnp.int32, (ck, tq), 0)
    tri = jnp.where(lax.broadcasted_iota(jnp.int32, (ck, ck), 0) <= lax.broadcasted_iota(jnp.int32, (ck, ck), 1),
                    1.0, 0.0).astype(BF16)

    def bias_body(i, ties_after):
        c = nch - 1 - i
        blk = key_sc[c]
        above = blk >= next_f
        at_least = blk >= tau_f
        is_tie = jnp.where(above, 0.0, jnp.where(at_least, 1.0, 0.0))
        suffix = jnp.dot(tri, is_tie.astype(BF16), preferred_element_type=F32)
        tie = jnp.where(at_least, jnp.where(ties_after + suffix > excess, 0.0, NEG), NEG)
        sel = jnp.where(above, 0.0, tie)
        bias_sc[c] = jnp.where(c * ck + krow_c <= qpos_c, sel, NEG)
        return ties_after + suffix[0:1, :]

    lax.fori_loop(0, nch, bias_body, jnp.zeros((1, tq), F32))

    for h in range(DSA_HEADS):
        g, lo = divmod(h * DSA_HEAD_DIM, LANES)
        qt = dq_ref[:, g * LANES:(g + 1) * LANES].astype(F32).T
        dim = lax.broadcasted_iota(jnp.int32, (LANES, tq), 0)
        qt_sc[h] = jnp.where((dim >> 6) == (lo >> 6), qt, 0.0).astype(BF16)
    m_sc[...] = jnp.full(m_sc.shape, NEG, F32)
    acc_sc[...] = jnp.zeros(acc_sc.shape, F32)

    def attn_body(c, carry):
        off = pl.multiple_of(c * ck, ck)
        bias = bias_sc[c]

        def logit(h):
            g = h * DSA_HEAD_DIM // LANES
            k_slab = dk_ref[pl.ds(off, ck), g * LANES:(g + 1) * LANES]
            return jnp.dot(k_slab, qt_sc[h], preferred_element_type=F32) + bias

        _online_softmax_chunk(
            [functools.partial(logit, h) for h in range(DSA_HEADS)],
            [functools.partial(lambda h: dv_ref[pl.ds(off, ck), h * LANES:(h + 1) * LANES], h) for h in range(DSA_HEADS)],
            m_sc, acc_sc, p_sc, al_sc)
        return carry

    lax.fori_loop(0, nch, attn_body, 0)
    for h in range(DSA_HEADS):
        o_ref[:, h * LANES:(h + 1) * LANES] = _normalise_t(acc_sc[h]).T.astype(o_ref.dtype)


def _sparse_attention(dq, dk, dv, iq, ik, iw):
    B, S, W = dv.shape
    wqk = dq.shape[-1]
    tq, ck = 256, 512
    n_sel = min(TOPK_MAX, S // 4)
    assert S & (S - 1) == 0 and S % ck == 0
    blk_q = lambda w: pl.BlockSpec((None, tq, w), lambda b, i: (b, i, 0))
    blk_k = lambda w: pl.BlockSpec((None, S, w), lambda b, i: (b, 0, 0), pipeline_mode=pl.Buffered(1))
    return pl.pallas_call(
        functools.partial(_dsa_kernel, tq=tq, ck=ck, n_sel=n_sel),
        out_shape=jax.ShapeDtypeStruct((B, S, W), BF16),
        grid=(B, S // tq),
        in_specs=[blk_q(wqk), blk_k(wqk), blk_k(W), blk_q(iq.shape[-1]), blk_k(LANES), blk_q(LANES)],
        out_specs=blk_q(W),
        scratch_shapes=[
            pltpu.VMEM((S // ck, ck, tq), F32),
            pltpu.VMEM((S // ck, ck, tq), F32),
            pltpu.VMEM((LANES, tq), F32),
            pltpu.VMEM((LANES, IDX_HEADS * tq), BF16),
            pltpu.VMEM((DSA_HEADS, LANES, tq), BF16),
            pltpu.VMEM((DSA_HEADS, 1, tq), F32),
            pltpu.VMEM((DSA_HEADS, LANES, tq), F32),
            pltpu.VMEM((DSA_HEADS, ck, tq), BF16),
            pltpu.VMEM((DSA_HEADS, 1, tq), F32)],
        compiler_params=_cparams(("parallel", "arbitrary")),
        name="sparse_attention",
    )(dq, dk, dv, iq, ik, iw)


def _merge_kernel(x_ref, oa_ref, ob_ref, woa_ref, wob_ref, wg_ref, wout_ref, g_ref, b_ref, y_ref, *, alpha):
    x = x_ref[...]
    xb = x.astype(BF16)
    d = x.shape[-1]
    ya = jnp.dot(oa_ref[...], woa_ref[...], preferred_element_type=F32)
    yb = jnp.dot(ob_ref[...], wob_ref[...], preferred_element_type=F32)
    ga = jax.nn.sigmoid(jnp.dot(xb, wg_ref[:, :d], preferred_element_type=F32))
    gb = jax.nn.sigmoid(jnp.dot(xb, wg_ref[:, d:], preferred_element_type=F32))
    merged = ga * ya + gb * yb
    h = jnp.dot(merged.astype(BF16), wout_ref[...], preferred_element_type=F32)
    y_ref[...] = _layer_norm(alpha * x + h, g_ref[...], b_ref[...])


def _pad_head_rows(w, nh, d):
    n = w.shape[1]
    return jnp.pad(w.reshape(nh, d, n), ((0, 0), (0, LANES - d), (0, 0))).reshape(nh * LANES, n)


def _merge(x2, oa, ob, woa, wob, wg, wout, g, b, alpha):
    T, D = x2.shape
    tm = 256
    row = lambda w: pl.BlockSpec((tm, w), lambda i: (i, 0))
    return pl.pallas_call(
        functools.partial(_merge_kernel, alpha=alpha),
        out_shape=jax.ShapeDtypeStruct((T, D), F32),
        grid=(T // tm,),
        in_specs=[row(D), row(oa.shape[1]), row(ob.shape[1]),
                  _const_spec(woa.shape), _const_spec(wob.shape), _const_spec(wg.shape), _const_spec(wout.shape),
                  _const_spec((1, D)), _const_spec((1, D))],
        out_specs=row(D),
        compiler_params=_cparams(("parallel",)),
        name="merge_outproj_ln",
    )(x2, oa, ob, woa, wob, wg, wout, g.reshape(1, D), b.reshape(1, D))


def _ffn_kernel(x_ref, w1_ref, w3_ref, w2_ref, g_ref, b_ref, y_ref, xb_sc, acc_sc, *, alpha):
    f = pl.program_id(1)

    @pl.when(f == 0)
    def _():
        xb_sc[...] = x_ref[...].astype(BF16)
        acc_sc[...] = jnp.zeros(acc_sc.shape, F32)

    xb = xb_sc[...]
    a = jax.nn.silu(jnp.dot(xb, w1_ref[...], preferred_element_type=F32))
    a = a * jnp.dot(xb, w3_ref[...], preferred_element_type=F32)
    acc_sc[...] += jnp.dot(a.astype(BF16), w2_ref[...], preferred_element_type=F32)

    @pl.when(f == pl.num_programs(1) - 1)
    def _():
        y_ref[...] = _layer_norm(alpha * x_ref[...] + acc_sc[...], g_ref[...], b_ref[...])


def _dense_ffn(x2, w1, w3, w2, g, b, alpha):
    T, D = x2.shape
    FF = w1.shape[1]
    tm, tf = 1024, 256
    assert FF % tf == 0
    return pl.pallas_call(
        functools.partial(_ffn_kernel, alpha=alpha),
        out_shape=jax.ShapeDtypeStruct((T, D), F32),
        grid=(T // tm, FF // tf),
        in_specs=[pl.BlockSpec((tm, D), lambda i, f: (i, 0)),
                  pl.BlockSpec((D, tf), lambda i, f: (0, f)),
                  pl.BlockSpec((D, tf), lambda i, f: (0, f)),
                  pl.BlockSpec((tf, D), lambda i, f: (f, 0)),
                  _const_spec((1, D)), _const_spec((1, D))],
        out_specs=pl.BlockSpec((tm, D), lambda i, f: (i, 0)),
        scratch_shapes=[pltpu.VMEM((tm, D), BF16), pltpu.VMEM((tm, D), F32)],
        compiler_params=_cparams(("parallel", "arbitrary")),
        name="dense_ffn_ln",
    )(x2, w1.astype(BF16), w3.astype(BF16), w2.astype(BF16), g.reshape(1, D), b.reshape(1, D))


def _router_kernel(x_ref, wr_ref, route_ref):
    logits = jnp.dot(x_ref[...], wr_ref[...], preferred_element_type=F32, precision=lax.Precision.HIGHEST)
    lane = lax.broadcasted_iota(jnp.int32, logits.shape, 1).astype(F32)
    logits = jnp.where(lane < N_EXPERTS, logits, -jnp.inf)
    v1 = jnp.max(logits, axis=1, keepdims=True)
    i1 = jnp.min(jnp.where(logits == v1, lane, float(LANES)), axis=1, keepdims=True)
    rest = jnp.where(lane == i1, -jnp.inf, logits)
    v2 = jnp.max(rest, axis=1, keepdims=True)
    i2 = jnp.min(jnp.where(rest == v2, lane, float(LANES)), axis=1, keepdims=True)
    e2 = jnp.exp(v2 - v1)
    p1 = 1.0 / (1.0 + e2)
    route_ref[...] = jnp.where(lane == 0.0, i1, jnp.where(lane == 1.0, i2, jnp.where(lane == 2.0, p1, e2 * p1)))


def _router(x2, w_router):
    T, D = x2.shape
    tm = 512
    wr = jnp.pad(w_router, ((0, 0), (0, LANES - N_EXPERTS)))
    return pl.pallas_call(
        _router_kernel,
        out_shape=jax.ShapeDtypeStruct((T, LANES), F32),
        grid=(T // tm,),
        in_specs=[pl.BlockSpec((tm, D), lambda i: (i, 0)), _const_spec(wr.shape)],
        out_specs=pl.BlockSpec((tm, LANES), lambda i: (i, 0)),
        compiler_params=_cparams(("parallel",)),
        name="router",
    )(x2, wr)


def _row_gather(idx_ref, base, n_rows, src_hbm, dst_ref, sem):
    def issue(r, c):
        row = idx_ref[base + r]
        pltpu.make_async_copy(src_hbm.at[pl.ds(row, 1), :], dst_ref.at[pl.ds(r, 1), :], sem).start()
        return c

    lax.fori_loop(0, n_rows, issue, 0, unroll=8)
    pltpu.make_async_copy(src_hbm.at[pl.ds(0, n_rows), :], dst_ref, sem).wait()


def _dispatch_kernel(idx_ref, src_hbm, out_ref, sem, *, rows):
    _row_gather(idx_ref, pl.program_id(0) * rows, rows, src_hbm, out_ref, sem)


def _dispatch(tok_of_slot, x2, rows):
    n, D = tok_of_slot.shape[0], x2.shape[1]
    return pl.pallas_call(
        functools.partial(_dispatch_kernel, rows=rows),
        out_shape=jax.ShapeDtypeStruct((n, D), x2.dtype),
        grid_spec=pltpu.PrefetchScalarGridSpec(
            num_scalar_prefetch=1, grid=(n // rows,),
            in_specs=[pl.BlockSpec(memory_space=pl.ANY)],
            out_specs=pl.BlockSpec((rows, D), lambda t, idx: (t, 0)),
            scratch_shapes=[pltpu.SemaphoreType.DMA(())]),
        compiler_params=_cparams(("arbitrary",)),
        name="moe_dispatch",
    )(tok_of_slot, x2)


def _experts_kernel(te_ref, nu_ref, xs_ref, w1_ref, w3_ref, w2_ref, ys_ref, *, tf):
    t = pl.program_id(0)

    @pl.when(t < nu_ref[0])
    def _():
        xb = xs_ref[...].astype(BF16)
        acc = jnp.zeros(ys_ref.shape, F32)
        for f in range(w1_ref.shape[1] // tf):
            cols = slice(f * tf, (f + 1) * tf)
            a = jax.nn.silu(jnp.dot(xb, w1_ref[:, cols], preferred_element_type=F32))
            a = a * jnp.dot(xb, w3_ref[:, cols], preferred_element_type=F32)
            acc = acc + jnp.dot(a.astype(BF16), w2_ref[cols, :], preferred_element_type=F32)
        ys_ref[...] = acc

    @pl.when(t >= nu_ref[0])
    def _():
        ys_ref[...] = jnp.zeros(ys_ref.shape, F32)


def _experts(tile_expert, n_used, xs, w1, w3, w2, rows):
    P, D = xs.shape
    E, _, FF = w1.shape
    resident = lambda shape: pl.BlockSpec((None,) + shape, lambda t, te, nu: (te[t], 0, 0),
                                          pipeline_mode=pl.Buffered(1))
    return pl.pallas_call(
        functools.partial(_experts_kernel, tf=512),
        out_shape=jax.ShapeDtypeStruct((P, D), F32),
        grid_spec=pltpu.PrefetchScalarGridSpec(
            num_scalar_prefetch=2, grid=(P // rows,),
            in_specs=[pl.BlockSpec((rows, D), lambda t, te, nu: (t, 0)),
                      resident((D, FF)), resident((D, FF)), resident((FF, D))],
            out_specs=pl.BlockSpec((rows, D), lambda t, te, nu: (t, 0))),
        compiler_params=_cparams(("arbitrary",)),
        name="moe_experts",
    )(tile_expert, n_used, xs, w1, w3, w2)


def _combine_kernel(slot_ref, x_ref, route_ref, g_ref, b_ref, ys_hbm, y_ref, buf, sem, *, tm, alpha):
    _row_gather(slot_ref, pl.program_id(0) * 2 * tm, 2 * tm, ys_hbm, buf, sem)
    p1, p2 = route_ref[:, 2:3], route_ref[:, 3:4]
    f = p1 * buf[:tm, :] + p2 * buf[tm:, :]
    y_ref[...] = _layer_norm(alpha * x_ref[...] + f, g_ref[...], b_ref[...])


def _combine(slots, ys, x2, route, g, b, alpha, tm):
    T, D = x2.shape
    return pl.pallas_call(
        functools.partial(_combine_kernel, tm=tm, alpha=alpha),
        out_shape=jax.ShapeDtypeStruct((T, D), F32),
        grid_spec=pltpu.PrefetchScalarGridSpec(
            num_scalar_prefetch=1, grid=(T // tm,),
            in_specs=[pl.BlockSpec((tm, D), lambda i, s: (i, 0)),
                      pl.BlockSpec((tm, LANES), lambda i, s: (i, 0)),
                      pl.BlockSpec((1, D), lambda i, s: (0, 0)), pl.BlockSpec((1, D), lambda i, s: (0, 0)),
                      pl.BlockSpec(memory_space=pl.ANY)],
            out_specs=pl.BlockSpec((tm, D), lambda i, s: (i, 0)),
            scratch_shapes=[pltpu.VMEM((2 * tm, D), F32), pltpu.SemaphoreType.DMA(())]),
        compiler_params=_cparams(("arbitrary",)),
        name="moe_combine_ln",
    )(slots, x2, route, g.reshape(1, D), b.reshape(1, D), ys)


def _moe_ffn(x2, w_router, w1, w3, w2, g, b, alpha):
    T, D = x2.shape
    E = w1.shape[0]
    rows, tm = 512, 512
    gather_rows = 2 * rows
    route = _router(x2, w_router)

    e_flat = jnp.concatenate([route[:, 0], route[:, 1]]).astype(jnp.int32)
    onehot = (e_flat[:, None] == jnp.arange(E, dtype=jnp.int32)[None, :]).astype(jnp.int32)
    csum = jnp.cumsum(onehot, axis=0)
    counts = csum[-1]
    padded = (counts + rows - 1) // rows * rows
    ends = jnp.cumsum(padded)
    slot = jnp.sum(onehot * (csum - 1 + (ends - padded)[None, :]), axis=1)
    n_slots = 2 * T + E * rows
    tok_of_slot = jnp.zeros((n_slots,), jnp.int32).at[slot].set(jnp.arange(2 * T, dtype=jnp.int32) % T)
    tile_start = jnp.arange(n_slots // rows, dtype=jnp.int32) * rows
    tile_expert = jnp.minimum(jnp.sum((tile_start[:, None] >= ends[None, :]).astype(jnp.int32), axis=1), E - 1)
    n_used = (ends[-1:] // rows).astype(jnp.int32)

    assert n_slots % gather_rows == 0
    xs = _dispatch(tok_of_slot, x2, gather_rows)
    ys = _experts(tile_expert, n_used, xs, w1.astype(BF16), w3.astype(BF16), w2.astype(BF16), rows)
    slots = jnp.concatenate([slot[:T].reshape(T // tm, tm), slot[T:].reshape(T // tm, tm)], axis=1).reshape(-1)
    return _combine(slots, ys, x2, route, g, b, alpha, tm)


def kernel(x, positions, w_in, mla_q_norm, w_uq, mla_kv_norm, w_ukv, w_o_mla, w_o_dsa, w_out,
           ln1_g, ln1_b, ln2_g, ln2_b, dense_w1, dense_w3, dense_w2,
           moe_router, moe_w1, moe_w3, moe_w2):
    B, S, D = x.shape
    depth = w_in.shape[0]
    alpha = (2 * depth) ** 0.25
    T = B * S
    cos, sin = _rope_tables(positions)
    x2 = x.reshape(T, D)
    for l in range(depth):
        w_in_p, w_uq_p, w_ukv_p, w_gates = _prep_in_weights(w_in[l], w_uq[l], w_ukv[l])
        qm, km, vm, dq, dk, dv, iq, ik, iw = _projections(
            x2, cos, sin, w_in_p, w_uq_p, w_ukv_p, mla_q_norm[l], mla_kv_norm[l])
        b3 = lambda a: a.reshape(B, S, a.shape[-1])
        o_a = _mla_attention(b3(qm), b3(km), b3(vm)).reshape(T, -1)
        o_b = _sparse_attention(b3(dq), b3(dk), b3(dv), b3(iq), b3(ik), b3(iw)).reshape(T, -1)
        x2 = _merge(x2, o_a, o_b,
                    _pad_head_rows(w_o_mla[l], MLA_HEADS, MLA_V).astype(BF16),
                    _pad_head_rows(w_o_dsa[l], DSA_HEADS, DSA_HEAD_DIM).astype(BF16),
                    w_gates, w_out[l].astype(BF16), ln1_g[l], ln1_b[l], alpha)
        if l % 2 == 0:
            x2 = _dense_ffn(x2, dense_w1[l // 2], dense_w3[l // 2], dense_w2[l // 2], ln2_g[l], ln2_b[l], alpha)
        else:
            x2 = _moe_ffn(x2, moe_router[l // 2], moe_w1[l // 2], moe_w3[l // 2], moe_w2[l // 2],
                          ln2_g[l], ln2_b[l], alpha)
    return x2.reshape(B, S, D)
```
